```python
import math, functools
import jax, jax.numpy as jnp
from jax import lax
import numpy as np

D_MODEL = 1024
BATCH = 8
SEQ = 4096
DEPTH = 2

GRID_W = 64
CTX_LEN = 256
HEAD_DIM = 64
ROPE_THETA = 10000.0
BLOCK = 128
S5_WIDTH = D_MODEL // 2
S5_GROUP = 16
S5_GROUPS = S5_WIDTH // S5_GROUP
S5_STATE = 64
WIN_HEADS = (D_MODEL // 2) // HEAD_DIM
WIN_KV_HEADS = WIN_HEADS // 4
WINDOW = 128
C_HEADS = D_MODEL // HEAD_DIM
C_KV_HEADS = C_HEADS // 4
N_EXPERTS = 32
N_GROUPS = 8
EXPERTS_PER_GROUP = N_EXPERTS // N_GROUPS
TOP_K = 2
EXPERT_FF = D_MODEL // 2
N_EVEN = (DEPTH + 1) // 2
N_ODD = DEPTH // 2
ALPHA = (2 * DEPTH) ** 0.25
BETA = (8 * DEPTH) ** -0.25
LN_EPS = 1e-5
RMS_EPS = 1e-6
NEG_INF = -1e30
EVEN_IN = S5_WIDTH + (WIN_HEADS + 2 * WIN_KV_HEADS) * HEAD_DIM
EVEN_MIX = S5_WIDTH + WIN_HEADS * HEAD_DIM
ODD_IN = (C_HEADS + 2 * C_KV_HEADS) * HEAD_DIM
ODD_MIX = C_HEADS * HEAD_DIM

kernel_name = 'hybrid_s5_swa_axialgqa_groupmoe_dit'


def layer_norm(x, g, b):
    xf = x.astype(jnp.float32)
    mu = jnp.mean(xf, -1, keepdims=True)
    xc = xf - mu
    var = jnp.mean(xc * xc, -1, keepdims=True)
    return (xc * lax.rsqrt(var + LN_EPS) * g + b).astype(x.dtype)


def rms_norm(x, g):
    xf = x.astype(jnp.float32)
    return (xf * lax.rsqrt(jnp.mean(xf * xf, -1, keepdims=True) + RMS_EPS) * g).astype(x.dtype)


def axial_rope_tables(rows):
    n_freq = HEAD_DIM // 4
    inv_freq = ROPE_THETA ** (-jnp.arange(n_freq, dtype=jnp.float32) / n_freq)
    r = jnp.repeat(jnp.arange(rows, dtype=jnp.float32), GRID_W)
    col = jnp.tile(jnp.arange(GRID_W, dtype=jnp.float32), rows)
    ang = jnp.concatenate([r[:, None] * inv_freq, col[:, None] * inv_freq], -1)
    return jnp.cos(ang), jnp.sin(ang)


def apply_rope(t, cos, sin):
    t1, t2 = jnp.split(t.astype(jnp.float32), 2, axis=-1)
    return jnp.concatenate([t1 * cos - t2 * sin, t2 * cos + t1 * sin], -1).astype(t.dtype)


def to_q_heads(t, n_kv, grp):
    b, n, _ = t.shape
    return t.reshape(b, n, n_kv, grp, HEAD_DIM).transpose(0, 2, 3, 1, 4)


def to_kv_heads(t, n_kv):
    b, n, _ = t.shape
    return t.reshape(b, n, n_kv, HEAD_DIM).transpose(0, 2, 1, 3)


def merge_heads(o):
    b, h, g, n, d = o.shape
    return o.transpose(0, 3, 1, 2, 4).reshape(b, n, h * g * d)


def full_attention(q, k, v, sink=None):
    bsz, hkv, grp = q.shape[:3]
    s = jnp.einsum('bhgqd,bhkd->bhgqk', q, k, preferred_element_type=jnp.float32) * (HEAD_DIM ** -0.5)
    if sink is None:
        p = jax.nn.softmax(s, axis=-1)
    else:
        sink_b = jnp.broadcast_to(sink.astype(jnp.float32).reshape(1, hkv, grp, 1, 1), s.shape[:-1] + (1,))
        p = jax.nn.softmax(jnp.concatenate([s, sink_b], -1), axis=-1)[..., :-1]
    return jnp.einsum('bhgqk,bhkd->bhgqd', p.astype(v.dtype), v)


def blocked_attention(q, k, v):
    bsz, hkv, grp, seq, hd = q.shape
    nb = seq // BLOCK
    qb = jnp.moveaxis(q.reshape(bsz, hkv, grp, nb, BLOCK, hd), 3, 0)
    ob = lax.map(lambda qblk: full_attention(qblk, k, v), qb)
    return jnp.moveaxis(ob, 0, 3).reshape(bsz, hkv, grp, seq, hd)


def window_attention(q, k, v, kc, vc, sink):
    bsz, hkv, grp, seq, hd = q.shape
    nb = seq // BLOCK
    qb = q.reshape(bsz, hkv, grp, nb, BLOCK, hd)

    def band(t):
        tp = jnp.pad(t, ((0, 0), (0, 0), (BLOCK, BLOCK), (0, 0))).reshape(bsz, hkv, nb + 2, BLOCK, hd)
        return jnp.concatenate([tp[:, :, :-2], tp[:, :, 1:-1], tp[:, :, 2:]], axis=3)

    kb, vb = band(k), band(v)
    scale = HEAD_DIM ** -0.5
    s_loc = jnp.einsum('bhgnqd,bhnkd->bhgnqk', qb, kb, preferred_element_type=jnp.float32) * scale
    s_ctx = jnp.einsum('bhgnqd,bhkd->bhgnqk', qb, kc, preferred_element_type=jnp.float32) * scale
    q_pos = jnp.arange(seq).reshape(nb, BLOCK, 1)
    k_pos = ((jnp.arange(nb)[:, None] - 1) * BLOCK + jnp.arange(3 * BLOCK)[None, :])[:, None, :]
    valid = (jnp.abs(q_pos - k_pos) <= WINDOW) & (k_pos >= 0) & (k_pos < seq)
    s_loc = jnp.where(valid, s_loc, NEG_INF)
    sink_b = jnp.broadcast_to(sink.astype(jnp.float32).reshape(1, hkv, grp, 1, 1, 1), s_ctx.shape[:-1] + (1,))
    p = jax.nn.softmax(jnp.concatenate([s_ctx, s_loc, sink_b], -1), axis=-1).astype(v.dtype)
    n_ctx = kc.shape[2]
    out = (jnp.einsum('bhgnqk,bhkd->bhgnqd', p[..., :n_ctx], vc)
           + jnp.einsum('bhgnqk,bhnkd->bhgnqd', p[..., n_ctx:-1], vb))
    return out.reshape(bsz, hkv, grp, seq, hd)


def s5_discretize(lam_re, lam_im, log_step, b_re, b_im):
    lr, li = lam_re.astype(jnp.float32), lam_im.astype(jnp.float32)
    dt = jnp.exp(log_step.astype(jnp.float32))[:, None]
    mag = jnp.exp(lr * dt)
    ab_re, ab_im = mag * jnp.cos(li * dt), mag * jnp.sin(li * dt)
    den = lr * lr + li * li
    num_re, num_im = ab_re - 1.0, ab_im
    f_re = (num_re * lr + num_im * li) / den
    f_im = (num_im * lr - num_re * li) / den
    br, bi = b_re.astype(jnp.float32), b_im.astype(jnp.float32)
    bb_re = f_re[..., None] * br - f_im[..., None] * bi
    bb_im = f_re[..., None] * bi + f_im[..., None] * br
    return ab_re, ab_im, bb_re, bb_im


def complex_affine_combine(e1, e2):
    a1r, a1i, b1r, b1i = e1
    a2r, a2i, b2r, b2i = e2
    return (a2r * a1r - a2i * a1i, a2r * a1i + a2i * a1r,
            a2r * b1r - a2i * b1i + b2r, a2r * b1i + a2i * b1r + b2i)


def s5_scan(u, ab_re, ab_im, bb_re, bb_im, h0_re, h0_im, reverse):
    bu_re = jnp.einsum('tgc,gpc->tgp', u, bb_re)
    bu_im = jnp.einsum('tgc,gpc->tgp', u, bb_im)
    a_re = jnp.broadcast_to(ab_re, bu_re.shape)
    a_im = jnp.broadcast_to(ab_im, bu_re.shape)
    pw_re, pw_im, s_re, s_im = lax.associative_scan(complex_affine_combine, (a_re, a_im, bu_re, bu_im),
                                                    reverse=reverse, axis=0)
    s_re = s_re + pw_re * h0_re - pw_im * h0_im
    s_im = s_im + pw_re * h0_im + pw_im * h0_re
    return s_re, s_im


def s5_readout(s_re, s_im, c_re, c_im):
    return (jnp.einsum('btgp,gcp->btgc', s_re, c_re.astype(jnp.float32))
            - jnp.einsum('btgp,gcp->btgc', s_im, c_im.astype(jnp.float32)))


def s5_mixer(ul, uc, lam_re, lam_im, log_step, b_re, b_im, c_re, c_im, d_skip, w_glu, b_glu, need_ctx):
    bsz, seq, _ = ul.shape
    n_ctx = uc.shape[1]
    ulg = ul.astype(jnp.float32).reshape(bsz, seq, S5_GROUPS, S5_GROUP)
    ucg = uc.astype(jnp.float32).reshape(bsz, n_ctx, S5_GROUPS, S5_GROUP)
    zeros = jnp.zeros((bsz, S5_GROUPS, S5_STATE), jnp.float32)
    y_lat = ul.astype(jnp.float32) * d_skip
    y_ctx = uc.astype(jnp.float32) * d_skip
    for direction, reverse in ((0, False), (1, True)):
        ab_re, ab_im, bb_re, bb_im = s5_discretize(lam_re[direction], lam_im[direction], log_step[direction],
                                                   b_re[direction], b_im[direction])
        scan = jax.vmap(functools.partial(s5_scan, reverse=reverse), in_axes=(0, None, None, None, None, 0, 0))
        sc_re, sc_im = scan(ucg, ab_re, ab_im, bb_re, bb_im, zeros, zeros)
        end = 0 if reverse else -1
        sl_re, sl_im = scan(ulg, ab_re, ab_im, bb_re, bb_im, sc_re[:, end], sc_im[:, end])
        y_lat = y_lat + s5_readout(sl_re, sl_im, c_re[direction], c_im[direction]).reshape(bsz, seq, S5_WIDTH)
        if need_ctx:
            y_ctx = y_ctx + s5_readout(sc_re, sc_im, c_re[direction], c_im[direction]).reshape(bsz, n_ctx, S5_WIDTH)

    def glu(y):
        z = jax.nn.gelu(y)
        return z * jax.nn.sigmoid(jnp.dot(z, w_glu) + b_glu)

    out_ctx = glu(y_ctx).astype(uc.dtype) if need_ctx else None
    return glu(y_lat).astype(ul.dtype), out_ctx


def mixer_ab(hl, hc, cos, sin, w_in, w_out, lam_re, lam_im, log_step, b_re, b_im, c_re, c_im,
             d_skip, w_glu, b_glu, sink, need_ctx):
    grp = WIN_HEADS // WIN_KV_HEADS
    cuts = [S5_WIDTH, S5_WIDTH + WIN_HEADS * HEAD_DIM, S5_WIDTH + (WIN_HEADS + WIN_KV_HEADS) * HEAD_DIM]
    ul, ql, kl, vl = jnp.split(jnp.dot(hl, w_in), cuts, axis=-1)
    uc, qc, kc, vc = jnp.split(jnp.dot(hc, w_in), cuts, axis=-1)
    a_lat, a_ctx = s5_mixer(ul, uc, lam_re, lam_im, log_step, b_re, b_im, c_re, c_im,
                            d_skip, w_glu, b_glu, need_ctx)
    q = apply_rope(to_q_heads(ql, WIN_KV_HEADS, grp), cos, sin)
    k = apply_rope(to_kv_heads(kl, WIN_KV_HEADS), cos, sin)
    v = to_kv_heads(vl, WIN_KV_HEADS)
    kch = to_kv_heads(kc, WIN_KV_HEADS)
    vch = to_kv_heads(vc, WIN_KV_HEADS)
    o_lat = merge_heads(window_attention(q, k, v, kch, vch, sink))
    out_lat = jnp.dot(jnp.concatenate([a_lat, o_lat], -1), w_out)
    out_ctx = None
    if need_ctx:
        o_ctx = merge_heads(full_attention(to_q_heads(qc, WIN_KV_HEADS, grp), kch, vch, sink))
        out_ctx = jnp.dot(jnp.concatenate([a_ctx, o_ctx], -1), w_out)
    return out_lat, out_ctx


def mixer_c(hl, hc, cos, sin, w_in, w_out, q_norm, k_norm, need_ctx):
    grp = C_HEADS // C_KV_HEADS
    cuts = [C_HEADS * HEAD_DIM, (C_HEADS + C_KV_HEADS) * HEAD_DIM]

    def project(h):
        q, k, v = jnp.split(jnp.dot(h, w_in), cuts, axis=-1)
        return (rms_norm(to_q_heads(q, C_KV_HEADS, grp), q_norm),
                rms_norm(to_kv_heads(k, C_KV_HEADS), k_norm),
                to_kv_heads(v, C_KV_HEADS))

    ql, kl, vl = project(hl)
    qc, kc, vc = project(hc)
    ql = apply_rope(ql, cos, sin)
    kl = apply_rope(kl, cos, sin)
    k_all = jnp.concatenate([kc, kl], axis=2)
    v_all = jnp.concatenate([vc, vl], axis=2)
    out_lat = jnp.dot(merge_heads(blocked_attention(ql, k_all, v_all)), w_out)
    out_ctx = jnp.dot(merge_heads(full_attention(qc, kc, vc)), w_out) if need_ctx else None
    return out_lat, out_ctx


def moe_ffn(t, router_w, router_bias, w_gate, w_up, w_down):
    n_tok = t.shape[0]
    scores = jax.nn.sigmoid(jnp.dot(t, router_w, preferred_element_type=jnp.float32))
    sel = (scores + router_bias.astype(jnp.float32)).reshape(n_tok, N_GROUPS, EXPERTS_PER_GROUP)
    grp_score = lax.top_k(sel, TOP_K)[0].sum(-1)
    g_idx = jnp.argmax(grp_score, axis=-1)
    in_grp = sel[jnp.arange(n_tok), g_idx]
    _, e_local = lax.top_k(in_grp, TOP_K)
    e_idx = g_idx[:, None] * EXPERTS_PER_GROUP + e_local
    w = jnp.take_along_axis(scores, e_idx, axis=1)
    w = w / jnp.sum(w, -1, keepdims=True)
    combine = jnp.einsum('nk,nke->ne', w, jax.nn.one_hot(e_idx, N_EXPERTS, dtype=jnp.float32))
    out = jnp.zeros(t.shape, jnp.float32)
    for e in range(N_EXPERTS):
        hid = jax.nn.silu(jnp.dot(t, w_gate[e])) * jnp.dot(t, w_up[e])
        out = out + combine[:, e:e + 1] * jnp.dot(hid, w_down[e], preferred_element_type=jnp.float32)
    return out.astype(t.dtype)


def setup_inputs(seed: int = 0) -> dict:
    key = jax.random.key(seed)
    ks = jax.random.split(key, 32)
    f32 = jnp.float32
    D = D_MODEL

    def nrm(k, shape, s):
        return jax.random.normal(k, shape, f32) * s

    s5_shape = (N_EVEN, 2, S5_GROUPS, S5_STATE)
    n_idx = jnp.arange(S5_STATE, dtype=f32)
    return {
        'x': nrm(ks[0], (BATCH, SEQ, D), 1.0),
        'c': nrm(ks[1], (BATCH, D), 1.0),
        'ctx': nrm(ks[2], (BATCH, CTX_LEN, D), 1.0),
        'c_ctx': nrm(ks[3], (D,), 1.0),
        'ada_w': nrm(ks[4], (DEPTH, D, 6 * D), 0.5 * D ** -0.5),
        'ada_b': nrm(ks[5], (DEPTH, 6 * D), 0.02),
        'ln_g': 1.0 + nrm(ks[6], (DEPTH, 2, D), 0.02),
        'ln_b': nrm(ks[7], (DEPTH, 2, D), 0.02),
        'even_w_in': nrm(ks[8], (N_EVEN, D, EVEN_IN), D ** -0.5),
        'even_w_out': nrm(ks[9], (N_EVEN, EVEN_MIX, D), BETA * EVEN_MIX ** -0.5),
        's5_lam_re': -0.5 + nrm(ks[10], s5_shape, 0.01),
        's5_lam_im': math.pi * n_idx + nrm(ks[11], s5_shape, 0.01),
        's5_log_step': jax.random.uniform(ks[12], (N_EVEN, 2, S5_GROUPS), f32, math.log(1e-3), math.log(1e-1)),
        's5_b_re': nrm(ks[13], (N_EVEN, 2, S5_GROUPS, S5_STATE, S5_GROUP), (2 * S5_GROUP) ** -0.5),
        's5_b_im': nrm(ks[14], (N_EVEN, 2, S5_GROUPS, S5_STATE, S5_GROUP), (2 * S5_GROUP) ** -0.5),
        's5_c_re': nrm(ks[15], (N_EVEN, 2, S5_GROUPS, S5_GROUP, S5_STATE), 0.5),
        's5_c_im': nrm(ks[16], (N_EVEN, 2, S5_GROUPS, S5_GROUP, S5_STATE), 0.5),
        's5_d': nrm(ks[17], (N_EVEN, S5_WIDTH), 1.0),
        's5_w_glu': nrm(ks[18], (N_EVEN, S5_WIDTH, S5_WIDTH), S5_WIDTH ** -0.5),
        's5_b_glu': nrm(ks[19], (N_EVEN, S5_WIDTH), 0.02),
        'win_sink': nrm(ks[20], (N_EVEN, WIN_HEADS), 0.5),
        'odd_w_in': nrm(ks[21], (N_ODD, D, ODD_IN), D ** -0.5),
        'odd_w_out': nrm(ks[22], (N_ODD, ODD_MIX, D), BETA * ODD_MIX ** -0.5),
        'odd_q_norm': 1.0 + nrm(ks[23], (N_ODD, HEAD_DIM), 0.02),
        'odd_k_norm': 1.0 + nrm(ks[24], (N_ODD, HEAD_DIM), 0.02),
        'router_w': nrm(ks[25], (D, N_EXPERTS), D ** -0.5),
        'router_bias': nrm(ks[26], (N_EXPERTS,), 0.01),
        'moe_w_gate': nrm(ks[27], (DEPTH, N_EXPERTS, D, EXPERT_FF), D ** -0.5),
        'moe_w_up': nrm(ks[28], (DEPTH, N_EXPERTS, D, EXPERT_FF), D ** -0.5),
        'moe_w_down': nrm(ks[29], (DEPTH, N_EXPERTS, EXPERT_FF, D), BETA * EXPERT_FF ** -0.5),
    }


def reference(x, c, ctx, c_ctx, ada_w, ada_b, ln_g, ln_b, even_w_in, even_w_out,
              s5_lam_re, s5_lam_im, s5_log_step, s5_b_re, s5_b_im, s5_c_re, s5_c_im, s5_d,
              s5_w_glu, s5_b_glu, win_sink, odd_w_in, odd_w_out, odd_q_norm, odd_k_norm,
              router_w, router_bias, moe_w_gate, moe_w_up, moe_w_down):
    bsz, seq, dm = x.shape
    rows = seq // GRID_W
    cos, sin = axial_rope_tables(rows)
    silu_c = jax.nn.silu(c)
    silu_cc = jax.nn.silu(c_ctx)
    n_lat = bsz * seq
    xl, xc = x, ctx
    for i in range(DEPTH):
        need_ctx = i < DEPTH - 1
        j = i // 2
        mod_l = jnp.split((jnp.dot(silu_c, ada_w[i]) + ada_b[i])[:, None, :], 6, axis=-1)
        mod_c = jnp.split(jnp.dot(silu_cc, ada_w[i]) + ada_b[i], 6, axis=-1)
        hl = xl * (1.0 + mod_l[1]) + mod_l[0]
        hc = xc * (1.0 + mod_c[1]) + mod_c[0]
        if i % 2 == 0:
            ol, oc = mixer_ab(hl, hc, cos, sin, even_w_in[j], even_w_out[j], s5_lam_re[j], s5_lam_im[j],
                              s5_log_step[j], s5_b_re[j], s5_b_im[j], s5_c_re[j], s5_c_im[j], s5_d[j],
                              s5_w_glu[j], s5_b_glu[j], win_sink[j], need_ctx)
        else:
            ol, oc = mixer_c(hl, hc, cos, sin, odd_w_in[j], odd_w_out[j], odd_q_norm[j], odd_k_norm[j],
                             need_ctx)
        xl = layer_norm(ALPHA * xl + mod_l[2] * ol, ln_g[i, 0], ln_b[i, 0])
        hl = xl * (1.0 + mod_l[4]) + mod_l[3]
        if need_ctx:
            xc = layer_norm(ALPHA * xc + mod_c[2] * oc, ln_g[i, 0], ln_b[i, 0])
            hc = xc * (1.0 + mod_c[4]) + mod_c[3]
            tokens = jnp.concatenate([hl.reshape(-1, dm), hc.reshape(-1, dm)], axis=0)
        else:
            tokens = hl.reshape(-1, dm)
        f = moe_ffn(tokens, router_w, router_bias, moe_w_gate[i], moe_w_up[i], moe_w_down[i])
        xl = layer_norm(ALPHA * xl + mod_l[5] * f[:n_lat].reshape(bsz, seq, dm), ln_g[i, 1], ln_b[i, 1])
        if need_ctx:
            xc = layer_norm(ALPHA * xc + mod_c[5] * f[n_lat:].reshape(xc.shape), ln_g[i, 1], ln_b[i, 1])
    return xl
```

```python
import functools
import math

import jax
import jax.numpy as jnp
from jax import lax
from jax.experimental import pallas as pl
from jax.experimental.pallas import tpu as pltpu

F32 = jnp.float32
BF16 = jnp.bfloat16

HEAD_DIM = 64
GRID_W = 64
ROPE_THETA = 10000.0
S5_GROUP = 16
S5_STATE = 64
WINDOW = 128
N_EXPERTS = 32
N_GROUPS = 8
EXPERTS_PER_GROUP = N_EXPERTS // N_GROUPS
TOP_K = 2
DEPTH = 2
ALPHA = (2 * DEPTH) ** 0.25
LN_EPS = 1e-5
RMS_EPS = 1e-6
NEG_INF = -1e30

ROW_TILE = 256
S5_CHUNK = 64
S5_LANE_GROUP = 512
EXPERT_TILE = 256
GATHER_CHUNK = 2048
VMEM_LIMIT = 48 * 1024 * 1024


def _params(sem):
    return pltpu.CompilerParams(dimension_semantics=sem, vmem_limit_bytes=VMEM_LIMIT)


def _full(shape):
    n = len(shape)
    return pl.BlockSpec(shape, lambda *_: (0,) * n)


def _ada_kernel(c_ref, w_ref, b_ref, o_ref):
    c = c_ref[...]
    s = c * jax.nn.sigmoid(c)
    o_ref[...] = jnp.dot(s, w_ref[0], preferred_element_type=F32, precision=lax.Precision.HIGHEST) + b_ref[0]


def _ada(cond, ada_w, ada_b):
    g, d = cond.shape
    depth, _, n = ada_w.shape
    bn = 1024
    return pl.pallas_call(
        _ada_kernel,
        grid=(depth, n // bn),
        in_specs=[pl.BlockSpec((g, d), lambda i, j: (0, 0)),
                  pl.BlockSpec((1, d, bn), lambda i, j: (i, 0, j)),
                  pl.BlockSpec((1, 1, bn), lambda i, j: (i, 0, j))],
        out_specs=pl.BlockSpec((None, g, bn), lambda i, j: (i, 0, j)),
        out_shape=jax.ShapeDtypeStruct((depth, g, n), F32),
        compiler_params=_params(("arbitrary", "arbitrary")),
        name="ada",
    )(cond, ada_w, ada_b.reshape(depth, 1, n))


class _Rows:
    def __init__(self, bsz, seq, ctx, tm=ROW_TILE):
        assert seq % tm == 0 and ctx % tm == 0
        self.bsz, self.seq, self.ctx, self.tm = bsz, seq, ctx, tm
        self.tpb = seq // tm
        self.cpb = ctx // tm
        self.nl = bsz * self.tpb
        self.nc = bsz * self.cpb
        self.n_lat = bsz * seq
        self.n_all = bsz * (seq + ctx)

    def group(self, i):
        return jnp.where(i < self.nl, i // self.tpb, self.bsz)

    def rope_block(self, i):
        return jnp.where(i < self.nl, i % self.tpb, self.tpb)

    def time_major(self, i):
        lat = i < self.nl
        j = i - self.nl
        return (jnp.where(lat, self.cpb + i % self.tpb, j % self.cpb),
                jnp.where(lat, i // self.tpb, j // self.cpb))


def _layer_norm(r, g, b):
    mu = jnp.mean(r, axis=-1, keepdims=True)
    rc = r - mu
    var = jnp.mean(rc * rc, axis=-1, keepdims=True)
    return rc * lax.rsqrt(var + LN_EPS) * g + b


def _even_in_kernel(x_ref, mod_ref, w_ref, cos_ref, sin_ref, u32_ref, utm_ref, q_ref, k_ref, v_ref, *, s5w, qw, kw):
    m = mod_ref[0]
    h = (x_ref[...] * (1.0 + m[1:2, :]) + m[0:1, :]).astype(BF16)
    r = jnp.dot(h, w_ref[...], preferred_element_type=F32)
    u = r[:, :s5w]
    u32_ref[...] = u
    utm_ref[...] = u.astype(BF16)
    o = s5w
    q, q_sw = r[:, o:o + qw], r[:, o + qw:o + 2 * qw]
    o += 2 * qw
    k, k_sw = r[:, o:o + kw], r[:, o + kw:o + 2 * kw]
    o += 2 * kw
    v = r[:, o:o + kw]
    cos, sin = cos_ref[...], sin_ref[...]
    cq = jnp.concatenate([cos] * (qw // 128), axis=1)
    sq = jnp.concatenate([sin] * (qw // 128), axis=1)
    q_ref[...] = ((q * cq + q_sw * sq) * (HEAD_DIM ** -0.5)).astype(BF16)
    k_ref[...] = (k * cos + k_sw * sin).astype(BF16)
    v_ref[...] = v.astype(BF16)


def _even_in(rows, x_all, mods, w_cat, cos_t, sin_t, s5w, qw, kw):
    tm, d = rows.tm, x_all.shape[1]
    n = rows.n_all
    t_all = rows.seq + rows.ctx
    kern = functools.partial(_even_in_kernel, s5w=s5w, qw=qw, kw=kw)
    row = lambda w: pl.BlockSpec((tm, w), lambda i: (i, 0))
    return pl.pallas_call(
        kern,
        grid=(n // tm,),
        in_specs=[row(d),
                  pl.BlockSpec((1, 6, d), lambda i: (rows.group(i), 0, 0)),
                  _full(w_cat.shape),
                  pl.BlockSpec((tm, 128), lambda i: (rows.rope_block(i), 0)),
                  pl.BlockSpec((tm, 128), lambda i: (rows.rope_block(i), 0))],
        out_specs=[row(s5w),
                   pl.BlockSpec((tm, s5w), lambda i: rows.time_major(i)),
                   row(qw), row(kw), row(kw)],
        out_shape=[jax.ShapeDtypeStruct((n, s5w), F32),
                   jax.ShapeDtypeStruct((t_all, rows.bsz * s5w), BF16),
                   jax.ShapeDtypeStruct((n, qw), BF16),
                   jax.ShapeDtypeStruct((n, kw), BF16),
                   jax.ShapeDtypeStruct((n, kw), BF16)],
        compiler_params=_params(("parallel",)),
        name="even_in",
    )(x_all, mods, w_cat, cos_t, sin_t)


def _s5_kernel(u_ref, wb_ref, a_ref, wc_ref, y_ref, bu_ref, h_ref, *, tc, nb, width, n_state):
    lg = S5_LANE_GROUP
    n_lg = n_state // lg
    kch = width // n_lg
    d = pl.program_id(0)

    @pl.when(pl.program_id(1) == 0)
    def _():
        h_ref[...] = jnp.zeros_like(h_ref)

    u = u_ref[...]
    for g in range(n_lg):
        r = jnp.dot(u[:, kch * g:kch * (g + 1)], wb_ref[0, g], preferred_element_type=F32)
        bu_ref[:, lg * g:lg * (g + 1)] = r[:, :lg]
        bu_ref[:, n_state + lg * g:n_state + lg * (g + 1)] = r[:, lg:]

    rev = d == 1
    for g in range(n_lg):
        re = slice(lg * g, lg * (g + 1))
        im = slice(n_state + lg * g, n_state + lg * (g + 1))
        a_re = a_ref[0, :, re]
        a_im = a_ref[0, :, im]

        def body(i, carry, re=re, im=im, a_re=a_re, a_im=a_im):
            hr, hi = carry
            t = jnp.where(rev, tc - 1 - i, i)
            row = pl.multiple_of(t * nb, nb)
            br = bu_ref[pl.ds(row, nb), re]
            bi = bu_ref[pl.ds(row, nb), im]
            nr = a_re * hr - a_im * hi + br
            ni = a_re * hi + a_im * hr + bi
            bu_ref[pl.ds(row, nb), re] = nr
            bu_ref[pl.ds(row, nb), im] = ni
            return nr, ni

        hr, hi = lax.fori_loop(0, tc, body, (h_ref[:, re], h_ref[:, im]), unroll=4)
        h_ref[:, re] = hr
        h_ref[:, im] = hi

    for g in range(n_lg):
        hc = jnp.concatenate([bu_ref[:, lg * g:lg * (g + 1)],
                              bu_ref[:, n_state + lg * g:n_state + lg * (g + 1)]], axis=1).astype(BF16)
        y_ref[0, :, kch * g:kch * (g + 1)] = jnp.dot(hc, wc_ref[0, g], preferred_element_type=F32)


def _s5(u_tm, wb, a_b, wc, nb, ctx, seq):
    tc = S5_CHUNK
    t_all = ctx + seq
    width = u_tm.shape[1]
    n_state = a_b.shape[2] // 2
    assert ctx % tc == 0 and seq % tc == 0 and nb == 8
    n_ctx, n_chunks = ctx // tc, t_all // tc

    def chunk(d, s):
        back = jnp.where(s < n_ctx, n_ctx - 1 - s, n_chunks - 1 - (s - n_ctx))
        return jnp.where(d == 0, s, back)

    kern = functools.partial(_s5_kernel, tc=tc, nb=nb, width=width, n_state=n_state)
    return pl.pallas_call(
        kern,
        grid=(2, n_chunks),
        in_specs=[pl.BlockSpec((tc * nb, width), lambda d, s: (chunk(d, s), 0)),
                  pl.BlockSpec((1,) + wb.shape[1:], lambda d, s: (d, 0, 0, 0)),
                  pl.BlockSpec((1,) + a_b.shape[1:], lambda d, s: (d, 0, 0)),
                  pl.BlockSpec((1,) + wc.shape[1:], lambda d, s: (d, 0, 0, 0))],
        out_specs=pl.BlockSpec((1, tc * nb, width), lambda d, s: (d, chunk(d, s), 0)),
        out_shape=jax.ShapeDtypeStruct((2, t_all * nb, width), F32),
        scratch_shapes=[pltpu.VMEM((tc * nb, 2 * n_state), F32), pltpu.VMEM((nb, 2 * n_state), F32)],
        compiler_params=_params(("arbitrary", "arbitrary")),
        name="s5_scan",
    )(u_tm, wb, a_b, wc)


def _s5_weights(lam_re, lam_im, log_step, b_re, b_im, c_re, c_im, nb):
    n_dir, n_grp, n_p = lam_re.shape
    ch = b_re.shape[-1]
    lr, li = lam_re.astype(F32), lam_im.astype(F32)
    dt = jnp.exp(log_step.astype(F32))[..., None]
    mag = jnp.exp(lr * dt)
    ab_re, ab_im = mag * jnp.cos(li * dt), mag * jnp.sin(li * dt)
    den = lr * lr + li * li
    num_re, num_im = ab_re - 1.0, ab_im
    f_re = (num_re * lr + num_im * li) / den
    f_im = (num_im * lr - num_re * li) / den
    br, bi = b_re.astype(F32), b_im.astype(F32)
    bb_re = f_re[..., None] * br - f_im[..., None] * bi
    bb_im = f_re[..., None] * bi + f_im[..., None] * br
    n_state = n_grp * n_p
    n_lg = n_state // S5_LANE_GROUP
    gpl = n_grp // n_lg
    eye = jnp.eye(gpl, dtype=F32)

    def pack_in(bb):
        t = bb.reshape(n_dir, n_lg, gpl, n_p, ch)
        m = jnp.einsum('dlgpc,gh->dlgchp', t, eye)
        return m.reshape(n_dir, n_lg, gpl * ch, gpl * n_p)

    def pack_out(c):
        t = c.astype(F32).reshape(n_dir, n_lg, gpl, ch, n_p)
        m = jnp.einsum('dlgcp,gh->dlgphc', t, eye)
        return m.reshape(n_dir, n_lg, gpl * n_p, gpl * ch)

    wb = jnp.concatenate([pack_in(bb_re), pack_in(bb_im)], axis=-1).astype(BF16)
    wc = jnp.concatenate([pack_out(c_re), -pack_out(c_im)], axis=-2).astype(BF16)
    a = jnp.concatenate([ab_re.reshape(n_dir, n_state), ab_im.reshape(n_dir, n_state)], axis=-1)
    a_b = jnp.broadcast_to(a[:, None, :], (n_dir, nb, 2 * n_state))
    return wb, a_b, wc


def _nt_dot(a, b):
    return lax.dot_general(a, b, (((1,), (1,)), ((), ())), preferred_element_type=F32)


def _win_attn_kernel(sink_ref, q_ref, kc_ref, vc_ref, *rest, tq, seq, grp, local):
    if local:
        kl_ref, vl_ref, o_ref = rest
    else:
        (o_ref,) = rest
    g = pl.program_id(2) if local else pl.program_id(1)
    kc, vc = kc_ref[0, 0], vc_ref[0, 0]
    if local:
        band = tq + 2 * WINDOW
        j = pl.program_id(1)
        start = pl.multiple_of(jnp.clip(j * tq - WINDOW, 0, seq - band), 8)
        kb = kl_ref[0, 0, pl.ds(start, band), :]
        vb = vl_ref[0, 0, pl.ds(start, band), :]
        qpos = j * tq + lax.broadcasted_iota(jnp.int32, (tq, band), 0)
        kpos = start + lax.broadcasted_iota(jnp.int32, (tq, band), 1)
        valid = jnp.abs(qpos - kpos) <= WINDOW
    outs = []
    for hh in range(grp):
        q = q_ref[:, hh * HEAD_DIM:(hh + 1) * HEAD_DIM]
        sink = sink_ref[g * grp + hh]
        s_c = _nt_dot(q, kc)
        m = jnp.maximum(jnp.max(s_c, axis=-1, keepdims=True), sink)
        if local:
            s_l = jnp.where(valid, _nt_dot(q, kb), NEG_INF)
            m = jnp.maximum(m, jnp.max(s_l, axis=-1, keepdims=True))
        p_c = jnp.exp(s_c - m)
        l = jnp.sum(p_c, axis=-1, keepdims=True) + jnp.exp(sink - m)
        acc = jnp.dot(p_c.astype(BF16), vc, preferred_element_type=F32)
        if local:
            p_l = jnp.exp(s_l - m)
            l = l + jnp.sum(p_l, axis=-1, keepdims=True)
            acc = acc + jnp.dot(p_l.astype(BF16), vb, preferred_element_type=F32)
        outs.append(acc / l)
    o_ref[...] = jnp.concatenate(outs, axis=1).astype(o_ref.dtype)


def _win_attn(q_rows, k_ctx, v_ctx, k_lat, v_lat, sink, bsz, seq, tq):
    hkv = k_ctx.shape[1]
    grp = q_rows.shape[1] // (hkv * HEAD_DIM)
    gw = grp * HEAD_DIM
    ctx = k_ctx.shape[2]
    nq = seq // tq
    assert seq >= tq + 2 * WINDOW and tq % 128 == 0
    kern = functools.partial(_win_attn_kernel, tq=tq, seq=seq, grp=grp, local=True)
    kv = lambda n: pl.BlockSpec((1, 1, n, HEAD_DIM), lambda b, j, g: (b, g, 0, 0))
    return pl.pallas_call(
        kern,
        grid=(bsz, nq, hkv),
        in_specs=[pl.BlockSpec(memory_space=pltpu.SMEM),
                  pl.BlockSpec((tq, gw), lambda b, j, g: (b * nq + j, g)),
                  kv(ctx), kv(ctx), kv(seq), kv(seq)],
        out_specs=pl.BlockSpec((tq, gw), lambda b, j, g: (b * nq + j, g)),
        out_shape=jax.ShapeDtypeStruct(q_rows.shape, BF16),
        compiler_params=_params(("parallel", "parallel", "arbitrary")),
        name="win_attn",
    )(sink, q_rows, k_ctx, v_ctx, k_lat, v_lat)


def _ctx_attn(q_rows, k_ctx, v_ctx, sink, bsz):
    hkv, ctx = k_ctx.shape[1], k_ctx.shape[2]
    grp = q_rows.shape[1] // (hkv * HEAD_DIM)
    gw = grp * HEAD_DIM
    kern = functools.partial(_win_attn_kernel, tq=ctx, seq=ctx, grp=grp, local=False)
    kv = pl.BlockSpec((1, 1, ctx, HEAD_DIM), lambda b, g: (b, g, 0, 0))
    return pl.pallas_call(
        kern,
        grid=(bsz, hkv),
        in_specs=[pl.BlockSpec(memory_space=pltpu.SMEM),
                  pl.BlockSpec((ctx, gw), lambda b, g: (b, g)), kv, kv],
        out_specs=pl.BlockSpec((ctx, gw), lambda b, g: (b, g)),
        out_shape=jax.ShapeDtypeStruct(q_rows.shape, BF16),
        compiler_params=_params(("parallel", "arbitrary")),
        name="ctx_attn",
    )(sink, q_rows, k_ctx, v_ctx)


def _gelu_tanh(y):
    return 0.5 * y * (1.0 + jnp.tanh(math.sqrt(2.0 / math.pi) * (y + 0.044715 * (y * y * y))))


def _split_bf16(v):
    hi = v.astype(BF16)
    return hi, (v - hi.astype(F32)).astype(BF16)


def _mix_out_kernel(*refs, even):
    if even:
        (x_ref, mod_ref, o_ref, wout_ref, lng_ref, lnb_ref, rwh_ref, rwl_ref,
         u_ref, yf_ref, yb_ref, dsk_ref, wglu_ref, bglu_ref, x1_ref, h2_ref, lg_ref) = refs
        y = u_ref[...] * dsk_ref[...] + yf_ref[0] + yb_ref[0]
        z = _gelu_tanh(y)
        gate = jax.nn.sigmoid(jnp.dot(z.astype(BF16), wglu_ref[...], preferred_element_type=F32) + bglu_ref[...])
        mix = jnp.concatenate([(z * gate).astype(BF16), o_ref[...]], axis=1)
    else:
        (x_ref, mod_ref, o_ref, wout_ref, lng_ref, lnb_ref, rwh_ref, rwl_ref, x1_ref, h2_ref, lg_ref) = refs
        mix = o_ref[...]
    m = mod_ref[0]
    ol = jnp.dot(mix, wout_ref[...], preferred_element_type=F32)
    x1 = _layer_norm(ALPHA * x_ref[...] + m[2:3, :] * ol, lng_ref[...], lnb_ref[...])
    x1_ref[...] = x1
    h2 = x1 * (1.0 + m[4:5, :]) + m[3:4, :]
    h2_ref[...] = h2
    hh, hl = _split_bf16(h2)
    rwh, rwl = rwh_ref[...], rwl_ref[...]
    lg_ref[...] = _nt_dot(rwh, hh) + (_nt_dot(rwh, hl) + _nt_dot(rwl, hh))


def _mix_out(rows, n_rows, x_all, mods, o_rows, w_out, ln_g, ln_b, rw_hi, rw_lo, even_args=None):
    tm, d = rows.tm, x_all.shape[1]
    even = even_args is not None
    row = lambda w: pl.BlockSpec((tm, w), lambda i: (i, 0))
    ins = [x_all, mods, o_rows, w_out, ln_g, ln_b, rw_hi, rw_lo]
    specs = [row(d), pl.BlockSpec((1, 6, d), lambda i: (rows.group(i), 0, 0)), row(o_rows.shape[1]),
             _full(w_out.shape), _full(ln_g.shape), _full(ln_b.shape), _full(rw_hi.shape), _full(rw_lo.shape)]
    if even:
        u32, y_tm, d_skip, w_glu, b_glu = even_args
        s5w = u32.shape[1]
        tmaj = lambda dr: pl.BlockSpec((1, tm, s5w), lambda i: (dr,) + tuple(rows.time_major(i)))
        ins += [u32, y_tm, y_tm, d_skip, w_glu, b_glu]
        specs += [row(s5w), tmaj(0), tmaj(1), _full(d_skip.shape), _full(w_glu.shape), _full(b_glu.shape)]
    n_e = rw_hi.shape[0]
    return pl.pallas_call(
        functools.partial(_mix_out_kernel, even=even),
        grid=(n_rows // tm,),
        in_specs=specs,
        out_specs=[row(d), row(d), pl.BlockSpec((n_e, tm), lambda i: (0, i))],
        out_shape=[jax.ShapeDtypeStruct((n_rows, d), F32), jax.ShapeDtypeStruct((n_rows, d), F32),
                   jax.ShapeDtypeStruct((n_e, n_rows), F32)],
        compiler_params=_params(("parallel",)),
        name="mix_out_even" if even else "mix_out_odd",
    )(*ins)


def _route_kernel(lg_ref, bias_ref, e0_ref, e1_ref, w0_ref, w1_ref):
    epg = EXPERTS_PER_GROUP
    sc = [jax.nn.sigmoid(lg_ref[e]) for e in range(N_EXPERTS)]
    sel = [sc[e] + bias_ref[e] for e in range(N_EXPERTS)]
    best, gidx = None, None
    for g in range(N_GROUPS):
        a = sel[g * epg:(g + 1) * epg]
        pair = None
        for i in range(epg):
            for j in range(i + 1, epg):
                s2 = a[i] + a[j]
                pair = s2 if pair is None else jnp.maximum(pair, s2)
        if best is None:
            best, gidx = pair, jnp.zeros(pair.shape, jnp.int32)
        else:
            take = pair > best
            best = jnp.where(take, pair, best)
            gidx = jnp.where(take, g, gidx)
    sg = [sel[i] for i in range(epg)]
    cg = [sc[i] for i in range(epg)]
    for g in range(1, N_GROUPS):
        hit = gidx == g
        sg = [jnp.where(hit, sel[g * epg + i], sg[i]) for i in range(epg)]
        cg = [jnp.where(hit, sc[g * epg + i], cg[i]) for i in range(epg)]

    def first_max(vals, skip=None):
        bv, bi, bs = None, None, None
        for i in range(epg):
            v = vals[i] if skip is None else jnp.where(skip == i, -jnp.inf, vals[i])
            if bv is None:
                bv, bi, bs = v, jnp.zeros(v.shape, jnp.int32), cg[0]
            else:
                take = v > bv
                bv = jnp.where(take, v, bv)
                bi = jnp.where(take, i, bi)
                bs = jnp.where(take, cg[i], bs)
        return bi, bs

    i0, s0 = first_max(sg)
    i1, s1 = first_max(sg, skip=i0)
    tot = s0 + s1
    e0_ref[...] = gidx * epg + i0
    e1_ref[...] = gidx * epg + i1
    w0_ref[...] = s0 / tot
    w1_ref[...] = s1 / tot


def _route(logits_t, bias):
    n_e, n = logits_t.shape
    r = n // 128
    rb = 8
    assert n % (128 * rb) == 0
    lg3 = logits_t.reshape(n_e, r, 128)
    blk = pl.BlockSpec((rb, 128), lambda i: (i, 0))
    outs = pl.pallas_call(
        _route_kernel,
        grid=(r // rb,),
        in_specs=[pl.BlockSpec((n_e, rb, 128), lambda i: (0, i, 0)), pl.BlockSpec(memory_space=pltpu.SMEM)],
        out_specs=[blk, blk, blk, blk],
        out_shape=[jax.ShapeDtypeStruct((r, 128), jnp.int32)] * 2 + [jax.ShapeDtypeStruct((r, 128), F32)] * 2,
        compiler_params=_params(("parallel",)),
        name="route",
    )(lg3, bias)
    return [o.reshape(n) for o in outs]


def _gather_kernel(idx_ref, src_ref, out_ref, sem, *, chunk):
    base = pl.program_id(0) * chunk

    def issue(r, _):
        pltpu.make_async_copy(src_ref.at[pl.ds(idx_ref[0, 0, r], 1)], out_ref.at[pl.ds(base + r, 1)], sem).start()
        return 0

    lax.fori_loop(0, chunk, issue, 0)
    pltpu.make_async_copy(out_ref.at[pl.ds(base, chunk)], out_ref.at[pl.ds(base, chunk)], sem).wait()


def _gather_rows(src, idx):
    n_out = idx.shape[0]
    chunk = GATHER_CHUNK
    assert n_out % chunk == 0
    return pl.pallas_call(
        functools.partial(_gather_kernel, chunk=chunk),
        grid=(n_out // chunk,),
        in_specs=[pl.BlockSpec((1, 1, chunk), lambda i: (i, 0, 0), memory_space=pltpu.SMEM),
                  pl.BlockSpec(memory_space=pl.ANY)],
        out_specs=pl.BlockSpec(memory_space=pl.ANY),
        out_shape=jax.ShapeDtypeStruct((n_out, src.shape[1]), src.dtype),
        scratch_shapes=[pltpu.SemaphoreType.DMA(())],
        compiler_params=_params(("arbitrary",)),
        name="gather_rows",
    )(idx.reshape(n_out // chunk, 1, chunk), src)


def _expert_kernel(te_ref, nu_ref, x_ref, wr_ref, wgu_ref, wd_ref, y_ref, *, ff):
    @pl.when(pl.program_id(0) < nu_ref[0])
    def _():
        x = x_ref[...].astype(BF16)
        gu = jnp.dot(x, wgu_ref[0], preferred_element_type=F32)
        g, u = gu[:, :ff], gu[:, ff:]
        hid = (g * jax.nn.sigmoid(g) * u).astype(BF16)
        y = jnp.dot(hid, wd_ref[0], preferred_element_type=F32)
        y_ref[...] = y * wr_ref[:, 0:1]

    @pl.when(pl.program_id(0) >= nu_ref[0])
    def _():
        y_ref[...] = jnp.zeros_like(y_ref)


def _expert_ffn(xs, w_rows, tile_expert, n_used, wgu, wd):
    p, d = xs.shape
    te = EXPERT_TILE
    ff = wd.shape[1]
    grid_spec = pltpu.PrefetchScalarGridSpec(
        num_scalar_prefetch=2,
        grid=(p // te,),
        in_specs=[pl.BlockSpec((te, d), lambda i, t, n: (i, 0)),
                  pl.BlockSpec((te, 128), lambda i, t, n: (i, 0)),
                  pl.BlockSpec((1, d, 2 * ff), lambda i, t, n: (t[i], 0, 0)),
                  pl.BlockSpec((1, ff, d), lambda i, t, n: (t[i], 0, 0))],
        out_specs=pl.BlockSpec((te, d), lambda i, t, n: (i, 0)),
    )
    return pl.pallas_call(
        functools.partial(_expert_kernel, ff=ff),
        grid_spec=grid_spec,
        out_shape=jax.ShapeDtypeStruct((p, d), F32),
        compiler_params=_params(("arbitrary",)),
        name="expert_ffn",
    )(tile_expert, n_used, xs, w_rows, wgu, wd)


def _dispatch_plan(e0, e1, w0, w1, n_tok):
    te = EXPERT_TILE
    n_pair = TOP_K * n_tok
    p_max = -(-(n_pair + N_EXPERTS * (te - 1)) // GATHER_CHUNK) * GATHER_CHUNK
    e_flat = jnp.concatenate([e0, e1])
    w_flat = jnp.concatenate([w0, w1])
    tok = jnp.concatenate([jnp.arange(n_tok, dtype=jnp.int32)] * TOP_K)
    order = jnp.argsort(e_flat, stable=True)
    counts = jnp.zeros((N_EXPERTS,), jnp.int32).at[e_flat].add(1)
    padded = (counts + te - 1) // te * te
    pad_end = jnp.cumsum(padded)
    pad_start = pad_end - padded
    raw_start = jnp.cumsum(counts) - counts
    e_sorted = e_flat[order]
    dest_sorted = pad_start[e_sorted] + (jnp.arange(n_pair, dtype=jnp.int32) - raw_start[e_sorted])
    dest = jnp.zeros((n_pair,), jnp.int32).at[order].set(dest_sorted)
    src_tok = jnp.zeros((p_max,), jnp.int32).at[dest].set(tok)
    w_rows = jnp.zeros((p_max,), F32).at[dest].set(w_flat)
    tiles = jnp.arange(p_max // te, dtype=jnp.int32) * te
    tile_expert = jnp.minimum(jnp.searchsorted(pad_end, tiles, side='right'), N_EXPERTS - 1).astype(jnp.int32)
    n_used = (pad_end[-1] // te).astype(jnp.int32).reshape(1)
    return src_tok, jnp.broadcast_to(w_rows[:, None], (p_max, 128)), tile_expert, n_used, dest


def _moe_out_kernel(x_ref, mod_ref, ya_ref, yb_ref, lng_ref, lnb_ref, o_ref):
    m = mod_ref[0]
    f = ya_ref[...] + yb_ref[...]
    o_ref[...] = _layer_norm(ALPHA * x_ref[...] + m[5:6, :] * f, lng_ref[...], lnb_ref[...])


def _moe_out(rows, n_rows, x1, mods, yg, ln_g, ln_b):
    tm, d = rows.tm, x1.shape[1]
    nt = n_rows // tm
    row = pl.BlockSpec((tm, d), lambda i: (i, 0))
    return pl.pallas_call(
        _moe_out_kernel,
        grid=(nt,),
        in_specs=[row, pl.BlockSpec((1, 6, d), lambda i: (rows.group(i), 0, 0)),
                  row, pl.BlockSpec((tm, d), lambda i: (i + nt, 0)), _full(ln_g.shape), _full(ln_b.shape)],
        out_specs=row,
        out_shape=jax.ShapeDtypeStruct((n_rows, d), F32),
        compiler_params=_params(("parallel",)),
        name="moe_out",
    )(x1, mods, yg, yg, ln_g, ln_b)


def _moe(rows, n_rows, x1, h2, logits_t, mods, router_bias, wgu, wd, ln_g, ln_b):
    e0, e1, w0, w1 = _route(logits_t, router_bias)
    src_tok, w_rows, tile_expert, n_used, dest = _dispatch_plan(e0, e1, w0, w1, n_rows)
    xs = _gather_rows(h2, src_tok)
    ys = _expert_ffn(xs, w_rows, tile_expert, n_used, wgu, wd)
    yg = _gather_rows(ys, dest)
    return _moe_out(rows, n_rows, x1, mods, yg, ln_g, ln_b)


def _odd_in_kernel(x_ref, mod_ref, w_ref, cos_ref, sin_ref, gq_ref, gk_ref, hm_ref, q_ref, k_ref, v_ref, *, qw, kw):
    m = mod_ref[0]
    h = (x_ref[...] * (1.0 + m[1:2, :]) + m[0:1, :]).astype(BF16)
    r = jnp.dot(h, w_ref[...], preferred_element_type=F32)
    q, q_sw = r[:, :qw], r[:, qw:2 * qw]
    o = 2 * qw
    k, k_sw = r[:, o:o + kw], r[:, o + kw:o + 2 * kw]
    v = r[:, o + 2 * kw:o + 3 * kw]
    cos, sin = cos_ref[...], sin_ref[...]
    hm = hm_ref[...]

    def norm_rope(t, t_sw, gains, width):
        hi, lo = _split_bf16(t * t)
        ms = (jnp.dot(hi, hm[:width, :width], preferred_element_type=F32)
              + jnp.dot(lo, hm[:width, :width], preferred_element_type=F32))
        rs = lax.rsqrt(ms + RMS_EPS)
        c = jnp.concatenate([cos] * (width // 128), axis=1)
        s = jnp.concatenate([sin] * (width // 128), axis=1)
        return (t * gains[0:1, :] * c + t_sw * gains[1:2, :] * s) * rs

    q_ref[...] = (norm_rope(q, q_sw, gq_ref[...], qw) * (HEAD_DIM ** -0.5)).astype(BF16)
    k_ref[...] = norm_rope(k, k_sw, gk_ref[...], kw).astype(BF16)
    v_ref[...] = v.astype(BF16)


def _odd_in(rows, x_all, mods, w_cat, cos_t, sin_t, gq, gk, head_mean, qw, kw):
    tm, d = rows.tm, x_all.shape[1]
    n = rows.n_all
    row = lambda w: pl.BlockSpec((tm, w), lambda i: (i, 0))
    tab = pl.BlockSpec((tm, 128), lambda i: (rows.rope_block(i), 0))
    return pl.pallas_call(
        functools.partial(_odd_in_kernel, qw=qw, kw=kw),
        grid=(n // tm,),
        in_specs=[row(d), pl.BlockSpec((1, 6, d), lambda i: (rows.group(i), 0, 0)), _full(w_cat.shape), tab, tab,
                  _full(gq.shape), _full(gk.shape), _full(head_mean.shape)],
        out_specs=[row(qw), row(kw), row(kw)],
        out_shape=[jax.ShapeDtypeStruct((n, qw), BF16), jax.ShapeDtypeStruct((n, kw), BF16),
                   jax.ShapeDtypeStruct((n, kw), BF16)],
        compiler_params=_params(("parallel",)),
        name="odd_in",
    )(x_all, mods, w_cat, cos_t, sin_t, gq, gk, head_mean)


def _flash_kernel(q_ref, k_ref, v_ref, o_ref, m_ref, l_ref, acc_ref, *, grp, tq):
    kv = pl.program_id(3)

    @pl.when(kv == 0)
    def _():
        m_ref[...] = jnp.full_like(m_ref, -jnp.inf)
        l_ref[...] = jnp.zeros_like(l_ref)
        acc_ref[...] = jnp.zeros_like(acc_ref)

    k, v = k_ref[0, 0], v_ref[0, 0]
    for hh in range(grp):
        q = q_ref[:, hh * HEAD_DIM:(hh + 1) * HEAD_DIM]
        s = _nt_dot(q, k)
        m_prev = m_ref[hh]
        m_new = jnp.maximum(m_prev, jnp.max(s, axis=-1, keepdims=True))
        a = jnp.exp(m_prev - m_new)
        p = jnp.exp(s - m_new)
        l_ref[hh] = a * l_ref[hh] + jnp.sum(p, axis=-1, keepdims=True)
        acc_ref[hh] = a * acc_ref[hh] + jnp.dot(p.astype(BF16), v, preferred_element_type=F32)
        m_ref[hh] = m_new

    @pl.when(kv == pl.num_programs(3) - 1)
    def _():
        o_ref[...] = jnp.concatenate([acc_ref[hh] / l_ref[hh] for hh in range(grp)], axis=1).astype(o_ref.dtype)


def _flash_attn(q_rows, k_all, v_all, bsz, seq, tq, tk):
    hkv, n_k = k_all.shape[1], k_all.shape[2]
    grp = q_rows.shape[1] // (hkv * HEAD_DIM)
    gw = grp * HEAD_DIM
    nq = seq // tq
    assert n_k % tk == 0 and seq % tq == 0
    kvs = pl.BlockSpec((1, 1, tk, HEAD_DIM), lambda b, j, g, t: (b, g, t, 0))
    return pl.pallas_call(
        functools.partial(_flash_kernel, grp=grp, tq=tq),
        grid=(bsz, nq, hkv, n_k // tk),
        in_specs=[pl.BlockSpec((tq, gw), lambda b, j, g, t: (b * nq + j, g)), kvs, kvs],
        out_specs=pl.BlockSpec((tq, gw), lambda b, j, g, t: (b * nq + j, g)),
        out_shape=jax.ShapeDtypeStruct((bsz * seq, q_rows.shape[1]), BF16),
        scratch_shapes=[pltpu.VMEM((grp, tq, 1), F32), pltpu.VMEM((grp, tq, 1), F32),
                        pltpu.VMEM((grp, tq, HEAD_DIM), F32)],
        compiler_params=_params(("parallel", "parallel", "parallel", "arbitrary")),
        name="flash_attn",
    )(q_rows, k_all, v_all)


def _rope_tables(seq, tm):
    n_freq = HEAD_DIM // 4
    inv_freq = ROPE_THETA ** (-jnp.arange(n_freq, dtype=F32) / n_freq)
    rows = seq // GRID_W
    r = jnp.repeat(jnp.arange(rows, dtype=F32), GRID_W)
    col = jnp.tile(jnp.arange(GRID_W, dtype=F32), rows)
    ang = jnp.concatenate([r[:, None] * inv_freq, col[:, None] * inv_freq], -1)
    cos, sin = jnp.cos(ang), jnp.sin(ang)
    cos_t = jnp.concatenate([jnp.tile(cos, (1, 4)), jnp.ones((tm, 128), F32)], axis=0)
    sin_t = jnp.concatenate([jnp.tile(sin, (1, 4)), jnp.zeros((tm, 128), F32)], axis=0)
    return cos_t, sin_t


def _swap_halves(w):
    d, n = w.shape
    t = w.reshape(d, n // HEAD_DIM, 2, HEAD_DIM // 2)
    return jnp.stack([-t[:, :, 1], t[:, :, 0]], axis=2).reshape(d, n)


def _to_heads(t_rows, bsz, length, hkv):
    return t_rows.reshape(bsz, length, hkv, HEAD_DIM).transpose(0, 2, 1, 3)


def kernel(x, c, ctx, c_ctx, ada_w, ada_b, ln_g, ln_b, even_w_in, even_w_out, s5_lam_re, s5_lam_im, s5_log_step,
           s5_b_re, s5_b_im, s5_c_re, s5_c_im, s5_d, s5_w_glu, s5_b_glu, win_sink, odd_w_in, odd_w_out,
           odd_q_norm, odd_k_norm, router_w, router_bias, moe_w_gate, moe_w_up, moe_w_down):
    bsz, seq, d = x.shape
    n_ctx = ctx.shape[1]
    assert ada_w.shape[0] == DEPTH == 2
    rows = _Rows(bsz, seq, n_ctx)
    n_lat, n_all = rows.n_lat, rows.n_all
    s5w = s5_d.shape[1]
    win_q = win_sink.shape[1] * HEAD_DIM
    win_kv = (even_w_in.shape[2] - s5w - win_q) // 2
    win_hkv = win_kv // HEAD_DIM
    odd_q = odd_w_out.shape[1]
    odd_kv = (odd_w_in.shape[2] - odd_q) // 2
    odd_hkv = odd_kv // HEAD_DIM

    cond = jnp.zeros((16, d), F32).at[:bsz].set(c).at[bsz].set(c_ctx)
    mods = _ada(cond, ada_w, ada_b).reshape(DEPTH, 16, 6, d)
    cos_t, sin_t = _rope_tables(seq, rows.tm)
    rw_hi, rw_lo = _split_bf16(router_w.T)
    x_all = jnp.concatenate([x.reshape(n_lat, d), ctx.reshape(bsz * n_ctx, d)], axis=0)
    lnv = lambda i, j: (ln_g[i, j].reshape(1, d), ln_b[i, j].reshape(1, d))
    moe_w = lambda i: (jnp.concatenate([moe_w_gate[i], moe_w_up[i]], axis=-1).astype(BF16), moe_w_down[i].astype(BF16))

    w = even_w_in[0]
    wu, wq, wk, wv = (w[:, :s5w], w[:, s5w:s5w + win_q], w[:, s5w + win_q:s5w + win_q + win_kv],
                      w[:, s5w + win_q + win_kv:])
    w_cat = jnp.concatenate([wu, wq, _swap_halves(wq), wk, _swap_halves(wk), wv], axis=1).astype(BF16)
    u32, u_tm, q0, k0, v0 = _even_in(rows, x_all, mods[0], w_cat, cos_t, sin_t, s5w, win_q, win_kv)
    wb, a_b, wc = _s5_weights(s5_lam_re[0], s5_lam_im[0], s5_log_step[0], s5_b_re[0], s5_b_im[0],
                              s5_c_re[0], s5_c_im[0], bsz)
    t_all = seq + n_ctx
    y_tm = _s5(u_tm.reshape(t_all * bsz, s5w), wb, a_b, wc, bsz, n_ctx, seq).reshape(2, t_all, bsz * s5w)
    k_lat, v_lat = _to_heads(k0[:n_lat], bsz, seq, win_hkv), _to_heads(v0[:n_lat], bsz, seq, win_hkv)
    k_ctx, v_ctx = _to_heads(k0[n_lat:], bsz, n_ctx, win_hkv), _to_heads(v0[n_lat:], bsz, n_ctx, win_hkv)
    sink = win_sink[0].astype(F32)
    o_lat = _win_attn(q0[:n_lat], k_ctx, v_ctx, k_lat, v_lat, sink, bsz, seq, tq=256)
    o_ctx = _ctx_attn(q0[n_lat:], k_ctx, v_ctx, sink, bsz)
    o_rows = jnp.concatenate([o_lat, o_ctx], axis=0)
    g0, b0 = lnv(0, 0)
    even_args = (u32, y_tm, s5_d[0].reshape(1, s5w), s5_w_glu[0].astype(BF16), s5_b_glu[0].reshape(1, s5w))
    x1, h2, lg = _mix_out(rows, n_all, x_all, mods[0], o_rows, even_w_out[0].astype(BF16), g0, b0, rw_hi, rw_lo,
                          even_args)
    g1, b1 = lnv(0, 1)
    x2 = _moe(rows, n_all, x1, h2, lg, mods[0], router_bias.astype(F32), *moe_w(0), g1, b1)

    w = odd_w_in[0]
    wq, wk, wv = w[:, :odd_q], w[:, odd_q:odd_q + odd_kv], w[:, odd_q + odd_kv:]
    w_cat = jnp.concatenate([wq, _swap_halves(wq), wk, _swap_halves(wk), wv], axis=1).astype(BF16)

    def gains(gv, width):
        gs = jnp.concatenate([gv[HEAD_DIM // 2:], gv[:HEAD_DIM // 2]])
        return jnp.stack([jnp.tile(gv, width // HEAD_DIM), jnp.tile(gs, width // HEAD_DIM)]).astype(F32)

    head_mean = jnp.kron(jnp.eye(odd_q // HEAD_DIM, dtype=F32),
                         jnp.full((HEAD_DIM, HEAD_DIM), 1.0 / HEAD_DIM, F32)).astype(BF16)
    q1, k1, v1 = _odd_in(rows, x2, mods[1], w_cat, cos_t, sin_t, gains(odd_q_norm[0], odd_q),
                         gains(odd_k_norm[0], odd_kv), head_mean, odd_q, odd_kv)
    cat_kv = lambda t: jnp.concatenate([_to_heads(t[n_lat:], bsz, n_ctx, odd_hkv),
                                        _to_heads(t[:n_lat], bsz, seq, odd_hkv)], axis=2)
    o1 = _flash_attn(q1, cat_kv(k1), cat_kv(v1), bsz, seq, tq=512, tk=256)
    g0, b0 = lnv(1, 0)
    x1, h2, lg = _mix_out(rows, n_lat, x2, mods[1], o1, odd_w_out[0].astype(BF16), g0, b0, rw_hi, rw_lo)
    g1, b1 = lnv(1, 1)
    out = _moe(rows, n_lat, x1, h2, lg, mods[1], router_bias.astype(F32), *moe_w(1), g1, b1)
    return out.reshape(bsz, seq, d)
```

```python
import functools
import math

import jax
import jax.numpy as jnp
from jax import lax
from jax.experimental import pallas as pl
from jax.experimental.pallas import tpu as pltpu

F32 = jnp.float32
BF16 = jnp.bfloat16

HEAD_DIM = 64
GRID_W = 64
ROPE_THETA = 10000.0
S5_GROUP = 16
S5_STATE = 64
WINDOW = 128
N_EXPERTS = 32
N_GROUPS = 8
EXPERTS_PER_GROUP = N_EXPERTS // N_GROUPS
TOP_K = 2
DEPTH = 2
ALPHA = (2 * DEPTH) ** 0.25
LN_EPS = 1e-5
RMS_EPS = 1e-6
NEG_INF = -1e30

ROW_TILE = 256
S5_CHUNK = 64
S5_LANE_GROUP = 512
EXPERT_TILE = 256
RANK_BLOCK = 512
DENSE_TQ = 256
VMEM_LIMIT = 48 * 1024 * 1024


def _params(sem):
    return pltpu.CompilerParams(dimension_semantics=sem, vmem_limit_bytes=VMEM_LIMIT)


def _full(shape):
    n = len(shape)
    return pl.BlockSpec(shape, lambda *_: (0,) * n)


def _ada_kernel(c_ref, w_ref, b_ref, o_ref):
    c = c_ref[...]
    s = c * jax.nn.sigmoid(c)
    o_ref[...] = jnp.dot(s, w_ref[0], preferred_element_type=F32, precision=lax.Precision.HIGHEST) + b_ref[0]


def _ada(cond, ada_w, ada_b):
    g, d = cond.shape
    depth, _, n = ada_w.shape
    bn = 1024
    return pl.pallas_call(
        _ada_kernel,
        grid=(depth, n // bn),
        in_specs=[pl.BlockSpec((g, d), lambda i, j: (0, 0)),
                  pl.BlockSpec((1, d, bn), lambda i, j: (i, 0, j)),
                  pl.BlockSpec((1, 1, bn), lambda i, j: (i, 0, j))],
        out_specs=pl.BlockSpec((None, g, bn), lambda i, j: (i, 0, j)),
        out_shape=jax.ShapeDtypeStruct((depth, g, n), F32),
        compiler_params=_params(("arbitrary", "arbitrary")),
        name="ada",
    )(cond, ada_w, ada_b.reshape(depth, 1, n))


class _Rows:
    def __init__(self, bsz, seq, ctx, tm=ROW_TILE):
        assert seq % tm == 0 and ctx % tm == 0
        self.bsz, self.seq, self.ctx, self.tm = bsz, seq, ctx, tm
        self.tpb = seq // tm
        self.cpb = ctx // tm
        self.nl = bsz * self.tpb
        self.nc = bsz * self.cpb
        self.n_lat = bsz * seq
        self.n_all = bsz * (seq + ctx)

    def group(self, i):
        return jnp.where(i < self.nl, i // self.tpb, self.bsz)

    def rope_block(self, i):
        return jnp.where(i < self.nl, i % self.tpb, self.tpb)

    def time_major(self, i):
        lat = i < self.nl
        j = i - self.nl
        return (jnp.where(lat, self.cpb + i % self.tpb, j % self.cpb),
                jnp.where(lat, i // self.tpb, j // self.cpb))


def _layer_norm(r, g, b):
    mu = jnp.mean(r, axis=-1, keepdims=True)
    rc = r - mu
    var = jnp.mean(rc * rc, axis=-1, keepdims=True)
    return rc * lax.rsqrt(var + LN_EPS) * g + b


def _even_in_kernel(x_ref, mod_ref, w_ref, cos_ref, sin_ref, u32_ref, utm_ref, q_ref, k_ref, v_ref, *, s5w, qw, kw):
    m = mod_ref[0]
    h = (x_ref[...] * (1.0 + m[1:2, :]) + m[0:1, :]).astype(BF16)
    r = jnp.dot(h, w_ref[...], preferred_element_type=F32)
    u = r[:, :s5w]
    u32_ref[...] = u
    utm_ref[...] = u.astype(BF16)
    o = s5w
    q, q_sw = r[:, o:o + qw], r[:, o + qw:o + 2 * qw]
    o += 2 * qw
    k, k_sw = r[:, o:o + kw], r[:, o + kw:o + 2 * kw]
    o += 2 * kw
    v = r[:, o:o + kw]
    cos, sin = cos_ref[...], sin_ref[...]
    cq = jnp.concatenate([cos] * (qw // 128), axis=1)
    sq = jnp.concatenate([sin] * (qw // 128), axis=1)
    q_ref[...] = ((q * cq + q_sw * sq) * (HEAD_DIM ** -0.5)).astype(BF16)
    k_ref[...] = (k * cos + k_sw * sin).astype(BF16)
    v_ref[...] = v.astype(BF16)


def _even_in(rows, x_all, mods, w_cat, cos_t, sin_t, s5w, qw, kw):
    tm, d = rows.tm, x_all.shape[1]
    n = rows.n_all
    t_all = rows.seq + rows.ctx
    kern = functools.partial(_even_in_kernel, s5w=s5w, qw=qw, kw=kw)
    row = lambda w: pl.BlockSpec((tm, w), lambda i: (i, 0))
    return pl.pallas_call(
        kern,
        grid=(n // tm,),
        in_specs=[row(d),
                  pl.BlockSpec((1, 6, d), lambda i: (rows.group(i), 0, 0)),
                  _full(w_cat.shape),
                  pl.BlockSpec((tm, 128), lambda i: (rows.rope_block(i), 0)),
                  pl.BlockSpec((tm, 128), lambda i: (rows.rope_block(i), 0))],
        out_specs=[row(s5w),
                   pl.BlockSpec((tm, s5w), lambda i: rows.time_major(i)),
                   row(qw), row(kw), row(kw)],
        out_shape=[jax.ShapeDtypeStruct((n, s5w), F32),
                   jax.ShapeDtypeStruct((t_all, rows.bsz * s5w), BF16),
                   jax.ShapeDtypeStruct((n, qw), BF16),
                   jax.ShapeDtypeStruct((n, kw), BF16),
                   jax.ShapeDtypeStruct((n, kw), BF16)],
        compiler_params=_params(("parallel",)),
        name="even_in",
    )(x_all, mods, w_cat, cos_t, sin_t)


def _s5_kernel(u_ref, wb_ref, a_ref, wc_ref, y_ref, bu_ref, h_ref, *, tc, nb, width, n_state):
    lg = S5_LANE_GROUP
    n_lg = n_state // lg
    kch = width // n_lg
    d = pl.program_id(0)

    @pl.when(pl.program_id(1) == 0)
    def _():
        h_ref[...] = jnp.zeros_like(h_ref)

    u = u_ref[...]
    for g in range(n_lg):
        r = jnp.dot(u[:, kch * g:kch * (g + 1)], wb_ref[0, g], preferred_element_type=F32)
        bu_ref[:, lg * g:lg * (g + 1)] = r[:, :lg]
        bu_ref[:, n_state + lg * g:n_state + lg * (g + 1)] = r[:, lg:]

    rev = d == 1
    for g in range(n_lg):
        re = slice(lg * g, lg * (g + 1))
        im = slice(n_state + lg * g, n_state + lg * (g + 1))
        a_re = a_ref[0, :, re]
        a_im = a_ref[0, :, im]

        def body(i, carry, re=re, im=im, a_re=a_re, a_im=a_im):
            hr, hi = carry
            t = jnp.where(rev, tc - 1 - i, i)
            row = pl.multiple_of(t * nb, nb)
            br = bu_ref[pl.ds(row, nb), re]
            bi = bu_ref[pl.ds(row, nb), im]
            nr = a_re * hr - a_im * hi + br
            ni = a_re * hi + a_im * hr + bi
            bu_ref[pl.ds(row, nb), re] = nr
            bu_ref[pl.ds(row, nb), im] = ni
            return nr, ni

        hr, hi = lax.fori_loop(0, tc, body, (h_ref[:, re], h_ref[:, im]), unroll=4)
        h_ref[:, re] = hr
        h_ref[:, im] = hi

    for g in range(n_lg):
        hc = jnp.concatenate([bu_ref[:, lg * g:lg * (g + 1)],
                              bu_ref[:, n_state + lg * g:n_state + lg * (g + 1)]], axis=1).astype(BF16)
        y_ref[0, :, kch * g:kch * (g + 1)] = jnp.dot(hc, wc_ref[0, g], preferred_element_type=F32)


def _s5(u_tm, wb, a_b, wc, nb, ctx, seq):
    tc = S5_CHUNK
    t_all = ctx + seq
    width = u_tm.shape[1]
    n_state = a_b.shape[2] // 2
    assert ctx % tc == 0 and seq % tc == 0 and nb == 8
    n_ctx, n_chunks = ctx // tc, t_all // tc

    def chunk(d, s):
        back = jnp.where(s < n_ctx, n_ctx - 1 - s, n_chunks - 1 - (s - n_ctx))
        return jnp.where(d == 0, s, back)

    kern = functools.partial(_s5_kernel, tc=tc, nb=nb, width=width, n_state=n_state)
    return pl.pallas_call(
        kern,
        grid=(2, n_chunks),
        in_specs=[pl.BlockSpec((tc * nb, width), lambda d, s: (chunk(d, s), 0)),
                  pl.BlockSpec((1,) + wb.shape[1:], lambda d, s: (d, 0, 0, 0)),
                  pl.BlockSpec((1,) + a_b.shape[1:], lambda d, s: (d, 0, 0)),
                  pl.BlockSpec((1,) + wc.shape[1:], lambda d, s: (d, 0, 0, 0))],
        out_specs=pl.BlockSpec((1, tc * nb, width), lambda d, s: (d, chunk(d, s), 0)),
        out_shape=jax.ShapeDtypeStruct((2, t_all * nb, width), F32),
        scratch_shapes=[pltpu.VMEM((tc * nb, 2 * n_state), F32), pltpu.VMEM((nb, 2 * n_state), F32)],
        compiler_params=_params(("arbitrary", "arbitrary")),
        name="s5_scan",
    )(u_tm, wb, a_b, wc)


def _s5_weights(lam_re, lam_im, log_step, b_re, b_im, c_re, c_im, nb):
    n_dir, n_grp, n_p = lam_re.shape
    ch = b_re.shape[-1]
    lr, li = lam_re.astype(F32), lam_im.astype(F32)
    dt = jnp.exp(log_step.astype(F32))[..., None]
    mag = jnp.exp(lr * dt)
    ab_re, ab_im = mag * jnp.cos(li * dt), mag * jnp.sin(li * dt)
    den = lr * lr + li * li
    num_re, num_im = ab_re - 1.0, ab_im
    f_re = (num_re * lr + num_im * li) / den
    f_im = (num_im * lr - num_re * li) / den
    br, bi = b_re.astype(F32), b_im.astype(F32)
    bb_re = f_re[..., None] * br - f_im[..., None] * bi
    bb_im = f_re[..., None] * bi + f_im[..., None] * br
    n_state = n_grp * n_p
    n_lg = n_state // S5_LANE_GROUP
    gpl = n_grp // n_lg
    eye = jnp.eye(gpl, dtype=F32)

    def pack_in(bb):
        t = bb.reshape(n_dir, n_lg, gpl, n_p, ch)
        m = jnp.einsum('dlgpc,gh->dlgchp', t, eye)
        return m.reshape(n_dir, n_lg, gpl * ch, gpl * n_p)

    def pack_out(c):
        t = c.astype(F32).reshape(n_dir, n_lg, gpl, ch, n_p)
        m = jnp.einsum('dlgcp,gh->dlgphc', t, eye)
        return m.reshape(n_dir, n_lg, gpl * n_p, gpl * ch)

    wb = jnp.concatenate([pack_in(bb_re), pack_in(bb_im)], axis=-1).astype(BF16)
    wc = jnp.concatenate([pack_out(c_re), -pack_out(c_im)], axis=-2).astype(BF16)
    a = jnp.concatenate([ab_re.reshape(n_dir, n_state), ab_im.reshape(n_dir, n_state)], axis=-1)
    a_b = jnp.broadcast_to(a[:, None, :], (n_dir, nb, 2 * n_state))
    return wb, a_b, wc


def _nt_dot(a, b):
    return lax.dot_general(a, b, (((1,), (1,)), ((), ())), preferred_element_type=F32)


def _win_attn_kernel(sink_ref, q_ref, kc_ref, vc_ref, *rest, tq, seq, grp, local):
    if local:
        kl_ref, vl_ref, o_ref = rest
    else:
        (o_ref,) = rest
    g = pl.program_id(2) if local else pl.program_id(1)
    kc, vc = kc_ref[0, 0], vc_ref[0, 0]
    if local:
        band = tq + 2 * WINDOW
        j = pl.program_id(1)
        start = pl.multiple_of(jnp.clip(j * tq - WINDOW, 0, seq - band), 8)
        kb = kl_ref[0, 0, pl.ds(start, band), :]
        vb = vl_ref[0, 0, pl.ds(start, band), :]
        qpos = j * tq + lax.broadcasted_iota(jnp.int32, (tq, band), 0)
        kpos = start + lax.broadcasted_iota(jnp.int32, (tq, band), 1)
        valid = jnp.abs(qpos - kpos) <= WINDOW
    outs = []
    for hh in range(grp):
        q = q_ref[:, hh * HEAD_DIM:(hh + 1) * HEAD_DIM]
        sink = sink_ref[g * grp + hh]
        s_c = _nt_dot(q, kc)
        m = jnp.maximum(jnp.max(s_c, axis=-1, keepdims=True), sink)
        if local:
            s_l = jnp.where(valid, _nt_dot(q, kb), NEG_INF)
            m = jnp.maximum(m, jnp.max(s_l, axis=-1, keepdims=True))
        p_c = jnp.exp(s_c - m)
        l = jnp.sum(p_c, axis=-1, keepdims=True) + jnp.exp(sink - m)
        acc = jnp.dot(p_c.astype(BF16), vc, preferred_element_type=F32)
        if local:
            p_l = jnp.exp(s_l - m)
            l = l + jnp.sum(p_l, axis=-1, keepdims=True)
            acc = acc + jnp.dot(p_l.astype(BF16), vb, preferred_element_type=F32)
        outs.append(acc / l)
    o_ref[...] = jnp.concatenate(outs, axis=1).astype(o_ref.dtype)


def _win_attn(q_rows, k_ctx, v_ctx, k_lat, v_lat, sink, bsz, seq, tq):
    hkv = k_ctx.shape[1]
    grp = q_rows.shape[1] // (hkv * HEAD_DIM)
    gw = grp * HEAD_DIM
    ctx = k_ctx.shape[2]
    nq = seq // tq
    assert seq >= tq + 2 * WINDOW and tq % 128 == 0
    kern = functools.partial(_win_attn_kernel, tq=tq, seq=seq, grp=grp, local=True)
    kv = lambda n: pl.BlockSpec((1, 1, n, HEAD_DIM), lambda b, j, g: (b, g, 0, 0))
    return pl.pallas_call(
        kern,
        grid=(bsz, nq, hkv),
        in_specs=[pl.BlockSpec(memory_space=pltpu.SMEM),
                  pl.BlockSpec((tq, gw), lambda b, j, g: (b * nq + j, g)),
                  kv(ctx), kv(ctx), kv(seq), kv(seq)],
        out_specs=pl.BlockSpec((tq, gw), lambda b, j, g: (b * nq + j, g)),
        out_shape=jax.ShapeDtypeStruct(q_rows.shape, BF16),
        compiler_params=_params(("parallel", "parallel", "arbitrary")),
        name="win_attn",
    )(sink, q_rows, k_ctx, v_ctx, k_lat, v_lat)


def _ctx_attn(q_rows, k_ctx, v_ctx, sink, bsz):
    hkv, ctx = k_ctx.shape[1], k_ctx.shape[2]
    grp = q_rows.shape[1] // (hkv * HEAD_DIM)
    gw = grp * HEAD_DIM
    kern = functools.partial(_win_attn_kernel, tq=ctx, seq=ctx, grp=grp, local=False)
    kv = pl.BlockSpec((1, 1, ctx, HEAD_DIM), lambda b, g: (b, g, 0, 0))
    return pl.pallas_call(
        kern,
        grid=(bsz, hkv),
        in_specs=[pl.BlockSpec(memory_space=pltpu.SMEM),
                  pl.BlockSpec((ctx, gw), lambda b, g: (b, g)), kv, kv],
        out_specs=pl.BlockSpec((ctx, gw), lambda b, g: (b, g)),
        out_shape=jax.ShapeDtypeStruct(q_rows.shape, BF16),
        compiler_params=_params(("parallel", "arbitrary")),
        name="ctx_attn",
    )(sink, q_rows, k_ctx, v_ctx)


def _gelu_tanh(y):
    return 0.5 * y * (1.0 + jnp.tanh(math.sqrt(2.0 / math.pi) * (y + 0.044715 * (y * y * y))))


def _split_bf16(v):
    hi = v.astype(BF16)
    return hi, (v - hi.astype(F32)).astype(BF16)


def _mix_out_kernel(*refs, even):
    if even:
        (x_ref, mod_ref, o_ref, wout_ref, lng_ref, lnb_ref, rwh_ref, rwl_ref,
         u_ref, yf_ref, yb_ref, dsk_ref, wglu_ref, bglu_ref, x1_ref, h2_ref, lg_ref) = refs
        y = u_ref[...] * dsk_ref[...] + yf_ref[0] + yb_ref[0]
        z = _gelu_tanh(y)
        gate = jax.nn.sigmoid(jnp.dot(z.astype(BF16), wglu_ref[...], preferred_element_type=F32) + bglu_ref[...])
        mix = jnp.concatenate([(z * gate).astype(BF16), o_ref[...]], axis=1)
    else:
        (x_ref, mod_ref, o_ref, wout_ref, lng_ref, lnb_ref, rwh_ref, rwl_ref, x1_ref, h2_ref, lg_ref) = refs
        mix = o_ref[...]
    m = mod_ref[0]
    ol = jnp.dot(mix, wout_ref[...], preferred_element_type=F32)
    x1 = _layer_norm(ALPHA * x_ref[...] + m[2:3, :] * ol, lng_ref[...], lnb_ref[...])
    x1_ref[...] = x1
    h2 = x1 * (1.0 + m[4:5, :]) + m[3:4, :]
    h2_ref[...] = h2
    hh, hl = _split_bf16(h2)
    rwh, rwl = rwh_ref[...], rwl_ref[...]
    lg_ref[...] = _nt_dot(rwh, hh) + (_nt_dot(rwh, hl) + _nt_dot(rwl, hh))


def _mix_out(rows, n_rows, x_all, mods, o_rows, w_out, ln_g, ln_b, rw_hi, rw_lo, even_args=None):
    tm, d = rows.tm, x_all.shape[1]
    even = even_args is not None
    row = lambda w: pl.BlockSpec((tm, w), lambda i: (i, 0))
    ins = [x_all, mods, o_rows, w_out, ln_g, ln_b, rw_hi, rw_lo]
    specs = [row(d), pl.BlockSpec((1, 6, d), lambda i: (rows.group(i), 0, 0)), row(o_rows.shape[1]),
             _full(w_out.shape), _full(ln_g.shape), _full(ln_b.shape), _full(rw_hi.shape), _full(rw_lo.shape)]
    if even:
        u32, y_tm, d_skip, w_glu, b_glu = even_args
        s5w = u32.shape[1]
        tmaj = lambda dr: pl.BlockSpec((1, tm, s5w), lambda i: (dr,) + tuple(rows.time_major(i)))
        ins += [u32, y_tm, y_tm, d_skip, w_glu, b_glu]
        specs += [row(s5w), tmaj(0), tmaj(1), _full(d_skip.shape), _full(w_glu.shape), _full(b_glu.shape)]
    n_e = rw_hi.shape[0]
    return pl.pallas_call(
        functools.partial(_mix_out_kernel, even=even),
        grid=(n_rows // tm,),
        in_specs=specs,
        out_specs=[row(d), row(d), pl.BlockSpec((n_e, tm), lambda i: (0, i))],
        out_shape=[jax.ShapeDtypeStruct((n_rows, d), F32), jax.ShapeDtypeStruct((n_rows, d), F32),
                   jax.ShapeDtypeStruct((n_e, n_rows), F32)],
        compiler_params=_params(("parallel",)),
        name="mix_out_even" if even else "mix_out_odd",
    )(*ins)


def _route_kernel(lg_ref, bias_ref, e0_ref, e1_ref, w0_ref, w1_ref):
    epg = EXPERTS_PER_GROUP
    sc = [jax.nn.sigmoid(lg_ref[e]) for e in range(N_EXPERTS)]
    sel = [sc[e] + bias_ref[e] for e in range(N_EXPERTS)]
    best, gidx = None, None
    for g in range(N_GROUPS):
        a = sel[g * epg:(g + 1) * epg]
        pair = None
        for i in range(epg):
            for j in range(i + 1, epg):
                s2 = a[i] + a[j]
                pair = s2 if pair is None else jnp.maximum(pair, s2)
        if best is None:
            best, gidx = pair, jnp.zeros(pair.shape, jnp.int32)
        else:
            take = pair > best
            best = jnp.where(take, pair, best)
            gidx = jnp.where(take, g, gidx)
    sg = [sel[i] for i in range(epg)]
    cg = [sc[i] for i in range(epg)]
    for g in range(1, N_GROUPS):
        hit = gidx == g
        sg = [jnp.where(hit, sel[g * epg + i], sg[i]) for i in range(epg)]
        cg = [jnp.where(hit, sc[g * epg + i], cg[i]) for i in range(epg)]

    def first_max(vals, skip=None):
        bv, bi, bs = None, None, None
        for i in range(epg):
            v = vals[i] if skip is None else jnp.where(skip == i, -jnp.inf, vals[i])
            if bv is None:
                bv, bi, bs = v, jnp.zeros(v.shape, jnp.int32), cg[0]
            else:
                take = v > bv
                bv = jnp.where(take, v, bv)
                bi = jnp.where(take, i, bi)
                bs = jnp.where(take, cg[i], bs)
        return bi, bs

    i0, s0 = first_max(sg)
    i1, s1 = first_max(sg, skip=i0)
    tot = s0 + s1
    e0_ref[...] = gidx * epg + i0
    e1_ref[...] = gidx * epg + i1
    w0_ref[...] = s0 / tot
    w1_ref[...] = s1 / tot


def _route(logits_t, bias):
    n_e, n = logits_t.shape
    r = n // 128
    rb = 8
    assert n % (128 * rb) == 0
    lg3 = logits_t.reshape(n_e, r, 128)
    blk = pl.BlockSpec((rb, 128), lambda i: (i, 0))
    outs = pl.pallas_call(
        _route_kernel,
        grid=(r // rb,),
        in_specs=[pl.BlockSpec((n_e, rb, 128), lambda i: (0, i, 0)), pl.BlockSpec(memory_space=pltpu.SMEM)],
        out_specs=[blk, blk, blk, blk],
        out_shape=[jax.ShapeDtypeStruct((r, 128), jnp.int32)] * 2 + [jax.ShapeDtypeStruct((r, 128), F32)] * 2,
        compiler_params=_params(("parallel",)),
        name="route",
    )(lg3, bias)
    return [o.reshape(n) for o in outs]


def _rank_kernel(e_ref, tri_ref, rank_ref, cnt_ref, carry_ref, *, bw):
    @pl.when(pl.program_id(0) == 0)
    def _():
        carry_ref[...] = jnp.zeros_like(carry_ref)

    hit = lax.broadcasted_iota(jnp.int32, (N_EXPERTS, bw), 0) == e_ref[...]
    onehot = jnp.where(hit, 1.0, 0.0).astype(BF16)
    run = jnp.dot(onehot, tri_ref[...], preferred_element_type=F32) + carry_ref[...]
    rank_ref[...] = (jnp.sum(jnp.where(hit, run, 0.0), axis=0, keepdims=True) - 1.0).astype(jnp.int32)
    carry_ref[...] = run[:, bw - 1:bw]
    cnt_ref[...] = run[:, bw - 1:bw]


def _pair_ranks(e_pairs):
    n_pair = e_pairs.shape[0]
    bw = RANK_BLOCK
    assert n_pair % bw == 0
    tri = (jnp.arange(bw)[:, None] <= jnp.arange(bw)[None, :]).astype(BF16)
    rank, cnt = pl.pallas_call(
        functools.partial(_rank_kernel, bw=bw),
        grid=(n_pair // bw,),
        in_specs=[pl.BlockSpec((1, bw), lambda i: (0, i)), _full((bw, bw))],
        out_specs=[pl.BlockSpec((1, bw), lambda i: (0, i)), _full((N_EXPERTS, 1))],
        out_shape=[jax.ShapeDtypeStruct((1, n_pair), jnp.int32), jax.ShapeDtypeStruct((N_EXPERTS, 1), F32)],
        scratch_shapes=[pltpu.VMEM((N_EXPERTS, 1), F32)],
        compiler_params=_params(("arbitrary",)),
        name="pair_ranks",
    )(e_pairs.reshape(1, n_pair), tri)
    return rank.reshape(n_pair), cnt.reshape(N_EXPERTS).astype(jnp.int32)


def _dispatch_plan(e0, e1, n_tok):
    te = EXPERT_TILE
    p_max = -(-(TOP_K * n_tok + N_EXPERTS * (te - 1)) // te) * te
    e_pairs = jnp.concatenate([e0, e1])
    rank, counts = _pair_ranks(e_pairs)
    padded = (counts + te - 1) // te * te
    pad_end = jnp.cumsum(padded)
    pad_start = pad_end - padded
    is_e = e_pairs[:, None] == jnp.arange(N_EXPERTS, dtype=jnp.int32)[None, :]
    dest = rank + jnp.sum(jnp.where(is_e, pad_start[None, :], 0), axis=1)
    tiles = jnp.arange(p_max // te, dtype=jnp.int32) * te
    tile_expert = jnp.minimum(jnp.sum(tiles[:, None] >= pad_end[None, :], axis=1), N_EXPERTS - 1).astype(jnp.int32)
    n_used = (pad_end[-1] // te).astype(jnp.int32).reshape(1)
    return dest[:n_tok], dest[n_tok:], pad_start + counts, pad_end, tile_expert, n_used, p_max


def _dispatch_kernel(lo_ref, hi_ref, nu_ref, d0_ref, d1_ref, h_ref, xs_ref, zero_ref, sem, zsem, *, tm):
    def issue(r, _):
        src = h_ref.at[pl.ds(r, 1)]
        pltpu.make_async_copy(src, xs_ref.at[pl.ds(d0_ref[0, 0, r], 1)], sem).start()
        pltpu.make_async_copy(src, xs_ref.at[pl.ds(d1_ref[0, 0, r], 1)], sem).start()
        return 0

    lax.fori_loop(0, tm, issue, 0)

    @pl.when(pl.program_id(0) == pl.num_programs(0) - 1)
    def _():
        zero_ref[...] = jnp.zeros_like(zero_ref)

        def pad_copy(r):
            return pltpu.make_async_copy(zero_ref.at[pl.ds(0, 1)], xs_ref.at[pl.ds(r, 1)], zsem)

        def per_expert(e, _):
            lax.fori_loop(lo_ref[e], hi_ref[e], lambda r, c: (pad_copy(r).start(), c)[1], 0)
            lax.fori_loop(lo_ref[e], hi_ref[e], lambda r, c: (pad_copy(r).wait(), c)[1], 0)
            return 0

        lax.fori_loop(0, N_EXPERTS, per_expert, 0)

        te = zero_ref.shape[0]

        def tail_copy(t):
            return pltpu.make_async_copy(zero_ref, xs_ref.at[pl.ds(pl.multiple_of(t * te, te), te)], zsem)

        n_tiles = xs_ref.shape[0] // te
        lax.fori_loop(nu_ref[0], n_tiles, lambda t, c: (tail_copy(t).start(), c)[1], 0)
        lax.fori_loop(nu_ref[0], n_tiles, lambda t, c: (tail_copy(t).wait(), c)[1], 0)

    pltpu.make_async_copy(xs_ref.at[pl.ds(0, 2 * tm)], xs_ref.at[pl.ds(0, 2 * tm)], sem).wait()


def _dispatch(h2, n_rows, dest0, dest1, pad_lo, pad_hi, n_used, p_max, tm):
    d = h2.shape[1]
    nt = n_rows // tm
    idx = pl.BlockSpec((1, 1, tm), lambda i, *_: (i, 0, 0), memory_space=pltpu.SMEM)
    grid_spec = pltpu.PrefetchScalarGridSpec(
        num_scalar_prefetch=3,
        grid=(nt,),
        in_specs=[idx, idx, pl.BlockSpec((tm, d), lambda i, *_: (i, 0))],
        out_specs=pl.BlockSpec(memory_space=pl.ANY),
        scratch_shapes=[pltpu.VMEM((EXPERT_TILE, d), h2.dtype), pltpu.SemaphoreType.DMA(()),
                        pltpu.SemaphoreType.DMA(())],
    )
    return pl.pallas_call(
        functools.partial(_dispatch_kernel, tm=tm),
        grid_spec=grid_spec,
        out_shape=jax.ShapeDtypeStruct((p_max, d), h2.dtype),
        compiler_params=_params(("arbitrary",)),
        name="dispatch",
    )(pad_lo, pad_hi, n_used, dest0.reshape(nt, 1, tm), dest1.reshape(nt, 1, tm), h2)


def _expert_kernel(te_ref, nu_ref, x_ref, wgu_ref, wd_ref, y_ref, *, ff):
    @pl.when(pl.program_id(0) < nu_ref[0])
    def _():
        x = x_ref[...].astype(BF16)
        gu = jnp.dot(x, wgu_ref[0], preferred_element_type=F32)
        g, u = gu[:, :ff], gu[:, ff:]
        hid = (g * jax.nn.sigmoid(g) * u).astype(BF16)
        y_ref[...] = jnp.dot(hid, wd_ref[0], preferred_element_type=F32)

    @pl.when(pl.program_id(0) >= nu_ref[0])
    def _():
        y_ref[...] = jnp.zeros_like(y_ref)


def _expert_ffn(xs, tile_expert, n_used, wgu, wd):
    p, d = xs.shape
    te = EXPERT_TILE
    ff = wd.shape[1]
    grid_spec = pltpu.PrefetchScalarGridSpec(
        num_scalar_prefetch=2,
        grid=(p // te,),
        in_specs=[pl.BlockSpec((te, d), lambda i, t, n: (jnp.minimum(i, n[0] - 1), 0)),
                  pl.BlockSpec((1, d, 2 * ff), lambda i, t, n: (t[i], 0, 0)),
                  pl.BlockSpec((1, ff, d), lambda i, t, n: (t[i], 0, 0))],
        out_specs=pl.BlockSpec((te, d), lambda i, t, n: (i, 0)),
    )
    return pl.pallas_call(
        functools.partial(_expert_kernel, ff=ff),
        grid_spec=grid_spec,
        out_shape=jax.ShapeDtypeStruct((p, d), F32),
        compiler_params=_params(("arbitrary",)),
        name="expert_ffn",
    )(tile_expert, n_used, xs, wgu, wd)


def _moe_out_kernel(d0_ref, d1_ref, x_ref, mod_ref, w0_ref, w1_ref, ys_ref, lng_ref, lnb_ref, o_ref,
                    ya_ref, yb_ref, sem, *, tm):
    def issue(r, _):
        pltpu.make_async_copy(ys_ref.at[pl.ds(d0_ref[0, 0, r], 1)], ya_ref.at[pl.ds(r, 1)], sem).start()
        pltpu.make_async_copy(ys_ref.at[pl.ds(d1_ref[0, 0, r], 1)], yb_ref.at[pl.ds(r, 1)], sem).start()
        return 0

    lax.fori_loop(0, tm, issue, 0)
    pltpu.make_async_copy(ys_ref.at[pl.ds(0, tm)], ya_ref, sem).wait()
    pltpu.make_async_copy(ys_ref.at[pl.ds(0, tm)], yb_ref, sem).wait()
    m = mod_ref[0]
    f = w0_ref[:, 0:1] * ya_ref[...] + w1_ref[:, 0:1] * yb_ref[...]
    o_ref[...] = _layer_norm(ALPHA * x_ref[...] + m[5:6, :] * f, lng_ref[...], lnb_ref[...])


def _moe_out(rows, n_rows, x1, mods, ys, dest0, dest1, w0, w1, ln_g, ln_b):
    tm, d = rows.tm, x1.shape[1]
    nt = n_rows // tm
    row = pl.BlockSpec((tm, d), lambda i: (i, 0))
    idx = pl.BlockSpec((1, 1, tm), lambda i: (i, 0, 0), memory_space=pltpu.SMEM)
    wcol = pl.BlockSpec((tm, 128), lambda i: (i, 0))
    bcast = lambda w: jnp.broadcast_to(w[:, None], (n_rows, 128))
    return pl.pallas_call(
        functools.partial(_moe_out_kernel, tm=tm),
        grid=(nt,),
        in_specs=[idx, idx, row, pl.BlockSpec((1, 6, d), lambda i: (rows.group(i), 0, 0)), wcol, wcol,
                  pl.BlockSpec(memory_space=pl.ANY), _full(ln_g.shape), _full(ln_b.shape)],
        out_specs=row,
        out_shape=jax.ShapeDtypeStruct((n_rows, d), F32),
        scratch_shapes=[pltpu.VMEM((tm, d), F32), pltpu.VMEM((tm, d), F32), pltpu.SemaphoreType.DMA(())],
        compiler_params=_params(("arbitrary",)),
        name="moe_out",
    )(dest0.reshape(nt, 1, tm), dest1.reshape(nt, 1, tm), x1, mods, bcast(w0), bcast(w1), ys, ln_g, ln_b)


def _moe(rows, n_rows, x1, h2, logits_t, mods, router_bias, wgu, wd, ln_g, ln_b):
    e0, e1, w0, w1 = _route(logits_t, router_bias)
    dest0, dest1, pad_lo, pad_hi, tile_expert, n_used, p_max = _dispatch_plan(e0, e1, n_rows)
    xs = _dispatch(h2, n_rows, dest0, dest1, pad_lo, pad_hi, n_used, p_max, rows.tm)
    ys = _expert_ffn(xs, tile_expert, n_used, wgu, wd)
    return _moe_out(rows, n_rows, x1, mods, ys, dest0, dest1, w0, w1, ln_g, ln_b)


def _odd_in_kernel(x_ref, mod_ref, w_ref, cos_ref, sin_ref, gq_ref, gk_ref, hm_ref, q_ref, k_ref, v_ref, *, qw, kw):
    m = mod_ref[0]
    h = (x_ref[...] * (1.0 + m[1:2, :]) + m[0:1, :]).astype(BF16)
    r = jnp.dot(h, w_ref[...], preferred_element_type=F32)
    q, q_sw = r[:, :qw], r[:, qw:2 * qw]
    o = 2 * qw
    k, k_sw = r[:, o:o + kw], r[:, o + kw:o + 2 * kw]
    v = r[:, o + 2 * kw:o + 3 * kw]
    cos, sin = cos_ref[...], sin_ref[...]
    hm = hm_ref[...]

    def norm_rope(t, t_sw, gains, width):
        hi, lo = _split_bf16(t * t)
        ms = (jnp.dot(hi, hm[:width, :width], preferred_element_type=F32)
              + jnp.dot(lo, hm[:width, :width], preferred_element_type=F32))
        rs = lax.rsqrt(ms + RMS_EPS)
        c = jnp.concatenate([cos] * (width // 128), axis=1)
        s = jnp.concatenate([sin] * (width // 128), axis=1)
        return (t * gains[0:1, :] * c + t_sw * gains[1:2, :] * s) * rs

    q_ref[...] = (norm_rope(q, q_sw, gq_ref[...], qw) * (HEAD_DIM ** -0.5)).astype(BF16)
    k_ref[...] = norm_rope(k, k_sw, gk_ref[...], kw).astype(BF16)
    v_ref[...] = v.astype(BF16)


def _odd_in(rows, x_all, mods, w_cat, cos_t, sin_t, gq, gk, head_mean, qw, kw):
    tm, d = rows.tm, x_all.shape[1]
    n = rows.n_all
    row = lambda w: pl.BlockSpec((tm, w), lambda i: (i, 0))
    tab = pl.BlockSpec((tm, 128), lambda i: (rows.rope_block(i), 0))
    return pl.pallas_call(
        functools.partial(_odd_in_kernel, qw=qw, kw=kw),
        grid=(n // tm,),
        in_specs=[row(d), pl.BlockSpec((1, 6, d), lambda i: (rows.group(i), 0, 0)), _full(w_cat.shape), tab, tab,
                  _full(gq.shape), _full(gk.shape), _full(head_mean.shape)],
        out_specs=[row(qw), row(kw), row(kw)],
        out_shape=[jax.ShapeDtypeStruct((n, qw), BF16), jax.ShapeDtypeStruct((n, kw), BF16),
                   jax.ShapeDtypeStruct((n, kw), BF16)],
        compiler_params=_params(("parallel",)),
        name="odd_in",
    )(x_all, mods, w_cat, cos_t, sin_t, gq, gk, head_mean)


def _dense_attn_kernel(q_ref, k_ref, v_ref, o_ref):
    s = _nt_dot(q_ref[0, 0], k_ref[0, 0])
    p = jnp.exp(s - jnp.max(s, axis=-1, keepdims=True))
    l = jnp.sum(p, axis=-1, keepdims=True)
    o = jnp.dot(p.astype(BF16), v_ref[0, 0], preferred_element_type=F32)
    o_ref[0, 0] = (o / l).astype(o_ref.dtype)


def _dense_attn(q_heads, k_all, v_all):
    bsz, n_h, seq, _ = q_heads.shape
    hkv, n_k = k_all.shape[1], k_all.shape[2]
    grp = n_h // hkv
    tq = DENSE_TQ
    assert seq % tq == 0
    qs = pl.BlockSpec((1, 1, tq, HEAD_DIM), lambda b, h, j: (b, h, j, 0))
    kvs = pl.BlockSpec((1, 1, n_k, HEAD_DIM), lambda b, h, j: (b, h // grp, 0, 0))
    return pl.pallas_call(
        _dense_attn_kernel,
        grid=(bsz, n_h, seq // tq),
        in_specs=[qs, kvs, kvs],
        out_specs=qs,
        out_shape=jax.ShapeDtypeStruct(q_heads.shape, BF16),
        compiler_params=_params(("parallel", "parallel", "arbitrary")),
        name="dense_attn",
    )(q_heads, k_all, v_all)


def _rope_tables(seq, tm):
    n_freq = HEAD_DIM // 4
    inv_freq = ROPE_THETA ** (-jnp.arange(n_freq, dtype=F32) / n_freq)
    rows = seq // GRID_W
    r = jnp.repeat(jnp.arange(rows, dtype=F32), GRID_W)
    col = jnp.tile(jnp.arange(GRID_W, dtype=F32), rows)
    ang = jnp.concatenate([r[:, None] * inv_freq, col[:, None] * inv_freq], -1)
    cos, sin = jnp.cos(ang), jnp.sin(ang)
    cos_t = jnp.concatenate([jnp.tile(cos, (1, 4)), jnp.ones((tm, 128), F32)], axis=0)
    sin_t = jnp.concatenate([jnp.tile(sin, (1, 4)), jnp.zeros((tm, 128), F32)], axis=0)
    return cos_t, sin_t


def _swap_halves(w):
    d, n = w.shape
    t = w.reshape(d, n // HEAD_DIM, 2, HEAD_DIM // 2)
    return jnp.stack([-t[:, :, 1], t[:, :, 0]], axis=2).reshape(d, n)


def _to_heads(t_rows, bsz, length, hkv):
    return t_rows.reshape(bsz, length, hkv, HEAD_DIM).transpose(0, 2, 1, 3)


def kernel(x, c, ctx, c_ctx, ada_w, ada_b, ln_g, ln_b, even_w_in, even_w_out, s5_lam_re, s5_lam_im, s5_log_step,
           s5_b_re, s5_b_im, s5_c_re, s5_c_im, s5_d, s5_w_glu, s5_b_glu, win_sink, odd_w_in, odd_w_out,
           odd_q_norm, odd_k_norm, router_w, router_bias, moe_w_gate, moe_w_up, moe_w_down):
    bsz, seq, d = x.shape
    n_ctx = ctx.shape[1]
    assert ada_w.shape[0] == DEPTH == 2
    rows = _Rows(bsz, seq, n_ctx)
    n_lat, n_all = rows.n_lat, rows.n_all
    s5w = s5_d.shape[1]
    win_q = win_sink.shape[1] * HEAD_DIM
    win_kv = (even_w_in.shape[2] - s5w - win_q) // 2
    win_hkv = win_kv // HEAD_DIM
    odd_q = odd_w_out.shape[1]
    odd_kv = (odd_w_in.shape[2] - odd_q) // 2
    odd_hkv = odd_kv // HEAD_DIM

    cond = jnp.zeros((16, d), F32).at[:bsz].set(c).at[bsz].set(c_ctx)
    mods = _ada(cond, ada_w, ada_b).reshape(DEPTH, 16, 6, d)
    cos_t, sin_t = _rope_tables(seq, rows.tm)
    rw_hi, rw_lo = _split_bf16(router_w.T)
    x_all = jnp.concatenate([x.reshape(n_lat, d), ctx.reshape(bsz * n_ctx, d)], axis=0)
    lnv = lambda i, j: (ln_g[i, j].reshape(1, d), ln_b[i, j].reshape(1, d))
    moe_w = lambda i: (jnp.concatenate([moe_w_gate[i], moe_w_up[i]], axis=-1).astype(BF16), moe_w_down[i].astype(BF16))

    w = even_w_in[0]
    wu, wq, wk, wv = (w[:, :s5w], w[:, s5w:s5w + win_q], w[:, s5w + win_q:s5w + win_q + win_kv],
                      w[:, s5w + win_q + win_kv:])
    w_cat = jnp.concatenate([wu, wq, _swap_halves(wq), wk, _swap_halves(wk), wv], axis=1).astype(BF16)
    u32, u_tm, q0, k0, v0 = _even_in(rows, x_all, mods[0], w_cat, cos_t, sin_t, s5w, win_q, win_kv)
    wb, a_b, wc = _s5_weights(s5_lam_re[0], s5_lam_im[0], s5_log_step[0], s5_b_re[0], s5_b_im[0],
                              s5_c_re[0], s5_c_im[0], bsz)
    t_all = seq + n_ctx
    y_tm = _s5(u_tm.reshape(t_all * bsz, s5w), wb, a_b, wc, bsz, n_ctx, seq).reshape(2, t_all, bsz * s5w)
    k_lat, v_lat = _to_heads(k0[:n_lat], bsz, seq, win_hkv), _to_heads(v0[:n_lat], bsz, seq, win_hkv)
    k_ctx, v_ctx = _to_heads(k0[n_lat:], bsz, n_ctx, win_hkv), _to_heads(v0[n_lat:], bsz, n_ctx, win_hkv)
    sink = win_sink[0].astype(F32)
    o_lat = _win_attn(q0[:n_lat], k_ctx, v_ctx, k_lat, v_lat, sink, bsz, seq, tq=256)
    o_ctx = _ctx_attn(q0[n_lat:], k_ctx, v_ctx, sink, bsz)
    o_rows = jnp.concatenate([o_lat, o_ctx], axis=0)
    g0, b0 = lnv(0, 0)
    even_args = (u32, y_tm, s5_d[0].reshape(1, s5w), s5_w_glu[0].astype(BF16), s5_b_glu[0].reshape(1, s5w))
    x1, h2, lg = _mix_out(rows, n_all, x_all, mods[0], o_rows, even_w_out[0].astype(BF16), g0, b0, rw_hi, rw_lo,
                          even_args)
    g1, b1 = lnv(0, 1)
    x2 = _moe(rows, n_all, x1, h2, lg, mods[0], router_bias.astype(F32), *moe_w(0), g1, b1)

    w = odd_w_in[0]
    wq, wk, wv = w[:, :odd_q], w[:, odd_q:odd_q + odd_kv], w[:, odd_q + odd_kv:]
    w_cat = jnp.concatenate([wq, _swap_halves(wq), wk, _swap_halves(wk), wv], axis=1).astype(BF16)

    def gains(gv, width):
        gs = jnp.concatenate([gv[HEAD_DIM // 2:], gv[:HEAD_DIM // 2]])
        return jnp.stack([jnp.tile(gv, width // HEAD_DIM), jnp.tile(gs, width // HEAD_DIM)]).astype(F32)

    head_mean = jnp.kron(jnp.eye(odd_q // HEAD_DIM, dtype=F32),
                         jnp.full((HEAD_DIM, HEAD_DIM), 1.0 / HEAD_DIM, F32)).astype(BF16)
    q1, k1, v1 = _odd_in(rows, x2, mods[1], w_cat, cos_t, sin_t, gains(odd_q_norm[0], odd_q),
                         gains(odd_k_norm[0], odd_kv), head_mean, odd_q, odd_kv)
    cat_kv = lambda t: jnp.concatenate([_to_heads(t[n_lat:], bsz, n_ctx, odd_hkv),
                                        _to_heads(t[:n_lat], bsz, seq, odd_hkv)], axis=2)
    n_qh = odd_q // HEAD_DIM
    q_heads = q1[:n_lat].reshape(bsz, seq, n_qh, HEAD_DIM).transpose(0, 2, 1, 3)
    o1 = _dense_attn(q_heads, cat_kv(k1), cat_kv(v1)).transpose(0, 2, 1, 3).reshape(n_lat, odd_q)
    g0, b0 = lnv(1, 0)
    x1, h2, lg = _mix_out(rows, n_lat, x2, mods[1], o1, odd_w_out[0].astype(BF16), g0, b0, rw_hi, rw_lo)
    g1, b1 = lnv(1, 1)
    out = _moe(rows, n_lat, x1, h2, lg, mods[1], router_bias.astype(F32), *moe_w(1), g1, b1)
    return out.reshape(bsz, seq, d)
```

```python
import functools
import math

import jax
import jax.numpy as jnp
from jax import lax
from jax.experimental import pallas as pl
from jax.experimental.pallas import tpu as pltpu

F32 = jnp.float32
BF16 = jnp.bfloat16

HEAD_DIM = 64
GRID_W = 64
ROPE_THETA = 10000.0
S5_GROUP = 16
S5_STATE = 64
WINDOW = 128
N_EXPERTS = 32
N_GROUPS = 8
EXPERTS_PER_GROUP = N_EXPERTS // N_GROUPS
TOP_K = 2
DEPTH = 2
ALPHA = (2 * DEPTH) ** 0.25
LN_EPS = 1e-5
LOG2_E = math.log2(math.e)
RMS_EPS = 1e-6
NEG_INF = -1e30

ROW_TILE = 256
S5_CHUNK = 64
S5_LANE_GROUP = 512
EXPERT_TILE = 256
RANK_BLOCK = 512
DENSE_TQ = 256
VMEM_LIMIT = 48 * 1024 * 1024


def _params(sem):
    return pltpu.CompilerParams(dimension_semantics=sem, vmem_limit_bytes=VMEM_LIMIT)


def _full(shape):
    n = len(shape)
    return pl.BlockSpec(shape, lambda *_: (0,) * n)


def _ada_kernel(c_ref, w_ref, b_ref, o_ref):
    c = c_ref[...]
    s = c * jax.nn.sigmoid(c)
    o_ref[...] = jnp.dot(s, w_ref[0], preferred_element_type=F32, precision=lax.Precision.HIGHEST) + b_ref[0]


def _ada(cond, ada_w, ada_b):
    g, d = cond.shape
    depth, _, n = ada_w.shape
    bn = 1024
    return pl.pallas_call(
        _ada_kernel,
        grid=(depth, n // bn),
        in_specs=[pl.BlockSpec((g, d), lambda i, j: (0, 0)),
                  pl.BlockSpec((1, d, bn), lambda i, j: (i, 0, j)),
                  pl.BlockSpec((1, 1, bn), lambda i, j: (i, 0, j))],
        out_specs=pl.BlockSpec((None, g, bn), lambda i, j: (i, 0, j)),
        out_shape=jax.ShapeDtypeStruct((depth, g, n), F32),
        compiler_params=_params(("arbitrary", "arbitrary")),
        name="ada",
    )(cond, ada_w, ada_b.reshape(depth, 1, n))


class _Rows:
    def __init__(self, bsz, seq, ctx, tm=ROW_TILE):
        assert seq % tm == 0 and ctx % tm == 0
        self.bsz, self.seq, self.ctx, self.tm = bsz, seq, ctx, tm
        self.tpb = seq // tm
        self.cpb = ctx // tm
        self.nl = bsz * self.tpb
        self.nc = bsz * self.cpb
        self.n_lat = bsz * seq
        self.n_all = bsz * (seq + ctx)

    def two_source(self, width):
        return (pl.BlockSpec((self.tm, width), lambda i: (jnp.minimum(i, self.nl - 1), 0)),
                pl.BlockSpec((self.tm, width), lambda i: (jnp.maximum(i - self.nl, 0), 0)))

    def group(self, i):
        return jnp.where(i < self.nl, i // self.tpb, self.bsz)

    def rope_block(self, i):
        return jnp.where(i < self.nl, i % self.tpb, self.tpb)

    def time_major(self, i):
        lat = i < self.nl
        j = i - self.nl
        return (jnp.where(lat, self.cpb + i % self.tpb, j % self.cpb),
                jnp.where(lat, i // self.tpb, j // self.cpb))


def _layer_norm(r, g, b):
    mu = jnp.mean(r, axis=-1, keepdims=True)
    rc = r - mu
    var = jnp.mean(rc * rc, axis=-1, keepdims=True)
    return rc * lax.rsqrt(var + LN_EPS) * g + b


def _even_in_kernel(xl_ref, xc_ref, mod_ref, w_ref, cos_ref, sin_ref, u32_ref, utm_ref, q_ref, k_ref, v_ref, *,
                    s5w, qw, kw, n_lat_tiles):
    m = mod_ref[0]
    x = jnp.where(pl.program_id(0) < n_lat_tiles, xl_ref[...], xc_ref[...])
    h = (x * (1.0 + m[1:2, :]) + m[0:1, :]).astype(BF16)
    r = jnp.dot(h, w_ref[...], preferred_element_type=F32)
    u = r[:, :s5w]
    u32_ref[...] = u
    utm_ref[...] = u.astype(BF16)
    o = s5w
    q, q_sw = r[:, o:o + qw], r[:, o + qw:o + 2 * qw]
    o += 2 * qw
    k, k_sw = r[:, o:o + kw], r[:, o + kw:o + 2 * kw]
    o += 2 * kw
    v = r[:, o:o + kw]
    cos, sin = cos_ref[...], sin_ref[...]
    cq = jnp.concatenate([cos] * (qw // 128), axis=1)
    sq = jnp.concatenate([sin] * (qw // 128), axis=1)
    q_ref[...] = ((q * cq + q_sw * sq) * (HEAD_DIM ** -0.5)).astype(BF16)
    k_ref[...] = (k * cos + k_sw * sin).astype(BF16)
    v_ref[...] = v.astype(BF16)


def _even_in(rows, x_lat, x_ctx, mods, w_cat, cos_t, sin_t, s5w, qw, kw):
    tm, d = rows.tm, x_lat.shape[1]
    n = rows.n_all
    t_all = rows.seq + rows.ctx
    kern = functools.partial(_even_in_kernel, s5w=s5w, qw=qw, kw=kw, n_lat_tiles=rows.nl)
    row = lambda w: pl.BlockSpec((tm, w), lambda i: (i, 0))
    return pl.pallas_call(
        kern,
        grid=(n // tm,),
        in_specs=list(rows.two_source(d)) + [
                  pl.BlockSpec((1, 6, d), lambda i: (rows.group(i), 0, 0)),
                  _full(w_cat.shape),
                  pl.BlockSpec((tm, 128), lambda i: (rows.rope_block(i), 0)),
                  pl.BlockSpec((tm, 128), lambda i: (rows.rope_block(i), 0))],
        out_specs=[row(s5w),
                   pl.BlockSpec((tm, s5w), lambda i: rows.time_major(i)),
                   row(qw), row(kw), row(kw)],
        out_shape=[jax.ShapeDtypeStruct((n, s5w), F32),
                   jax.ShapeDtypeStruct((t_all, rows.bsz * s5w), BF16),
                   jax.ShapeDtypeStruct((n, qw), BF16),
                   jax.ShapeDtypeStruct((n, kw), BF16),
                   jax.ShapeDtypeStruct((n, kw), BF16)],
        compiler_params=_params(("parallel",)),
        name="even_in",
    )(x_lat, x_ctx, mods, w_cat, cos_t, sin_t)


def _s5_kernel(u_ref, wb_ref, a_ref, wc_ref, y_ref, bu_ref, h_ref, *, tc, nb, width, n_state):
    lg = S5_LANE_GROUP
    n_lg = n_state // lg
    kch = width // n_lg
    d = pl.program_id(0)

    @pl.when(pl.program_id(1) == 0)
    def _():
        h_ref[...] = jnp.zeros_like(h_ref)

    u = u_ref[...]
    for g in range(n_lg):
        r = jnp.dot(u[:, kch * g:kch * (g + 1)], wb_ref[0, g], preferred_element_type=F32)
        bu_ref[:, lg * g:lg * (g + 1)] = r[:, :lg]
        bu_ref[:, n_state + lg * g:n_state + lg * (g + 1)] = r[:, lg:]

    rev = d == 1
    for g in range(n_lg):
        re = slice(lg * g, lg * (g + 1))
        im = slice(n_state + lg * g, n_state + lg * (g + 1))
        a_re = a_ref[0, :, re]
        a_im = a_ref[0, :, im]

        def body(i, carry, re=re, im=im, a_re=a_re, a_im=a_im):
            hr, hi = carry
            t = jnp.where(rev, tc - 1 - i, i)
            row = pl.multiple_of(t * nb, nb)
            br = bu_ref[pl.ds(row, nb), re]
            bi = bu_ref[pl.ds(row, nb), im]
            nr = a_re * hr - a_im * hi + br
            ni = a_re * hi + a_im * hr + bi
            bu_ref[pl.ds(row, nb), re] = nr
            bu_ref[pl.ds(row, nb), im] = ni
            return nr, ni

        hr, hi = lax.fori_loop(0, tc, body, (h_ref[:, re], h_ref[:, im]), unroll=4)
        h_ref[:, re] = hr
        h_ref[:, im] = hi

    for g in range(n_lg):
        hc = jnp.concatenate([bu_ref[:, lg * g:lg * (g + 1)],
                              bu_ref[:, n_state + lg * g:n_state + lg * (g + 1)]], axis=1).astype(BF16)
        y_ref[0, :, kch * g:kch * (g + 1)] = jnp.dot(hc, wc_ref[0, g], preferred_element_type=F32)


def _s5(u_tm, wb, a_b, wc, nb, ctx, seq):
    tc = S5_CHUNK
    t_all = ctx + seq
    width = u_tm.shape[1]
    n_state = a_b.shape[2] // 2
    assert ctx % tc == 0 and seq % tc == 0 and nb == 8
    n_ctx, n_chunks = ctx // tc, t_all // tc

    def chunk(d, s):
        back = jnp.where(s < n_ctx, n_ctx - 1 - s, n_chunks - 1 - (s - n_ctx))
        return jnp.where(d == 0, s, back)

    kern = functools.partial(_s5_kernel, tc=tc, nb=nb, width=width, n_state=n_state)
    return pl.pallas_call(
        kern,
        grid=(2, n_chunks),
        in_specs=[pl.BlockSpec((tc * nb, width), lambda d, s: (chunk(d, s), 0)),
                  pl.BlockSpec((1,) + wb.shape[1:], lambda d, s: (d, 0, 0, 0)),
                  pl.BlockSpec((1,) + a_b.shape[1:], lambda d, s: (d, 0, 0)),
                  pl.BlockSpec((1,) + wc.shape[1:], lambda d, s: (d, 0, 0, 0))],
        out_specs=pl.BlockSpec((1, tc * nb, width), lambda d, s: (d, chunk(d, s), 0)),
        out_shape=jax.ShapeDtypeStruct((2, t_all * nb, width), F32),
        scratch_shapes=[pltpu.VMEM((tc * nb, 2 * n_state), F32), pltpu.VMEM((nb, 2 * n_state), F32)],
        compiler_params=_params(("arbitrary", "arbitrary")),
        name="s5_scan",
    )(u_tm, wb, a_b, wc)


def _s5_weights(lam_re, lam_im, log_step, b_re, b_im, c_re, c_im, nb):
    n_dir, n_grp, n_p = lam_re.shape
    ch = b_re.shape[-1]
    lr, li = lam_re.astype(F32), lam_im.astype(F32)
    dt = jnp.exp(log_step.astype(F32))[..., None]
    mag = jnp.exp(lr * dt)
    ab_re, ab_im = mag * jnp.cos(li * dt), mag * jnp.sin(li * dt)
    den = lr * lr + li * li
    num_re, num_im = ab_re - 1.0, ab_im
    f_re = (num_re * lr + num_im * li) / den
    f_im = (num_im * lr - num_re * li) / den
    br, bi = b_re.astype(F32), b_im.astype(F32)
    bb_re = f_re[..., None] * br - f_im[..., None] * bi
    bb_im = f_re[..., None] * bi + f_im[..., None] * br
    n_state = n_grp * n_p
    n_lg = n_state // S5_LANE_GROUP
    gpl = n_grp // n_lg
    eye = jnp.eye(gpl, dtype=F32)

    def pack_in(bb):
        t = bb.reshape(n_dir, n_lg, gpl, n_p, ch)
        m = jnp.einsum('dlgpc,gh->dlgchp', t, eye)
        return m.reshape(n_dir, n_lg, gpl * ch, gpl * n_p)

    def pack_out(c):
        t = c.astype(F32).reshape(n_dir, n_lg, gpl, ch, n_p)
        m = jnp.einsum('dlgcp,gh->dlgphc', t, eye)
        return m.reshape(n_dir, n_lg, gpl * n_p, gpl * ch)

    wb = jnp.concatenate([pack_in(bb_re), pack_in(bb_im)], axis=-1).astype(BF16)
    wc = jnp.concatenate([pack_out(c_re), -pack_out(c_im)], axis=-2).astype(BF16)
    a = jnp.concatenate([ab_re.reshape(n_dir, n_state), ab_im.reshape(n_dir, n_state)], axis=-1)
    a_b = jnp.broadcast_to(a[:, None, :], (n_dir, nb, 2 * n_state))
    return wb, a_b, wc


def _nt_dot(a, b):
    return lax.dot_general(a, b, (((1,), (1,)), ((), ())), preferred_element_type=F32)


def _win_attn_kernel(sink_ref, q_ref, kc_ref, vc_ref, *rest, tq, seq, grp, local):
    if local:
        kl_ref, vl_ref, o_ref = rest
    else:
        (o_ref,) = rest
    g = pl.program_id(2) if local else pl.program_id(1)
    kc, vc = kc_ref[0, 0], vc_ref[0, 0]
    if local:
        band = tq + 2 * WINDOW
        j = pl.program_id(1)
        start = pl.multiple_of(jnp.clip(j * tq - WINDOW, 0, seq - band), 8)
        kb = kl_ref[0, 0, pl.ds(start, band), :]
        vb = vl_ref[0, 0, pl.ds(start, band), :]
        qpos = j * tq + lax.broadcasted_iota(jnp.int32, (tq, band), 0)
        kpos = start + lax.broadcasted_iota(jnp.int32, (tq, band), 1)
        valid = jnp.abs(qpos - kpos) <= WINDOW
    outs = []
    for hh in range(grp):
        q = q_ref[:, hh * HEAD_DIM:(hh + 1) * HEAD_DIM]
        sink = sink_ref[g * grp + hh]
        s_c = _nt_dot(q, kc)
        m = jnp.maximum(jnp.max(s_c, axis=-1, keepdims=True), sink)
        if local:
            s_l = jnp.where(valid, _nt_dot(q, kb), NEG_INF)
            m = jnp.maximum(m, jnp.max(s_l, axis=-1, keepdims=True))
        p_c = jnp.exp(s_c - m)
        l = jnp.sum(p_c, axis=-1, keepdims=True) + jnp.exp(sink - m)
        acc = jnp.dot(p_c.astype(BF16), vc, preferred_element_type=F32)
        if local:
            p_l = jnp.exp(s_l - m)
            l = l + jnp.sum(p_l, axis=-1, keepdims=True)
            acc = acc + jnp.dot(p_l.astype(BF16), vb, preferred_element_type=F32)
        outs.append(acc / l)
    o_ref[...] = jnp.concatenate(outs, axis=1).astype(o_ref.dtype)


def _win_attn(q_rows, k_ctx, v_ctx, k_lat, v_lat, sink, bsz, seq, tq):
    hkv = k_ctx.shape[1]
    grp = q_rows.shape[1] // (hkv * HEAD_DIM)
    gw = grp * HEAD_DIM
    ctx = k_ctx.shape[2]
    nq = seq // tq
    assert seq >= tq + 2 * WINDOW and tq % 128 == 0
    kern = functools.partial(_win_attn_kernel, tq=tq, seq=seq, grp=grp, local=True)
    kv = lambda n: pl.BlockSpec((1, 1, n, HEAD_DIM), lambda b, j, g: (b, g, 0, 0))
    return pl.pallas_call(
        kern,
        grid=(bsz, nq, hkv),
        in_specs=[pl.BlockSpec(memory_space=pltpu.SMEM),
                  pl.BlockSpec((tq, gw), lambda b, j, g: (b * nq + j, g)),
                  kv(ctx), kv(ctx), kv(seq), kv(seq)],
        out_specs=pl.BlockSpec((tq, gw), lambda b, j, g: (b * nq + j, g)),
        out_shape=jax.ShapeDtypeStruct(q_rows.shape, BF16),
        compiler_params=_params(("parallel", "parallel", "arbitrary")),
        name="win_attn",
    )(sink, q_rows, k_ctx, v_ctx, k_lat, v_lat)


def _ctx_attn(q_rows, k_ctx, v_ctx, sink, bsz):
    hkv, ctx = k_ctx.shape[1], k_ctx.shape[2]
    grp = q_rows.shape[1] // (hkv * HEAD_DIM)
    gw = grp * HEAD_DIM
    kern = functools.partial(_win_attn_kernel, tq=ctx, seq=ctx, grp=grp, local=False)
    kv = pl.BlockSpec((1, 1, ctx, HEAD_DIM), lambda b, g: (b, g, 0, 0))
    return pl.pallas_call(
        kern,
        grid=(bsz, hkv),
        in_specs=[pl.BlockSpec(memory_space=pltpu.SMEM),
                  pl.BlockSpec((ctx, gw), lambda b, g: (b, g)), kv, kv],
        out_specs=pl.BlockSpec((ctx, gw), lambda b, g: (b, g)),
        out_shape=jax.ShapeDtypeStruct(q_rows.shape, BF16),
        compiler_params=_params(("parallel", "arbitrary")),
        name="ctx_attn",
    )(sink, q_rows, k_ctx, v_ctx)


def _gelu_tanh(y):
    return 0.5 * y * (1.0 + jnp.tanh(math.sqrt(2.0 / math.pi) * (y + 0.044715 * (y * y * y))))


def _split_bf16(v):
    hi = v.astype(BF16)
    return hi, (v - hi.astype(F32)).astype(BF16)


def _mix_out_kernel(*refs, even, n_lat_tiles):
    if even:
        (xl_ref, xc_ref, mod_ref, o_ref, wout_ref, lng_ref, lnb_ref, rwh_ref, rwl_ref,
         u_ref, yf_ref, yb_ref, dsk_ref, wglu_ref, bglu_ref, x1_ref, h2_ref, lg_ref) = refs
        x = jnp.where(pl.program_id(0) < n_lat_tiles, xl_ref[...], xc_ref[...])
        y = u_ref[...] * dsk_ref[...] + yf_ref[0] + yb_ref[0]
        z = _gelu_tanh(y)
        gate = jax.nn.sigmoid(jnp.dot(z.astype(BF16), wglu_ref[...], preferred_element_type=F32) + bglu_ref[...])
        mix = jnp.concatenate([(z * gate).astype(BF16), o_ref[...]], axis=1)
    else:
        (x_ref, mod_ref, o_ref, wout_ref, lng_ref, lnb_ref, rwh_ref, rwl_ref, x1_ref, h2_ref, lg_ref) = refs
        x = x_ref[...]
        mix = o_ref[...]
    m = mod_ref[0]
    ol = jnp.dot(mix, wout_ref[...], preferred_element_type=F32)
    x1 = _layer_norm(ALPHA * x + m[2:3, :] * ol, lng_ref[...], lnb_ref[...])
    x1_ref[...] = x1
    h2 = x1 * (1.0 + m[4:5, :]) + m[3:4, :]
    h2_ref[...] = h2
    hh, hl = _split_bf16(h2)
    rwh, rwl = rwh_ref[...], rwl_ref[...]
    lg_ref[...] = _nt_dot(rwh, hh) + (_nt_dot(rwh, hl) + _nt_dot(rwl, hh))


def _mix_out(rows, n_rows, x_src, mods, o_rows, w_out, ln_g, ln_b, rw_hi, rw_lo, even_args=None):
    even = even_args is not None
    tm, d = rows.tm, w_out.shape[1]
    row = lambda w: pl.BlockSpec((tm, w), lambda i: (i, 0))
    x_ins, x_specs = (list(x_src), list(rows.two_source(d))) if even else ([x_src], [row(d)])
    ins = x_ins + [mods, o_rows, w_out, ln_g, ln_b, rw_hi, rw_lo]
    specs = x_specs + [pl.BlockSpec((1, 6, d), lambda i: (rows.group(i), 0, 0)), row(o_rows.shape[1]),
                       _full(w_out.shape), _full(ln_g.shape), _full(ln_b.shape), _full(rw_hi.shape),
                       _full(rw_lo.shape)]
    if even:
        u32, y_tm, d_skip, w_glu, b_glu = even_args
        s5w = u32.shape[1]
        tmaj = lambda dr: pl.BlockSpec((1, tm, s5w), lambda i: (dr,) + tuple(rows.time_major(i)))
        ins += [u32, y_tm, y_tm, d_skip, w_glu, b_glu]
        specs += [row(s5w), tmaj(0), tmaj(1), _full(d_skip.shape), _full(w_glu.shape), _full(b_glu.shape)]
    n_e = rw_hi.shape[0]
    return pl.pallas_call(
        functools.partial(_mix_out_kernel, even=even, n_lat_tiles=rows.nl),
        grid=(n_rows // tm,),
        in_specs=specs,
        out_specs=[row(d), row(d), pl.BlockSpec((n_e, tm), lambda i: (0, i))],
        out_shape=[jax.ShapeDtypeStruct((n_rows, d), F32), jax.ShapeDtypeStruct((n_rows, d), F32),
                   jax.ShapeDtypeStruct((n_e, n_rows), F32)],
        compiler_params=_params(("parallel",)),
        name="mix_out_even" if even else "mix_out_odd",
    )(*ins)


def _route_kernel(lg_ref, bias_ref, e0_ref, e1_ref, w0_ref, w1_ref):
    epg = EXPERTS_PER_GROUP
    sc = [jax.nn.sigmoid(lg_ref[e]) for e in range(N_EXPERTS)]
    sel = [sc[e] + bias_ref[e] for e in range(N_EXPERTS)]
    best, gidx = None, None
    for g in range(N_GROUPS):
        a = sel[g * epg:(g + 1) * epg]
        pair = None
        for i in range(epg):
            for j in range(i + 1, epg):
                s2 = a[i] + a[j]
                pair = s2 if pair is None else jnp.maximum(pair, s2)
        if best is None:
            best, gidx = pair, jnp.zeros(pair.shape, jnp.int32)
        else:
            take = pair > best
            best = jnp.where(take, pair, best)
            gidx = jnp.where(take, g, gidx)
    sg = [sel[i] for i in range(epg)]
    cg = [sc[i] for i in range(epg)]
    for g in range(1, N_GROUPS):
        hit = gidx == g
        sg = [jnp.where(hit, sel[g * epg + i], sg[i]) for i in range(epg)]
        cg = [jnp.where(hit, sc[g * epg + i], cg[i]) for i in range(epg)]

    def first_max(vals, skip=None):
        bv, bi, bs = None, None, None
        for i in range(epg):
            v = vals[i] if skip is None else jnp.where(skip == i, -jnp.inf, vals[i])
            if bv is None:
                bv, bi, bs = v, jnp.zeros(v.shape, jnp.int32), cg[0]
            else:
                take = v > bv
                bv = jnp.where(take, v, bv)
                bi = jnp.where(take, i, bi)
                bs = jnp.where(take, cg[i], bs)
        return bi, bs

    i0, s0 = first_max(sg)
    i1, s1 = first_max(sg, skip=i0)
    tot = s0 + s1
    e0_ref[...] = gidx * epg + i0
    e1_ref[...] = gidx * epg + i1
    w0_ref[...] = s0 / tot
    w1_ref[...] = s1 / tot


def _route(logits_t, bias):
    n_e, n = logits_t.shape
    r = n // 128
    rb = 8
    assert n % (128 * rb) == 0
    lg3 = logits_t.reshape(n_e, r, 128)
    blk = pl.BlockSpec((rb, 128), lambda i: (i, 0))
    outs = pl.pallas_call(
        _route_kernel,
        grid=(r // rb,),
        in_specs=[pl.BlockSpec((n_e, rb, 128), lambda i: (0, i, 0)), pl.BlockSpec(memory_space=pltpu.SMEM)],
        out_specs=[blk, blk, blk, blk],
        out_shape=[jax.ShapeDtypeStruct((r, 128), jnp.int32)] * 2 + [jax.ShapeDtypeStruct((r, 128), F32)] * 2,
        compiler_params=_params(("parallel",)),
        name="route",
    )(lg3, bias)
    return [o.reshape(n) for o in outs]


def _rank_kernel(e_ref, tri_ref, rank_ref, cnt_ref, carry_ref, *, bw):
    @pl.when(pl.program_id(0) == 0)
    def _():
        carry_ref[...] = jnp.zeros_like(carry_ref)

    hit = lax.broadcasted_iota(jnp.int32, (N_EXPERTS, bw), 0) == e_ref[...]
    onehot = jnp.where(hit, 1.0, 0.0).astype(BF16)
    run = jnp.dot(onehot, tri_ref[...], preferred_element_type=F32) + carry_ref[...]
    rank_ref[...] = (jnp.sum(jnp.where(hit, run, 0.0), axis=0, keepdims=True) - 1.0).astype(jnp.int32)
    carry_ref[...] = run[:, bw - 1:bw]
    cnt_ref[...] = run[:, bw - 1:bw]


def _pair_ranks(e_pairs):
    n_pair = e_pairs.shape[0]
    bw = RANK_BLOCK
    assert n_pair % bw == 0
    tri = (jnp.arange(bw)[:, None] <= jnp.arange(bw)[None, :]).astype(BF16)
    rank, cnt = pl.pallas_call(
        functools.partial(_rank_kernel, bw=bw),
        grid=(n_pair // bw,),
        in_specs=[pl.BlockSpec((1, bw), lambda i: (0, i)), _full((bw, bw))],
        out_specs=[pl.BlockSpec((1, bw), lambda i: (0, i)), _full((N_EXPERTS, 1))],
        out_shape=[jax.ShapeDtypeStruct((1, n_pair), jnp.int32), jax.ShapeDtypeStruct((N_EXPERTS, 1), F32)],
        scratch_shapes=[pltpu.VMEM((N_EXPERTS, 1), F32)],
        compiler_params=_params(("arbitrary",)),
        name="pair_ranks",
    )(e_pairs.reshape(1, n_pair), tri)
    return rank.reshape(n_pair), cnt.reshape(N_EXPERTS).astype(jnp.int32)


def _dispatch_plan(e0, e1, n_tok):
    te = EXPERT_TILE
    p_max = -(-(TOP_K * n_tok + N_EXPERTS * (te - 1)) // te) * te
    e_pairs = jnp.concatenate([e0, e1])
    rank, counts = _pair_ranks(e_pairs)
    padded = (counts + te - 1) // te * te
    pad_end = jnp.cumsum(padded)
    pad_start = pad_end - padded
    is_e = e_pairs[:, None] == jnp.arange(N_EXPERTS, dtype=jnp.int32)[None, :]
    dest = rank + jnp.sum(jnp.where(is_e, pad_start[None, :], 0), axis=1)
    tiles = jnp.arange(p_max // te, dtype=jnp.int32) * te
    tile_expert = jnp.minimum(jnp.sum(tiles[:, None] >= pad_end[None, :], axis=1), N_EXPERTS - 1).astype(jnp.int32)
    n_used = (pad_end[-1] // te).astype(jnp.int32).reshape(1)
    return dest[:n_tok], dest[n_tok:], pad_start + counts, pad_end, tile_expert, n_used, p_max


def _dispatch_kernel(lo_ref, hi_ref, nu_ref, d0_ref, d1_ref, h_ref, xs_ref, stage_ref, zero_ref, sems, zsem, *, tm):
    i = pl.program_id(0)
    slot = i % 2
    stage_ref[slot] = h_ref[...]

    def issue(r, _):
        src = stage_ref.at[slot, pl.ds(r, 1)]
        pltpu.make_async_copy(src, xs_ref.at[pl.ds(d0_ref[0, 0, r], 1)], sems.at[slot]).start(priority=0)
        pltpu.make_async_copy(src, xs_ref.at[pl.ds(d1_ref[0, 0, r], 1)], sems.at[slot]).start(priority=1)
        return 0

    lax.fori_loop(0, tm, issue, 0, unroll=4)

    def wait_rows(s):
        pltpu.make_async_copy(xs_ref.at[pl.ds(0, 2 * tm)], xs_ref.at[pl.ds(0, 2 * tm)], sems.at[s]).wait()

    @pl.when(i > 0)
    def _():
        wait_rows(1 - slot)

    @pl.when(pl.program_id(0) == pl.num_programs(0) - 1)
    def _():
        zero_ref[...] = jnp.zeros_like(zero_ref)

        def pad_copy(r):
            return pltpu.make_async_copy(zero_ref.at[pl.ds(0, 1)], xs_ref.at[pl.ds(r, 1)], zsem)

        def per_expert(e, _):
            lax.fori_loop(lo_ref[e], hi_ref[e], lambda r, c: (pad_copy(r).start(), c)[1], 0)
            lax.fori_loop(lo_ref[e], hi_ref[e], lambda r, c: (pad_copy(r).wait(), c)[1], 0)
            return 0

        lax.fori_loop(0, N_EXPERTS, per_expert, 0)

        te = zero_ref.shape[0]

        def tail_copy(t):
            return pltpu.make_async_copy(zero_ref, xs_ref.at[pl.ds(pl.multiple_of(t * te, te), te)], zsem)

        n_tiles = xs_ref.shape[0] // te
        lax.fori_loop(nu_ref[0], n_tiles, lambda t, c: (tail_copy(t).start(), c)[1], 0)
        lax.fori_loop(nu_ref[0], n_tiles, lambda t, c: (tail_copy(t).wait(), c)[1], 0)
        wait_rows(slot)


def _dispatch(h2, n_rows, dest0, dest1, pad_lo, pad_hi, n_used, p_max, tm):
    d = h2.shape[1]
    nt = n_rows // tm
    idx = pl.BlockSpec((1, 1, tm), lambda i, *_: (i, 0, 0), memory_space=pltpu.SMEM)
    grid_spec = pltpu.PrefetchScalarGridSpec(
        num_scalar_prefetch=3,
        grid=(nt,),
        in_specs=[idx, idx, pl.BlockSpec((tm, d), lambda i, *_: (i, 0))],
        out_specs=pl.BlockSpec(memory_space=pl.ANY),
        scratch_shapes=[pltpu.VMEM((2, tm, d), h2.dtype), pltpu.VMEM((EXPERT_TILE, d), h2.dtype),
                        pltpu.SemaphoreType.DMA((2,)), pltpu.SemaphoreType.DMA(())],
    )
    return pl.pallas_call(
        functools.partial(_dispatch_kernel, tm=tm),
        grid_spec=grid_spec,
        out_shape=jax.ShapeDtypeStruct((p_max, d), h2.dtype),
        compiler_params=_params(("arbitrary",)),
        name="dispatch",
    )(pad_lo, pad_hi, n_used, dest0.reshape(nt, 1, tm), dest1.reshape(nt, 1, tm), h2)


def _expert_kernel(te_ref, nu_ref, x_ref, wg_ref, wu_ref, wd_ref, y_ref, wgu_bf, wd_bf, *, ff):
    i = pl.program_id(0)

    @pl.when((i == 0) | (te_ref[i] != te_ref[jnp.maximum(i - 1, 0)]))
    def _():
        wgu_bf[:, :ff] = wg_ref[0].astype(BF16)
        wgu_bf[:, ff:] = wu_ref[0].astype(BF16)
        wd_bf[...] = wd_ref[0].astype(BF16)

    @pl.when(i < nu_ref[0])
    def _():
        x = x_ref[...].astype(BF16)
        gu = jnp.dot(x, wgu_bf[...], preferred_element_type=F32)
        g, u = gu[:, :ff], gu[:, ff:]
        hid = (g * jax.nn.sigmoid(g) * u).astype(BF16)
        y_ref[...] = jnp.dot(hid, wd_bf[...], preferred_element_type=F32)

    @pl.when(i >= nu_ref[0])
    def _():
        y_ref[...] = jnp.zeros_like(y_ref)


def _expert_ffn(xs, tile_expert, n_used, w_gate, w_up, w_down):
    p, d = xs.shape
    te = EXPERT_TILE
    ff = w_down.shape[1]
    by_expert = lambda shape: pl.BlockSpec((1,) + shape, lambda i, t, n: (t[i], 0, 0))
    grid_spec = pltpu.PrefetchScalarGridSpec(
        num_scalar_prefetch=2,
        grid=(p // te,),
        in_specs=[pl.BlockSpec((te, d), lambda i, t, n: (jnp.minimum(i, n[0] - 1), 0)),
                  by_expert((d, ff)), by_expert((d, ff)), by_expert((ff, d))],
        out_specs=pl.BlockSpec((te, d), lambda i, t, n: (i, 0)),
        scratch_shapes=[pltpu.VMEM((d, 2 * ff), BF16), pltpu.VMEM((ff, d), BF16)],
    )
    return pl.pallas_call(
        functools.partial(_expert_kernel, ff=ff),
        grid_spec=grid_spec,
        out_shape=jax.ShapeDtypeStruct((p, d), F32),
        compiler_params=_params(("arbitrary",)),
        name="expert_ffn",
    )(tile_expert, n_used, xs, w_gate, w_up, w_down)


def _moe_out_kernel(d0_ref, d1_ref, n0_ref, n1_ref, x_ref, mod_ref, w0_ref, w1_ref, ys_ref, lng_ref, lnb_ref, o_ref,
                    ya_ref, yb_ref, sems, *, tm):
    i = pl.program_id(0)
    slot = i % 2

    def start_rows(i0_ref, i1_ref, dst):
        def issue(r, _):
            pltpu.make_async_copy(ys_ref.at[pl.ds(i0_ref[0, 0, r], 1)], ya_ref.at[dst, pl.ds(r, 1)],
                                  sems.at[dst]).start(priority=0)
            pltpu.make_async_copy(ys_ref.at[pl.ds(i1_ref[0, 0, r], 1)], yb_ref.at[dst, pl.ds(r, 1)],
                                  sems.at[dst]).start(priority=1)
            return 0

        lax.fori_loop(0, tm, issue, 0, unroll=4)

    @pl.when(i == 0)
    def _():
        start_rows(d0_ref, d1_ref, 0)

    @pl.when(i + 1 < pl.num_programs(0))
    def _():
        start_rows(n0_ref, n1_ref, 1 - slot)

    pltpu.make_async_copy(ys_ref.at[pl.ds(0, tm)], ya_ref.at[slot], sems.at[slot]).wait()
    pltpu.make_async_copy(ys_ref.at[pl.ds(0, tm)], yb_ref.at[slot], sems.at[slot]).wait()
    m = mod_ref[0]
    f = w0_ref[:, 0:1] * ya_ref[slot] + w1_ref[:, 0:1] * yb_ref[slot]
    o_ref[...] = _layer_norm(ALPHA * x_ref[...] + m[5:6, :] * f, lng_ref[...], lnb_ref[...])


def _moe_out(rows, n_rows, x1, mods, ys, dest0, dest1, w0, w1, ln_g, ln_b):
    tm, d = rows.tm, x1.shape[1]
    nt = n_rows // tm
    row = pl.BlockSpec((tm, d), lambda i: (i, 0))
    idx = pl.BlockSpec((1, 1, tm), lambda i: (i, 0, 0), memory_space=pltpu.SMEM)
    idx_next = pl.BlockSpec((1, 1, tm), lambda i: (jnp.minimum(i + 1, nt - 1), 0, 0), memory_space=pltpu.SMEM)
    wcol = pl.BlockSpec((tm, 128), lambda i: (i, 0))
    bcast = lambda w: jnp.broadcast_to(w[:, None], (n_rows, 128))
    d0, d1 = dest0.reshape(nt, 1, tm), dest1.reshape(nt, 1, tm)
    return pl.pallas_call(
        functools.partial(_moe_out_kernel, tm=tm),
        grid=(nt,),
        in_specs=[idx, idx, idx_next, idx_next, row, pl.BlockSpec((1, 6, d), lambda i: (rows.group(i), 0, 0)),
                  wcol, wcol, pl.BlockSpec(memory_space=pl.ANY), _full(ln_g.shape), _full(ln_b.shape)],
        out_specs=row,
        out_shape=jax.ShapeDtypeStruct((n_rows, d), F32),
        scratch_shapes=[pltpu.VMEM((2, tm, d), F32), pltpu.VMEM((2, tm, d), F32), pltpu.SemaphoreType.DMA((2,))],
        compiler_params=_params(("arbitrary",)),
        name="moe_out",
    )(d0, d1, d0, d1, x1, mods, bcast(w0), bcast(w1), ys, ln_g, ln_b)


def _moe(rows, n_rows, x1, h2, logits_t, mods, router_bias, w_gate, w_up, w_down, ln_g, ln_b):
    e0, e1, w0, w1 = _route(logits_t, router_bias)
    dest0, dest1, pad_lo, pad_hi, tile_expert, n_used, p_max = _dispatch_plan(e0, e1, n_rows)
    xs = _dispatch(h2, n_rows, dest0, dest1, pad_lo, pad_hi, n_used, p_max, rows.tm)
    ys = _expert_ffn(xs, tile_expert, n_used, w_gate, w_up, w_down)
    return _moe_out(rows, n_rows, x1, mods, ys, dest0, dest1, w0, w1, ln_g, ln_b)


def _odd_in_kernel(x_ref, mod_ref, w_ref, cos_ref, sin_ref, gq_ref, gk_ref, hm_ref, q_ref, k_ref, v_ref, *, qw, kw):
    m = mod_ref[0]
    h = (x_ref[...] * (1.0 + m[1:2, :]) + m[0:1, :]).astype(BF16)
    r = jnp.dot(h, w_ref[...], preferred_element_type=F32)
    q, q_sw = r[:, :qw], r[:, qw:2 * qw]
    o = 2 * qw
    k, k_sw = r[:, o:o + kw], r[:, o + kw:o + 2 * kw]
    v = r[:, o + 2 * kw:o + 3 * kw]
    cos, sin = cos_ref[...], sin_ref[...]
    hm = hm_ref[...]

    def norm_rope(t, t_sw, gains, width):
        hi, lo = _split_bf16(t * t)
        ms = (jnp.dot(hi, hm[:width, :width], preferred_element_type=F32)
              + jnp.dot(lo, hm[:width, :width], preferred_element_type=F32))
        rs = lax.rsqrt(ms + RMS_EPS)
        c = jnp.concatenate([cos] * (width // 128), axis=1)
        s = jnp.concatenate([sin] * (width // 128), axis=1)
        return (t * gains[0:1, :] * c + t_sw * gains[1:2, :] * s) * rs

    q_ref[...] = (norm_rope(q, q_sw, gq_ref[...], qw) * (HEAD_DIM ** -0.5 * LOG2_E)).astype(BF16)
    k_ref[...] = norm_rope(k, k_sw, gk_ref[...], kw).astype(BF16)
    v_ref[...] = v.astype(BF16)


def _odd_in(rows, x_all, mods, w_cat, cos_t, sin_t, gq, gk, head_mean, qw, kw):
    tm, d = rows.tm, x_all.shape[1]
    n = rows.n_all
    row = lambda w: pl.BlockSpec((tm, w), lambda i: (i, 0))
    tab = pl.BlockSpec((tm, 128), lambda i: (rows.rope_block(i), 0))
    return pl.pallas_call(
        functools.partial(_odd_in_kernel, qw=qw, kw=kw),
        grid=(n // tm,),
        in_specs=[row(d), pl.BlockSpec((1, 6, d), lambda i: (rows.group(i), 0, 0)), _full(w_cat.shape), tab, tab,
                  _full(gq.shape), _full(gk.shape), _full(head_mean.shape)],
        out_specs=[row(qw), row(kw), row(kw)],
        out_shape=[jax.ShapeDtypeStruct((n, qw), BF16), jax.ShapeDtypeStruct((n, kw), BF16),
                   jax.ShapeDtypeStruct((n, kw), BF16)],
        compiler_params=_params(("parallel",)),
        name="odd_in",
    )(x_all, mods, w_cat, cos_t, sin_t, gq, gk, head_mean)


def _dense_attn_kernel(q_ref, kc_ref, kl_ref, vc_ref, vl_ref, o_ref, *, grp):
    half = HEAD_DIM
    sub = pl.program_id(3)
    kv_half = ((2 * pl.program_id(1)) // grp) % 2
    qb = q_ref[...]
    sw = jnp.concatenate([qb[:, half:], qb[:, :half]], axis=1)
    lane_half = lax.broadcasted_iota(jnp.int32, qb.shape, 1) // half
    q = jnp.where(lane_half == kv_half, jnp.where(sub == kv_half, qb, sw), jnp.zeros_like(qb))
    s_c = _nt_dot(q, kc_ref[...])
    s_l = _nt_dot(q, kl_ref[...])
    m = jnp.maximum(jnp.max(s_c, axis=-1, keepdims=True), jnp.max(s_l, axis=-1, keepdims=True))
    p_c = jnp.exp2(s_c - m)
    p_l = jnp.exp2(s_l - m)
    l = jnp.sum(p_c, axis=-1, keepdims=True) + jnp.sum(p_l, axis=-1, keepdims=True)
    o = (jnp.dot(p_c.astype(BF16), vc_ref[...], preferred_element_type=F32)
         + jnp.dot(p_l.astype(BF16), vl_ref[...], preferred_element_type=F32)) / l
    o_sw = pltpu.roll(o, half, 1)

    @pl.when(sub == 0)
    def _():
        o_ref[:, :half] = jnp.where(kv_half == 0, o, o_sw)[:, :half].astype(o_ref.dtype)

    @pl.when(sub == 1)
    def _():
        o_ref[:, half:] = jnp.where(kv_half == 1, o, o_sw)[:, half:].astype(o_ref.dtype)


def _dense_attn(q_rows, k_rows, v_rows, bsz, seq, ctx):
    n_lat = bsz * seq
    n_q = q_rows.shape[1] // HEAD_DIM
    grp = n_q // (k_rows.shape[1] // HEAD_DIM)
    tq = DENSE_TQ
    nq = seq // tq
    assert seq % tq == 0 and n_lat % ctx == 0 and grp % 2 == 0
    kv_blk = lambda pair: pair // grp
    qs = pl.BlockSpec((tq, 128), lambda b, p, j, s: (b * nq + j, p))
    kv_ctx = pl.BlockSpec((ctx, 128), lambda b, p, j, s: (n_lat // ctx + b, kv_blk(p)))
    kv_lat = pl.BlockSpec((seq, 128), lambda b, p, j, s: (b, kv_blk(p)))
    return pl.pallas_call(
        functools.partial(_dense_attn_kernel, grp=grp),
        grid=(bsz, n_q // 2, nq, 2),
        in_specs=[qs, kv_ctx, kv_lat, kv_ctx, kv_lat],
        out_specs=qs,
        out_shape=jax.ShapeDtypeStruct((n_lat, q_rows.shape[1]), BF16),
        compiler_params=_params(("parallel", "parallel", "arbitrary", "arbitrary")),
        name="dense_attn",
    )(q_rows, k_rows, k_rows, v_rows, v_rows)


def _rope_tables(seq, tm):
    n_freq = HEAD_DIM // 4
    inv_freq = ROPE_THETA ** (-jnp.arange(n_freq, dtype=F32) / n_freq)
    rows = seq // GRID_W
    r = jnp.repeat(jnp.arange(rows, dtype=F32), GRID_W)
    col = jnp.tile(jnp.arange(GRID_W, dtype=F32), rows)
    ang = jnp.concatenate([r[:, None] * inv_freq, col[:, None] * inv_freq], -1)
    cos, sin = jnp.cos(ang), jnp.sin(ang)
    cos_t = jnp.concatenate([jnp.tile(cos, (1, 4)), jnp.ones((tm, 128), F32)], axis=0)
    sin_t = jnp.concatenate([jnp.tile(sin, (1, 4)), jnp.zeros((tm, 128), F32)], axis=0)
    return cos_t, sin_t


def _swap_halves(w):
    d, n = w.shape
    t = w.reshape(d, n // HEAD_DIM, 2, HEAD_DIM // 2)
    return jnp.stack([-t[:, :, 1], t[:, :, 0]], axis=2).reshape(d, n)


def _to_heads(t_rows, bsz, length, hkv):
    return t_rows.reshape(bsz, length, hkv, HEAD_DIM).transpose(0, 2, 1, 3)


def kernel(x, c, ctx, c_ctx, ada_w, ada_b, ln_g, ln_b, even_w_in, even_w_out, s5_lam_re, s5_lam_im, s5_log_step,
           s5_b_re, s5_b_im, s5_c_re, s5_c_im, s5_d, s5_w_glu, s5_b_glu, win_sink, odd_w_in, odd_w_out,
           odd_q_norm, odd_k_norm, router_w, router_bias, moe_w_gate, moe_w_up, moe_w_down):
    bsz, seq, d = x.shape
    n_ctx = ctx.shape[1]
    assert ada_w.shape[0] == DEPTH == 2
    rows = _Rows(bsz, seq, n_ctx)
    n_lat, n_all = rows.n_lat, rows.n_all
    s5w = s5_d.shape[1]
    win_q = win_sink.shape[1] * HEAD_DIM
    win_kv = (even_w_in.shape[2] - s5w - win_q) // 2
    win_hkv = win_kv // HEAD_DIM
    odd_q = odd_w_out.shape[1]
    odd_kv = (odd_w_in.shape[2] - odd_q) // 2

    cond = jnp.zeros((16, d), F32).at[:bsz].set(c).at[bsz].set(c_ctx)
    mods = _ada(cond, ada_w, ada_b).reshape(DEPTH, 16, 6, d)
    cos_t, sin_t = _rope_tables(seq, rows.tm)
    rw_hi, rw_lo = _split_bf16(router_w.T)
    x_lat, x_ctx = x.reshape(n_lat, d), ctx.reshape(bsz * n_ctx, d)
    lnv = lambda i, j: (ln_g[i, j].reshape(1, d), ln_b[i, j].reshape(1, d))
    moe_w = lambda i: (moe_w_gate[i], moe_w_up[i], moe_w_down[i])

    w = even_w_in[0]
    wu, wq, wk, wv = (w[:, :s5w], w[:, s5w:s5w + win_q], w[:, s5w + win_q:s5w + win_q + win_kv],
                      w[:, s5w + win_q + win_kv:])
    w_cat = jnp.concatenate([wu, wq, _swap_halves(wq), wk, _swap_halves(wk), wv], axis=1).astype(BF16)
    u32, u_tm, q0, k0, v0 = _even_in(rows, x_lat, x_ctx, mods[0], w_cat, cos_t, sin_t, s5w, win_q, win_kv)
    wb, a_b, wc = _s5_weights(s5_lam_re[0], s5_lam_im[0], s5_log_step[0], s5_b_re[0], s5_b_im[0],
                              s5_c_re[0], s5_c_im[0], bsz)
    t_all = seq + n_ctx
    y_tm = _s5(u_tm.reshape(t_all * bsz, s5w), wb, a_b, wc, bsz, n_ctx, seq).reshape(2, t_all, bsz * s5w)
    k_lat, v_lat = _to_heads(k0[:n_lat], bsz, seq, win_hkv), _to_heads(v0[:n_lat], bsz, seq, win_hkv)
    k_ctx, v_ctx = _to_heads(k0[n_lat:], bsz, n_ctx, win_hkv), _to_heads(v0[n_lat:], bsz, n_ctx, win_hkv)
    sink = win_sink[0].astype(F32)
    o_lat = _win_attn(q0[:n_lat], k_ctx, v_ctx, k_lat, v_lat, sink, bsz, seq, tq=256)
    o_ctx = _ctx_attn(q0[n_lat:], k_ctx, v_ctx, sink, bsz)
    o_rows = jnp.concatenate([o_lat, o_ctx], axis=0)
    g0, b0 = lnv(0, 0)
    even_args = (u32, y_tm, s5_d[0].reshape(1, s5w), s5_w_glu[0].astype(BF16), s5_b_glu[0].reshape(1, s5w))
    x1, h2, lg = _mix_out(rows, n_all, (x_lat, x_ctx), mods[0], o_rows, even_w_out[0].astype(BF16), g0, b0, rw_hi, rw_lo,
                          even_args)
    g1, b1 = lnv(0, 1)
    x2 = _moe(rows, n_all, x1, h2, lg, mods[0], router_bias.astype(F32), *moe_w(0), g1, b1)

    w = odd_w_in[0]
    wq, wk, wv = w[:, :odd_q], w[:, odd_q:odd_q + odd_kv], w[:, odd_q + odd_kv:]
    w_cat = jnp.concatenate([wq, _swap_halves(wq), wk, _swap_halves(wk), wv], axis=1).astype(BF16)

    def gains(gv, width):
        gs = jnp.concatenate([gv[HEAD_DIM // 2:], gv[:HEAD_DIM // 2]])
        return jnp.stack([jnp.tile(gv, width // HEAD_DIM), jnp.tile(gs, width // HEAD_DIM)]).astype(F32)

    head_mean = jnp.kron(jnp.eye(odd_q // HEAD_DIM, dtype=F32),
                         jnp.full((HEAD_DIM, HEAD_DIM), 1.0 / HEAD_DIM, F32)).astype(BF16)
    q1, k1, v1 = _odd_in(rows, x2, mods[1], w_cat, cos_t, sin_t, gains(odd_q_norm[0], odd_q),
                         gains(odd_k_norm[0], odd_kv), head_mean, odd_q, odd_kv)
    o1 = _dense_attn(q1, k1, v1, bsz, seq, n_ctx)
    g0, b0 = lnv(1, 0)
    x1, h2, lg = _mix_out(rows, n_lat, x2, mods[1], o1, odd_w_out[0].astype(BF16), g0, b0, rw_hi, rw_lo)
    g1, b1 = lnv(1, 1)
    out = _moe(rows, n_lat, x1, h2, lg, mods[1], router_bias.astype(F32), *moe_w(1), g1, b1)
    return out.reshape(bsz, seq, d)
```

```python
import functools
import math

import jax
import jax.numpy as jnp
from jax import lax
from jax.experimental import pallas as pl
from jax.experimental.pallas import tpu as pltpu

F32 = jnp.float32
BF16 = jnp.bfloat16

HEAD_DIM = 64
GRID_W = 64
ROPE_THETA = 10000.0
S5_GROUP = 16
S5_STATE = 64
WINDOW = 128
N_EXPERTS = 32
N_GROUPS = 8
EXPERTS_PER_GROUP = N_EXPERTS // N_GROUPS
TOP_K = 2
DEPTH = 2
ALPHA = (2 * DEPTH) ** 0.25
LN_EPS = 1e-5
LOG2_E = math.log2(math.e)
RMS_EPS = 1e-6
NEG_INF = -1e30

ROW_TILE = 256
S5_CHUNK = 64
S5_LANE_GROUP = 512
EXPERT_TILE = 256
RANK_BLOCK = 512
DENSE_TQ = 256
DENSE_KEY_CHUNK = 512
VMEM_LIMIT = 48 * 1024 * 1024


def _params(sem):
    return pltpu.CompilerParams(dimension_semantics=sem, vmem_limit_bytes=VMEM_LIMIT)


def _full(shape):
    n = len(shape)
    return pl.BlockSpec(shape, lambda *_: (0,) * n)


def _ada_kernel(c_ref, w_ref, b_ref, o_ref):
    c = c_ref[...]
    s = c * jax.nn.sigmoid(c)
    o_ref[...] = jnp.dot(s, w_ref[0], preferred_element_type=F32, precision=lax.Precision.HIGHEST) + b_ref[0]


def _ada(cond, ada_w, ada_b):
    g, d = cond.shape
    depth, _, n = ada_w.shape
    bn = 1024
    return pl.pallas_call(
        _ada_kernel,
        grid=(depth, n // bn),
        in_specs=[pl.BlockSpec((g, d), lambda i, j: (0, 0)),
                  pl.BlockSpec((1, d, bn), lambda i, j: (i, 0, j)),
                  pl.BlockSpec((1, 1, bn), lambda i, j: (i, 0, j))],
        out_specs=pl.BlockSpec((None, g, bn), lambda i, j: (i, 0, j)),
        out_shape=jax.ShapeDtypeStruct((depth, g, n), F32),
        compiler_params=_params(("arbitrary", "arbitrary")),
        name="ada",
    )(cond, ada_w, ada_b.reshape(depth, 1, n))


class _Rows:
    def __init__(self, bsz, seq, ctx, tm=ROW_TILE):
        assert seq % tm == 0 and ctx % tm == 0
        self.bsz, self.seq, self.ctx, self.tm = bsz, seq, ctx, tm
        self.tpb = seq // tm
        self.cpb = ctx // tm
        self.nl = bsz * self.tpb
        self.nc = bsz * self.cpb
        self.n_lat = bsz * seq
        self.n_all = bsz * (seq + ctx)

    def two_source(self, width):
        return (pl.BlockSpec((self.tm, width), lambda i: (jnp.minimum(i, self.nl - 1), 0)),
                pl.BlockSpec((self.tm, width), lambda i: (jnp.maximum(i - self.nl, 0), 0)))

    def group(self, i):
        return jnp.where(i < self.nl, i // self.tpb, self.bsz)

    def rope_block(self, i):
        return jnp.where(i < self.nl, i % self.tpb, self.tpb)

    def time_major(self, i):
        lat = i < self.nl
        j = i - self.nl
        return (jnp.where(lat, self.cpb + i % self.tpb, j % self.cpb),
                jnp.where(lat, i // self.tpb, j // self.cpb))


def _layer_norm(r, g, b):
    mu = jnp.mean(r, axis=-1, keepdims=True)
    rc = r - mu
    var = jnp.mean(rc * rc, axis=-1, keepdims=True)
    return rc * lax.rsqrt(var + LN_EPS) * g + b


def _even_in_kernel(xl_ref, xc_ref, mod_ref, w_ref, cos_ref, sin_ref, u32_ref, utm_ref, q_ref, k_ref, v_ref, *,
                    s5w, qw, kw, n_lat_tiles):
    m = mod_ref[0]
    x = jnp.where(pl.program_id(0) < n_lat_tiles, xl_ref[...], xc_ref[...])
    h = (x * (1.0 + m[1:2, :]) + m[0:1, :]).astype(BF16)
    r = jnp.dot(h, w_ref[...], preferred_element_type=F32)
    u = r[:, :s5w]
    u32_ref[...] = u
    utm_ref[...] = u.astype(BF16)
    o = s5w
    q, q_sw = r[:, o:o + qw], r[:, o + qw:o + 2 * qw]
    o += 2 * qw
    k, k_sw = r[:, o:o + kw], r[:, o + kw:o + 2 * kw]
    o += 2 * kw
    v = r[:, o:o + kw]
    cos, sin = cos_ref[...], sin_ref[...]
    cq = jnp.concatenate([cos] * (qw // 128), axis=1)
    sq = jnp.concatenate([sin] * (qw // 128), axis=1)
    q_ref[...] = ((q * cq + q_sw * sq) * (HEAD_DIM ** -0.5)).astype(BF16)
    k_ref[...] = (k * cos + k_sw * sin).astype(BF16)
    v_ref[...] = v.astype(BF16)


def _even_in(rows, x_lat, x_ctx, mods, w_cat, cos_t, sin_t, s5w, qw, kw):
    tm, d = rows.tm, x_lat.shape[1]
    n = rows.n_all
    t_all = rows.seq + rows.ctx
    kern = functools.partial(_even_in_kernel, s5w=s5w, qw=qw, kw=kw, n_lat_tiles=rows.nl)
    row = lambda w: pl.BlockSpec((tm, w), lambda i: (i, 0))
    return pl.pallas_call(
        kern,
        grid=(n // tm,),
        in_specs=list(rows.two_source(d)) + [
                  pl.BlockSpec((1, 6, d), lambda i: (rows.group(i), 0, 0)),
                  _full(w_cat.shape),
                  pl.BlockSpec((tm, 128), lambda i: (rows.rope_block(i), 0)),
                  pl.BlockSpec((tm, 128), lambda i: (rows.rope_block(i), 0))],
        out_specs=[row(s5w),
                   pl.BlockSpec((tm, s5w), lambda i: rows.time_major(i)),
                   row(qw), row(kw), row(kw)],
        out_shape=[jax.ShapeDtypeStruct((n, s5w), F32),
                   jax.ShapeDtypeStruct((t_all, rows.bsz * s5w), BF16),
                   jax.ShapeDtypeStruct((n, qw), BF16),
                   jax.ShapeDtypeStruct((n, kw), BF16),
                   jax.ShapeDtypeStruct((n, kw), BF16)],
        compiler_params=_params(("parallel",)),
        name="even_in",
    )(x_lat, x_ctx, mods, w_cat, cos_t, sin_t)


def _s5_kernel(u_ref, wb_ref, a_ref, wc_ref, y_ref, bu_ref, h_ref, *, tc, nb, width, n_state):
    lg = S5_LANE_GROUP
    n_lg = n_state // lg
    kch = width // n_lg
    d = pl.program_id(0)

    @pl.when(pl.program_id(1) == 0)
    def _():
        h_ref[...] = jnp.zeros_like(h_ref)

    u = u_ref[...]
    for g in range(n_lg):
        r = jnp.dot(u[:, kch * g:kch * (g + 1)], wb_ref[0, g], preferred_element_type=F32)
        bu_ref[:, lg * g:lg * (g + 1)] = r[:, :lg]
        bu_ref[:, n_state + lg * g:n_state + lg * (g + 1)] = r[:, lg:]

    rev = d == 1
    for g in range(n_lg):
        re = slice(lg * g, lg * (g + 1))
        im = slice(n_state + lg * g, n_state + lg * (g + 1))
        a_re = a_ref[0, :, re]
        a_im = a_ref[0, :, im]

        def body(i, carry, re=re, im=im, a_re=a_re, a_im=a_im):
            hr, hi = carry
            t = jnp.where(rev, tc - 1 - i, i)
            row = pl.multiple_of(t * nb, nb)
            br = bu_ref[pl.ds(row, nb), re]
            bi = bu_ref[pl.ds(row, nb), im]
            nr = a_re * hr - a_im * hi + br
            ni = a_re * hi + a_im * hr + bi
            bu_ref[pl.ds(row, nb), re] = nr
            bu_ref[pl.ds(row, nb), im] = ni
            return nr, ni

        hr, hi = lax.fori_loop(0, tc, body, (h_ref[:, re], h_ref[:, im]), unroll=4)
        h_ref[:, re] = hr
        h_ref[:, im] = hi

    for g in range(n_lg):
        hc = jnp.concatenate([bu_ref[:, lg * g:lg * (g + 1)],
                              bu_ref[:, n_state + lg * g:n_state + lg * (g + 1)]], axis=1).astype(BF16)
        y_ref[0, :, kch * g:kch * (g + 1)] = jnp.dot(hc, wc_ref[0, g], preferred_element_type=F32)


def _s5(u_tm, wb, a_b, wc, nb, ctx, seq):
    tc = S5_CHUNK
    t_all = ctx + seq
    width = u_tm.shape[1]
    n_state = a_b.shape[2] // 2
    assert ctx % tc == 0 and seq % tc == 0 and nb == 8
    n_ctx, n_chunks = ctx // tc, t_all // tc

    def chunk(d, s):
        back = jnp.where(s < n_ctx, n_ctx - 1 - s, n_chunks - 1 - (s - n_ctx))
        return jnp.where(d == 0, s, back)

    kern = functools.partial(_s5_kernel, tc=tc, nb=nb, width=width, n_state=n_state)
    return pl.pallas_call(
        kern,
        grid=(2, n_chunks),
        in_specs=[pl.BlockSpec((tc * nb, width), lambda d, s: (chunk(d, s), 0)),
                  pl.BlockSpec((1,) + wb.shape[1:], lambda d, s: (d, 0, 0, 0)),
                  pl.BlockSpec((1,) + a_b.shape[1:], lambda d, s: (d, 0, 0)),
                  pl.BlockSpec((1,) + wc.shape[1:], lambda d, s: (d, 0, 0, 0))],
        out_specs=pl.BlockSpec((1, tc * nb, width), lambda d, s: (d, chunk(d, s), 0)),
        out_shape=jax.ShapeDtypeStruct((2, t_all * nb, width), F32),
        scratch_shapes=[pltpu.VMEM((tc * nb, 2 * n_state), F32), pltpu.VMEM((nb, 2 * n_state), F32)],
        compiler_params=_params(("arbitrary", "arbitrary")),
        name="s5_scan",
    )(u_tm, wb, a_b, wc)


def _s5_weights(lam_re, lam_im, log_step, b_re, b_im, c_re, c_im, nb):
    n_dir, n_grp, n_p = lam_re.shape
    ch = b_re.shape[-1]
    lr, li = lam_re.astype(F32), lam_im.astype(F32)
    dt = jnp.exp(log_step.astype(F32))[..., None]
    mag = jnp.exp(lr * dt)
    ab_re, ab_im = mag * jnp.cos(li * dt), mag * jnp.sin(li * dt)
    den = lr * lr + li * li
    num_re, num_im = ab_re - 1.0, ab_im
    f_re = (num_re * lr + num_im * li) / den
    f_im = (num_im * lr - num_re * li) / den
    br, bi = b_re.astype(F32), b_im.astype(F32)
    bb_re = f_re[..., None] * br - f_im[..., None] * bi
    bb_im = f_re[..., None] * bi + f_im[..., None] * br
    n_state = n_grp * n_p
    n_lg = n_state // S5_LANE_GROUP
    gpl = n_grp // n_lg
    eye = jnp.eye(gpl, dtype=F32)

    def pack_in(bb):
        t = bb.reshape(n_dir, n_lg, gpl, n_p, ch)
        m = jnp.einsum('dlgpc,gh->dlgchp', t, eye)
        return m.reshape(n_dir, n_lg, gpl * ch, gpl * n_p)

    def pack_out(c):
        t = c.astype(F32).reshape(n_dir, n_lg, gpl, ch, n_p)
        m = jnp.einsum('dlgcp,gh->dlgphc', t, eye)
        return m.reshape(n_dir, n_lg, gpl * n_p, gpl * ch)

    wb = jnp.concatenate([pack_in(bb_re), pack_in(bb_im)], axis=-1).astype(BF16)
    wc = jnp.concatenate([pack_out(c_re), -pack_out(c_im)], axis=-2).astype(BF16)
    a = jnp.concatenate([ab_re.reshape(n_dir, n_state), ab_im.reshape(n_dir, n_state)], axis=-1)
    a_b = jnp.broadcast_to(a[:, None, :], (n_dir, nb, 2 * n_state))
    return wb, a_b, wc


def _nt_dot(a, b):
    return lax.dot_general(a, b, (((1,), (1,)), ((), ())), preferred_element_type=F32)


def _win_attn_kernel(sink_ref, q_ref, kc_ref, vc_ref, kl_ref, vl_ref, o_ref, *, tq, seq, grp, n_lat_tiles):
    i = pl.program_id(0)
    n_heads = q_ref.shape[1] // HEAD_DIM
    kc, vc = kc_ref[...], vc_ref[...]
    zeros = jnp.zeros((tq, HEAD_DIM), q_ref.dtype)

    def attend(local):
        if local:
            band = tq + 2 * WINDOW
            j = i % (seq // tq)
            start = pl.multiple_of(jnp.clip(j * tq - WINDOW, 0, seq - band), 8)
            kb = kl_ref[pl.ds(start, band), :]
            vb = vl_ref[pl.ds(start, band), :]
            qpos = j * tq + lax.broadcasted_iota(jnp.int32, (tq, band), 0)
            kpos = start + lax.broadcasted_iota(jnp.int32, (tq, band), 1)
            valid = jnp.abs(qpos - kpos) <= WINDOW
        outs = []
        for hh in range(n_heads):
            kv_half = (hh // grp) % 2
            qh = q_ref[:, hh * HEAD_DIM:(hh + 1) * HEAD_DIM]
            q = jnp.concatenate([qh, zeros] if kv_half == 0 else [zeros, qh], axis=1)
            sink = sink_ref[hh]
            s_c = _nt_dot(q, kc)
            m = jnp.maximum(jnp.max(s_c, axis=-1, keepdims=True), sink)
            if local:
                s_l = jnp.where(valid, _nt_dot(q, kb), NEG_INF)
                m = jnp.maximum(m, jnp.max(s_l, axis=-1, keepdims=True))
            p_c = jnp.exp(s_c - m)
            l = jnp.sum(p_c, axis=-1, keepdims=True) + jnp.exp(sink - m)
            acc = jnp.dot(p_c.astype(BF16), vc, preferred_element_type=F32)
            if local:
                p_l = jnp.exp(s_l - m)
                l = l + jnp.sum(p_l, axis=-1, keepdims=True)
                acc = acc + jnp.dot(p_l.astype(BF16), vb, preferred_element_type=F32)
            outs.append((acc / l)[:, kv_half * HEAD_DIM:(kv_half + 1) * HEAD_DIM])
        o_ref[...] = jnp.concatenate(outs, axis=1).astype(o_ref.dtype)

    @pl.when(i < n_lat_tiles)
    def _():
        attend(True)

    @pl.when(i >= n_lat_tiles)
    def _():
        attend(False)


def _win_attn(rows, q_rows, k_rows, v_rows, sink):
    tq, seq, ctx, bsz = rows.tm, rows.seq, rows.ctx, rows.bsz
    assert k_rows.shape[1] == 2 * HEAD_DIM and ctx == tq and seq >= tq + 2 * WINDOW
    grp = q_rows.shape[1] // k_rows.shape[1]
    nq = seq // tq
    n_lat_tiles = bsz * nq
    sample = lambda i: jnp.where(i < n_lat_tiles, i // nq, i - n_lat_tiles)
    kv_ctx = pl.BlockSpec((ctx, 128), lambda i: (rows.n_lat // ctx + sample(i), 0))
    kv_lat = pl.BlockSpec((seq, 128), lambda i: (sample(i), 0))
    qs = pl.BlockSpec((tq, q_rows.shape[1]), lambda i: (i, 0))
    return pl.pallas_call(
        functools.partial(_win_attn_kernel, tq=tq, seq=seq, grp=grp, n_lat_tiles=n_lat_tiles),
        grid=(n_lat_tiles + bsz,),
        in_specs=[pl.BlockSpec(memory_space=pltpu.SMEM), qs, kv_ctx, kv_ctx, kv_lat, kv_lat],
        out_specs=qs,
        out_shape=jax.ShapeDtypeStruct(q_rows.shape, BF16),
        compiler_params=_params(("parallel",)),
        name="win_attn",
    )(sink, q_rows, k_rows, v_rows, k_rows, v_rows)


def _gelu_tanh(y):
    return 0.5 * y * (1.0 + jnp.tanh(math.sqrt(2.0 / math.pi) * (y + 0.044715 * (y * y * y))))


def _split_bf16(v):
    hi = v.astype(BF16)
    return hi, (v - hi.astype(F32)).astype(BF16)


def _mix_out_kernel(*refs, even, n_lat_tiles):
    if even:
        (xl_ref, xc_ref, mod_ref, o_ref, wout_ref, lng_ref, lnb_ref, rwh_ref, rwl_ref,
         u_ref, yf_ref, yb_ref, dsk_ref, wglu_ref, bglu_ref, x1_ref, h2_ref, lg_ref) = refs
        x = jnp.where(pl.program_id(0) < n_lat_tiles, xl_ref[...], xc_ref[...])
        y = u_ref[...] * dsk_ref[...] + yf_ref[0] + yb_ref[0]
        z = _gelu_tanh(y)
        gate = jax.nn.sigmoid(jnp.dot(z.astype(BF16), wglu_ref[...], preferred_element_type=F32) + bglu_ref[...])
        mix = jnp.concatenate([(z * gate).astype(BF16), o_ref[...]], axis=1)
    else:
        (x_ref, mod_ref, o_ref, wout_ref, lng_ref, lnb_ref, rwh_ref, rwl_ref, x1_ref, h2_ref, lg_ref) = refs
        x = x_ref[...]
        mix = o_ref[...]
    m = mod_ref[0]
    ol = jnp.dot(mix, wout_ref[...], preferred_element_type=F32)
    x1 = _layer_norm(ALPHA * x + m[2:3, :] * ol, lng_ref[...], lnb_ref[...])
    x1_ref[...] = x1
    h2 = x1 * (1.0 + m[4:5, :]) + m[3:4, :]
    h2_ref[...] = h2
    hh, hl = _split_bf16(h2)
    rwh, rwl = rwh_ref[...], rwl_ref[...]
    lg_ref[...] = _nt_dot(rwh, hh) + (_nt_dot(rwh, hl) + _nt_dot(rwl, hh))


def _mix_out(rows, n_rows, x_src, mods, o_rows, w_out, ln_g, ln_b, rw_hi, rw_lo, even_args=None):
    even = even_args is not None
    tm, d = rows.tm, w_out.shape[1]
    row = lambda w: pl.BlockSpec((tm, w), lambda i: (i, 0))
    x_ins, x_specs = (list(x_src), list(rows.two_source(d))) if even else ([x_src], [row(d)])
    ins = x_ins + [mods, o_rows, w_out, ln_g, ln_b, rw_hi, rw_lo]
    specs = x_specs + [pl.BlockSpec((1, 6, d), lambda i: (rows.group(i), 0, 0)), row(o_rows.shape[1]),
                       _full(w_out.shape), _full(ln_g.shape), _full(ln_b.shape), _full(rw_hi.shape),
                       _full(rw_lo.shape)]
    if even:
        u32, y_tm, d_skip, w_glu, b_glu = even_args
        s5w = u32.shape[1]
        tmaj = lambda dr: pl.BlockSpec((1, tm, s5w), lambda i: (dr,) + tuple(rows.time_major(i)))
        ins += [u32, y_tm, y_tm, d_skip, w_glu, b_glu]
        specs += [row(s5w), tmaj(0), tmaj(1), _full(d_skip.shape), _full(w_glu.shape), _full(b_glu.shape)]
    n_e = rw_hi.shape[0]
    return pl.pallas_call(
        functools.partial(_mix_out_kernel, even=even, n_lat_tiles=rows.nl),
        grid=(n_rows // tm,),
        in_specs=specs,
        out_specs=[row(d), row(d), pl.BlockSpec((n_e, tm), lambda i: (0, i))],
        out_shape=[jax.ShapeDtypeStruct((n_rows, d), F32), jax.ShapeDtypeStruct((n_rows, d), F32),
                   jax.ShapeDtypeStruct((n_e, n_rows), F32)],
        compiler_params=_params(("parallel",)),
        name="mix_out_even" if even else "mix_out_odd",
    )(*ins)


def _route_kernel(lg_ref, bias_ref, e0_ref, e1_ref, w0_ref, w1_ref):
    epg = EXPERTS_PER_GROUP
    sc = [jax.nn.sigmoid(lg_ref[e]) for e in range(N_EXPERTS)]
    sel = [sc[e] + bias_ref[e] for e in range(N_EXPERTS)]
    best, gidx = None, None
    for g in range(N_GROUPS):
        a = sel[g * epg:(g + 1) * epg]
        pair = None
        for i in range(epg):
            for j in range(i + 1, epg):
                s2 = a[i] + a[j]
                pair = s2 if pair is None else jnp.maximum(pair, s2)
        if best is None:
            best, gidx = pair, jnp.zeros(pair.shape, jnp.int32)
        else:
            take = pair > best
            best = jnp.where(take, pair, best)
            gidx = jnp.where(take, g, gidx)
    sg = [sel[i] for i in range(epg)]
    cg = [sc[i] for i in range(epg)]
    for g in range(1, N_GROUPS):
        hit = gidx == g
        sg = [jnp.where(hit, sel[g * epg + i], sg[i]) for i in range(epg)]
        cg = [jnp.where(hit, sc[g * epg + i], cg[i]) for i in range(epg)]

    def first_max(vals, skip=None):
        bv, bi, bs = None, None, None
        for i in range(epg):
            v = vals[i] if skip is None else jnp.where(skip == i, -jnp.inf, vals[i])
            if bv is None:
                bv, bi, bs = v, jnp.zeros(v.shape, jnp.int32), cg[0]
            else:
                take = v > bv
                bv = jnp.where(take, v, bv)
                bi = jnp.where(take, i, bi)
                bs = jnp.where(take, cg[i], bs)
        return bi, bs

    i0, s0 = first_max(sg)
    i1, s1 = first_max(sg, skip=i0)
    tot = s0 + s1
    e0_ref[...] = gidx * epg + i0
    e1_ref[...] = gidx * epg + i1
    w0_ref[...] = s0 / tot
    w1_ref[...] = s1 / tot


def _route(logits_t, bias):
    n_e, n = logits_t.shape
    r = n // 128
    rb = 8
    assert n % (128 * rb) == 0
    lg3 = logits_t.reshape(n_e, r, 128)
    blk = pl.BlockSpec((rb, 128), lambda i: (i, 0))
    outs = pl.pallas_call(
        _route_kernel,
        grid=(r // rb,),
        in_specs=[pl.BlockSpec((n_e, rb, 128), lambda i: (0, i, 0)), pl.BlockSpec(memory_space=pltpu.SMEM)],
        out_specs=[blk, blk, blk, blk],
        out_shape=[jax.ShapeDtypeStruct((r, 128), jnp.int32)] * 2 + [jax.ShapeDtypeStruct((r, 128), F32)] * 2,
        compiler_params=_params(("parallel",)),
        name="route",
    )(lg3, bias)
    return [o.reshape(n) for o in outs]


def _rank_kernel(e_ref, tri_ref, rank_ref, cnt_ref, carry_ref, *, bw):
    @pl.when(pl.program_id(0) == 0)
    def _():
        carry_ref[...] = jnp.zeros_like(carry_ref)

    hit = lax.broadcasted_iota(jnp.int32, (N_EXPERTS, bw), 0) == e_ref[...]
    onehot = jnp.where(hit, 1.0, 0.0).astype(BF16)
    run = jnp.dot(onehot, tri_ref[...], preferred_element_type=F32) + carry_ref[...]
    rank_ref[...] = (jnp.sum(jnp.where(hit, run, 0.0), axis=0, keepdims=True) - 1.0).astype(jnp.int32)
    carry_ref[...] = run[:, bw - 1:bw]
    cnt_ref[...] = run[:, bw - 1:bw]


def _pair_ranks(e_pairs):
    n_pair = e_pairs.shape[0]
    bw = RANK_BLOCK
    assert n_pair % bw == 0
    tri = (jnp.arange(bw)[:, None] <= jnp.arange(bw)[None, :]).astype(BF16)
    rank, cnt = pl.pallas_call(
        functools.partial(_rank_kernel, bw=bw),
        grid=(n_pair // bw,),
        in_specs=[pl.BlockSpec((1, bw), lambda i: (0, i)), _full((bw, bw))],
        out_specs=[pl.BlockSpec((1, bw), lambda i: (0, i)), _full((N_EXPERTS, 1))],
        out_shape=[jax.ShapeDtypeStruct((1, n_pair), jnp.int32), jax.ShapeDtypeStruct((N_EXPERTS, 1), F32)],
        scratch_shapes=[pltpu.VMEM((N_EXPERTS, 1), F32)],
        compiler_params=_params(("arbitrary",)),
        name="pair_ranks",
    )(e_pairs.reshape(1, n_pair), tri)
    return rank.reshape(n_pair), cnt.reshape(N_EXPERTS).astype(jnp.int32)


def _dispatch_plan(e0, e1, n_tok):
    te = EXPERT_TILE
    p_max = -(-(TOP_K * n_tok + N_EXPERTS * (te - 1)) // te) * te
    e_pairs = jnp.concatenate([e0, e1])
    rank, counts = _pair_ranks(e_pairs)
    padded = (counts + te - 1) // te * te
    pad_end = jnp.cumsum(padded)
    pad_start = pad_end - padded
    is_e = e_pairs[:, None] == jnp.arange(N_EXPERTS, dtype=jnp.int32)[None, :]
    dest = rank + jnp.sum(jnp.where(is_e, pad_start[None, :], 0), axis=1)
    tiles = jnp.arange(p_max // te, dtype=jnp.int32) * te
    tile_expert = jnp.minimum(jnp.sum(tiles[:, None] >= pad_end[None, :], axis=1), N_EXPERTS - 1).astype(jnp.int32)
    n_used = (pad_end[-1] // te).astype(jnp.int32).reshape(1)
    return dest[:n_tok], dest[n_tok:], pad_start + counts, pad_end, tile_expert, n_used, p_max


def _dispatch_kernel(lo_ref, hi_ref, nu_ref, d0_ref, d1_ref, h_ref, xs_ref, stage_ref, zero_ref, sems, zsem, *, tm):
    i = pl.program_id(0)
    slot = i % 2
    stage_ref[slot] = h_ref[...]

    def issue(r, _):
        src = stage_ref.at[slot, pl.ds(r, 1)]
        pltpu.make_async_copy(src, xs_ref.at[pl.ds(d0_ref[0, 0, r], 1)], sems.at[slot]).start(priority=0)
        pltpu.make_async_copy(src, xs_ref.at[pl.ds(d1_ref[0, 0, r], 1)], sems.at[slot]).start(priority=1)
        return 0

    lax.fori_loop(0, tm, issue, 0, unroll=4)

    def wait_rows(s):
        pltpu.make_async_copy(xs_ref.at[pl.ds(0, 2 * tm)], xs_ref.at[pl.ds(0, 2 * tm)], sems.at[s]).wait()

    @pl.when(i > 0)
    def _():
        wait_rows(1 - slot)

    @pl.when(pl.program_id(0) == pl.num_programs(0) - 1)
    def _():
        zero_ref[...] = jnp.zeros_like(zero_ref)

        def pad_copy(r):
            return pltpu.make_async_copy(zero_ref.at[pl.ds(0, 1)], xs_ref.at[pl.ds(r, 1)], zsem)

        def per_expert(e, _):
            lax.fori_loop(lo_ref[e], hi_ref[e], lambda r, c: (pad_copy(r).start(), c)[1], 0)
            lax.fori_loop(lo_ref[e], hi_ref[e], lambda r, c: (pad_copy(r).wait(), c)[1], 0)
            return 0

        lax.fori_loop(0, N_EXPERTS, per_expert, 0)

        te = zero_ref.shape[0]

        def tail_copy(t):
            return pltpu.make_async_copy(zero_ref, xs_ref.at[pl.ds(pl.multiple_of(t * te, te), te)], zsem)

        n_tiles = xs_ref.shape[0] // te
        lax.fori_loop(nu_ref[0], n_tiles, lambda t, c: (tail_copy(t).start(), c)[1], 0)
        lax.fori_loop(nu_ref[0], n_tiles, lambda t, c: (tail_copy(t).wait(), c)[1], 0)
        wait_rows(slot)


def _dispatch(h2, n_rows, dest0, dest1, pad_lo, pad_hi, n_used, p_max, tm):
    d = h2.shape[1]
    nt = n_rows // tm
    idx = pl.BlockSpec((1, 1, tm), lambda i, *_: (i, 0, 0), memory_space=pltpu.SMEM)
    grid_spec = pltpu.PrefetchScalarGridSpec(
        num_scalar_prefetch=3,
        grid=(nt,),
        in_specs=[idx, idx, pl.BlockSpec((tm, d), lambda i, *_: (i, 0))],
        out_specs=pl.BlockSpec(memory_space=pl.ANY),
        scratch_shapes=[pltpu.VMEM((2, tm, d), h2.dtype), pltpu.VMEM((EXPERT_TILE, d), h2.dtype),
                        pltpu.SemaphoreType.DMA((2,)), pltpu.SemaphoreType.DMA(())],
    )
    return pl.pallas_call(
        functools.partial(_dispatch_kernel, tm=tm),
        grid_spec=grid_spec,
        out_shape=jax.ShapeDtypeStruct((p_max, d), h2.dtype),
        compiler_params=_params(("arbitrary",)),
        name="dispatch",
    )(pad_lo, pad_hi, n_used, dest0.reshape(nt, 1, tm), dest1.reshape(nt, 1, tm), h2)


def _expert_kernel(te_ref, nu_ref, x_ref, wg_ref, wu_ref, wd_ref, y_ref, wgu_bf, wd_bf, *, ff):
    i = pl.program_id(0)

    @pl.when((i == 0) | (te_ref[i] != te_ref[jnp.maximum(i - 1, 0)]))
    def _():
        wgu_bf[:, :ff] = wg_ref[0, 0].astype(BF16)
        wgu_bf[:, ff:] = wu_ref[0, 0].astype(BF16)
        wd_bf[...] = wd_ref[0, 0].astype(BF16)

    @pl.when(i < nu_ref[0])
    def _():
        x = x_ref[...].astype(BF16)
        gu = jnp.dot(x, wgu_bf[...], preferred_element_type=F32)
        g, u = gu[:, :ff], gu[:, ff:]
        hid = (g * jax.nn.sigmoid(g) * u).astype(BF16)
        y_ref[...] = jnp.dot(hid, wd_bf[...], preferred_element_type=F32)

    @pl.when(i >= nu_ref[0])
    def _():
        y_ref[...] = jnp.zeros_like(y_ref)


def _expert_ffn(xs, tile_expert, n_used, layer, w_gate, w_up, w_down):
    p, d = xs.shape
    te = EXPERT_TILE
    ff = w_down.shape[2]
    by_expert = lambda shape: pl.BlockSpec((1, 1) + shape, lambda i, t, n: (layer, t[i], 0, 0))
    grid_spec = pltpu.PrefetchScalarGridSpec(
        num_scalar_prefetch=2,
        grid=(p // te,),
        in_specs=[pl.BlockSpec((te, d), lambda i, t, n: (jnp.minimum(i, n[0] - 1), 0)),
                  by_expert((d, ff)), by_expert((d, ff)), by_expert((ff, d))],
        out_specs=pl.BlockSpec((te, d), lambda i, t, n: (i, 0)),
        scratch_shapes=[pltpu.VMEM((d, 2 * ff), BF16), pltpu.VMEM((ff, d), BF16)],
    )
    return pl.pallas_call(
        functools.partial(_expert_kernel, ff=ff),
        grid_spec=grid_spec,
        out_shape=jax.ShapeDtypeStruct((p, d), F32),
        compiler_params=_params(("arbitrary",)),
        name="expert_ffn",
    )(tile_expert, n_used, xs, w_gate, w_up, w_down)


def _moe_out_kernel(d0_ref, d1_ref, n0_ref, n1_ref, x_ref, mod_ref, w0_ref, w1_ref, ys_ref, lng_ref, lnb_ref, o_ref,
                    ya_ref, yb_ref, sems, *, tm):
    i = pl.program_id(0)
    slot = i % 2

    def start_rows(i0_ref, i1_ref, dst):
        def issue(r, _):
            pltpu.make_async_copy(ys_ref.at[pl.ds(i0_ref[0, 0, r], 1)], ya_ref.at[dst, pl.ds(r, 1)],
                                  sems.at[dst]).start(priority=0)
            pltpu.make_async_copy(ys_ref.at[pl.ds(i1_ref[0, 0, r], 1)], yb_ref.at[dst, pl.ds(r, 1)],
                                  sems.at[dst]).start(priority=1)
            return 0

        lax.fori_loop(0, tm, issue, 0, unroll=4)

    @pl.when(i == 0)
    def _():
        start_rows(d0_ref, d1_ref, 0)

    @pl.when(i + 1 < pl.num_programs(0))
    def _():
        start_rows(n0_ref, n1_ref, 1 - slot)

    pltpu.make_async_copy(ys_ref.at[pl.ds(0, tm)], ya_ref.at[slot], sems.at[slot]).wait()
    pltpu.make_async_copy(ys_ref.at[pl.ds(0, tm)], yb_ref.at[slot], sems.at[slot]).wait()
    m = mod_ref[0]
    f = w0_ref[:, 0:1] * ya_ref[slot] + w1_ref[:, 0:1] * yb_ref[slot]
    o_ref[...] = _layer_norm(ALPHA * x_ref[...] + m[5:6, :] * f, lng_ref[...], lnb_ref[...])


def _moe_out(rows, n_rows, x1, mods, ys, dest0, dest1, w0, w1, ln_g, ln_b):
    tm, d = rows.tm, x1.shape[1]
    nt = n_rows // tm
    row = pl.BlockSpec((tm, d), lambda i: (i, 0))
    idx = pl.BlockSpec((1, 1, tm), lambda i: (i, 0, 0), memory_space=pltpu.SMEM)
    idx_next = pl.BlockSpec((1, 1, tm), lambda i: (jnp.minimum(i + 1, nt - 1), 0, 0), memory_space=pltpu.SMEM)
    wcol = pl.BlockSpec((tm, 128), lambda i: (i, 0))
    bcast = lambda w: jnp.broadcast_to(w[:, None], (n_rows, 128))
    d0, d1 = dest0.reshape(nt, 1, tm), dest1.reshape(nt, 1, tm)
    return pl.pallas_call(
        functools.partial(_moe_out_kernel, tm=tm),
        grid=(nt,),
        in_specs=[idx, idx, idx_next, idx_next, row, pl.BlockSpec((1, 6, d), lambda i: (rows.group(i), 0, 0)),
                  wcol, wcol, pl.BlockSpec(memory_space=pl.ANY), _full(ln_g.shape), _full(ln_b.shape)],
        out_specs=row,
        out_shape=jax.ShapeDtypeStruct((n_rows, d), F32),
        scratch_shapes=[pltpu.VMEM((2, tm, d), F32), pltpu.VMEM((2, tm, d), F32), pltpu.SemaphoreType.DMA((2,))],
        compiler_params=_params(("arbitrary",)),
        name="moe_out",
    )(d0, d1, d0, d1, x1, mods, bcast(w0), bcast(w1), ys, ln_g, ln_b)


def _moe(rows, n_rows, x1, h2, logits_t, mods, router_bias, layer, w_gate, w_up, w_down, ln_g, ln_b):
    e0, e1, w0, w1 = _route(logits_t, router_bias)
    dest0, dest1, pad_lo, pad_hi, tile_expert, n_used, p_max = _dispatch_plan(e0, e1, n_rows)
    xs = _dispatch(h2, n_rows, dest0, dest1, pad_lo, pad_hi, n_used, p_max, rows.tm)
    ys = _expert_ffn(xs, tile_expert, n_used, layer, w_gate, w_up, w_down)
    return _moe_out(rows, n_rows, x1, mods, ys, dest0, dest1, w0, w1, ln_g, ln_b)


def _odd_in_kernel(x_ref, mod_ref, w_ref, cos_ref, sin_ref, gq_ref, gk_ref, hm_ref, q_ref, k_ref, v_ref, *, qw, kw):
    m = mod_ref[0]
    h = (x_ref[...] * (1.0 + m[1:2, :]) + m[0:1, :]).astype(BF16)
    r = jnp.dot(h, w_ref[...], preferred_element_type=F32)
    q, q_sw = r[:, :qw], r[:, qw:2 * qw]
    o = 2 * qw
    k, k_sw = r[:, o:o + kw], r[:, o + kw:o + 2 * kw]
    v = r[:, o + 2 * kw:o + 3 * kw]
    cos, sin = cos_ref[...], sin_ref[...]
    hm = hm_ref[...]

    def norm_rope(t, t_sw, gains, width):
        hi, lo = _split_bf16(t * t)
        ms = (jnp.dot(hi, hm[:width, :width], preferred_element_type=F32)
              + jnp.dot(lo, hm[:width, :width], preferred_element_type=F32))
        rs = lax.rsqrt(ms + RMS_EPS)
        c = jnp.concatenate([cos] * (width // 128), axis=1)
        s = jnp.concatenate([sin] * (width // 128), axis=1)
        return (t * gains[0:1, :] * c + t_sw * gains[1:2, :] * s) * rs

    q_ref[...] = (norm_rope(q, q_sw, gq_ref[...], qw) * (HEAD_DIM ** -0.5 * LOG2_E)).astype(BF16)
    k_ref[...] = norm_rope(k, k_sw, gk_ref[...], kw).astype(BF16)
    v_ref[...] = v.astype(BF16)


def _odd_in(rows, x_all, mods, w_cat, cos_t, sin_t, gq, gk, head_mean, qw, kw):
    tm, d = rows.tm, x_all.shape[1]
    n = rows.n_all
    row = lambda w: pl.BlockSpec((tm, w), lambda i: (i, 0))
    tab = pl.BlockSpec((tm, 128), lambda i: (rows.rope_block(i), 0))
    return pl.pallas_call(
        functools.partial(_odd_in_kernel, qw=qw, kw=kw),
        grid=(n // tm,),
        in_specs=[row(d), pl.BlockSpec((1, 6, d), lambda i: (rows.group(i), 0, 0)), _full(w_cat.shape), tab, tab,
                  _full(gq.shape), _full(gk.shape), _full(head_mean.shape)],
        out_specs=[row(qw), row(kw), row(kw)],
        out_shape=[jax.ShapeDtypeStruct((n, qw), BF16), jax.ShapeDtypeStruct((n, kw), BF16),
                   jax.ShapeDtypeStruct((n, kw), BF16)],
        compiler_params=_params(("parallel",)),
        name="odd_in",
    )(x_all, mods, w_cat, cos_t, sin_t, gq, gk, head_mean)


def _dense_attn_kernel(q_ref, kc_ref, kl_ref, vc_ref, vl_ref, o_ref, s0_ref, s1_ref, m0_ref, m1_ref, *, grp):
    half = HEAD_DIM
    ck = DENSE_KEY_CHUNK
    t = pl.program_id(2)
    s_refs, m_refs = (s0_ref, s1_ref), (m0_ref, m1_ref)
    kv_half = ((2 * pl.program_id(1)) // grp) % 2

    @pl.when((pl.program_id(0) == 0) & (pl.program_id(1) == 0) & (t == 0))
    def _():
        for r in s_refs + m_refs:
            r[...] = jnp.zeros_like(r)

    n_ctx = kc_ref.shape[0]
    chunks = [(kc_ref, vc_ref, 0, n_ctx, 0)]
    chunks += [(kl_ref, vl_ref, c, ck, n_ctx + c) for c in range(0, kl_ref.shape[0], ck)]

    def run(new, old):
        qb = q_ref[...]
        sw = jnp.concatenate([qb[:, half:], qb[:, :half]], axis=1)
        lane_half = lax.broadcasted_iota(jnp.int32, qb.shape, 1) // half
        q = jnp.where(lane_half == kv_half, jnp.where(kv_half == new, qb, sw), jnp.zeros_like(qb))
        m_old = m_refs[old][:, 0:1]
        m_new, l, o = None, None, None
        for k_ref, v_ref, start, size, col in chunks:
            s_new = _nt_dot(q, k_ref[start:start + size, :])
            s_refs[new][:, col:col + size] = s_new
            mc = jnp.max(s_new, axis=-1, keepdims=True)
            m_new = mc if m_new is None else jnp.maximum(m_new, mc)
            p = jnp.exp2(s_refs[old][:, col:col + size] - m_old)
            lc = jnp.sum(p, axis=-1, keepdims=True)
            oc = jnp.dot(p.astype(BF16), v_ref[start:start + size, :], preferred_element_type=F32)
            l, o = (lc, oc) if l is None else (l + lc, o + oc)
        m_refs[new][...] = jnp.broadcast_to(m_new, m_refs[new].shape)
        o = o / l
        o = jnp.where(kv_half == old, o, pltpu.roll(o, half, 1))
        o_ref[:, old * half:(old + 1) * half] = o[:, old * half:(old + 1) * half].astype(o_ref.dtype)

    @pl.when(t % 2 == 0)
    def _():
        run(0, 1)

    @pl.when(t % 2 == 1)
    def _():
        run(1, 0)


def _dense_attn(q_rows, k_rows, v_rows, bsz, seq, ctx):
    n_lat = bsz * seq
    n_q = q_rows.shape[1] // HEAD_DIM
    grp = n_q // (k_rows.shape[1] // HEAD_DIM)
    tq = DENSE_TQ
    nq = seq // tq
    n_tiles = 2 * nq
    assert seq % tq == 0 and n_lat % ctx == 0 and grp % 2 == 0 and seq % DENSE_KEY_CHUNK == 0
    kv_blk = lambda pair: pair // grp
    q_blk = lambda t: jnp.minimum(t, n_tiles - 1) // 2
    o_blk = lambda t: jnp.maximum(t - 1, 0) // 2
    kv_ctx = pl.BlockSpec((ctx, 128), lambda b, p, t: (n_lat // ctx + b, kv_blk(p)))
    kv_lat = pl.BlockSpec((seq, 128), lambda b, p, t: (b, kv_blk(p)))
    return pl.pallas_call(
        functools.partial(_dense_attn_kernel, grp=grp),
        grid=(bsz, n_q // 2, n_tiles + 1),
        in_specs=[pl.BlockSpec((tq, 128), lambda b, p, t: (b * nq + q_blk(t), p)), kv_ctx, kv_lat, kv_ctx, kv_lat],
        out_specs=pl.BlockSpec((tq, 128), lambda b, p, t: (b * nq + o_blk(t), p)),
        out_shape=jax.ShapeDtypeStruct((n_lat, q_rows.shape[1]), BF16),
        scratch_shapes=[pltpu.VMEM((tq, ctx + seq), F32), pltpu.VMEM((tq, ctx + seq), F32),
                        pltpu.VMEM((tq, 128), F32), pltpu.VMEM((tq, 128), F32)],
        compiler_params=_params(("arbitrary", "arbitrary", "arbitrary")),
        name="dense_attn",
    )(q_rows, k_rows, k_rows, v_rows, v_rows)


def _rope_tables(seq, tm):
    n_freq = HEAD_DIM // 4
    inv_freq = ROPE_THETA ** (-jnp.arange(n_freq, dtype=F32) / n_freq)
    rows = seq // GRID_W
    r = jnp.repeat(jnp.arange(rows, dtype=F32), GRID_W)
    col = jnp.tile(jnp.arange(GRID_W, dtype=F32), rows)
    ang = jnp.concatenate([r[:, None] * inv_freq, col[:, None] * inv_freq], -1)
    cos, sin = jnp.cos(ang), jnp.sin(ang)
    cos_t = jnp.concatenate([jnp.tile(cos, (1, 4)), jnp.ones((tm, 128), F32)], axis=0)
    sin_t = jnp.concatenate([jnp.tile(sin, (1, 4)), jnp.zeros((tm, 128), F32)], axis=0)
    return cos_t, sin_t


def _swap_halves(w):
    d, n = w.shape
    t = w.reshape(d, n // HEAD_DIM, 2, HEAD_DIM // 2)
    return jnp.stack([-t[:, :, 1], t[:, :, 0]], axis=2).reshape(d, n)


def kernel(x, c, ctx, c_ctx, ada_w, ada_b, ln_g, ln_b, even_w_in, even_w_out, s5_lam_re, s5_lam_im, s5_log_step,
           s5_b_re, s5_b_im, s5_c_re, s5_c_im, s5_d, s5_w_glu, s5_b_glu, win_sink, odd_w_in, odd_w_out,
           odd_q_norm, odd_k_norm, router_w, router_bias, moe_w_gate, moe_w_up, moe_w_down):
    bsz, seq, d = x.shape
    n_ctx = ctx.shape[1]
    assert ada_w.shape[0] == DEPTH == 2
    rows = _Rows(bsz, seq, n_ctx)
    n_lat, n_all = rows.n_lat, rows.n_all
    s5w = s5_d.shape[1]
    win_q = win_sink.shape[1] * HEAD_DIM
    win_kv = (even_w_in.shape[2] - s5w - win_q) // 2
    odd_q = odd_w_out.shape[1]
    odd_kv = (odd_w_in.shape[2] - odd_q) // 2

    cond = jnp.zeros((16, d), F32).at[:bsz].set(c).at[bsz].set(c_ctx)
    mods = _ada(cond, ada_w, ada_b).reshape(DEPTH, 16, 6, d)
    cos_t, sin_t = _rope_tables(seq, rows.tm)
    rw_hi, rw_lo = _split_bf16(router_w.T)
    x_lat, x_ctx = x.reshape(n_lat, d), ctx.reshape(bsz * n_ctx, d)
    lnv = lambda i, j: (ln_g[i, j].reshape(1, d), ln_b[i, j].reshape(1, d))
    moe_w = lambda i: (i, moe_w_gate, moe_w_up, moe_w_down)

    w = even_w_in[0]
    wu, wq, wk, wv = (w[:, :s5w], w[:, s5w:s5w + win_q], w[:, s5w + win_q:s5w + win_q + win_kv],
                      w[:, s5w + win_q + win_kv:])
    w_cat = jnp.concatenate([wu, wq, _swap_halves(wq), wk, _swap_halves(wk), wv], axis=1).astype(BF16)
    u32, u_tm, q0, k0, v0 = _even_in(rows, x_lat, x_ctx, mods[0], w_cat, cos_t, sin_t, s5w, win_q, win_kv)
    wb, a_b, wc = _s5_weights(s5_lam_re[0], s5_lam_im[0], s5_log_step[0], s5_b_re[0], s5_b_im[0],
                              s5_c_re[0], s5_c_im[0], bsz)
    t_all = seq + n_ctx
    y_tm = _s5(u_tm.reshape(t_all * bsz, s5w), wb, a_b, wc, bsz, n_ctx, seq).reshape(2, t_all, bsz * s5w)
    o_rows = _win_attn(rows, q0, k0, v0, win_sink[0].astype(F32))
    g0, b0 = lnv(0, 0)
    even_args = (u32, y_tm, s5_d[0].reshape(1, s5w), s5_w_glu[0].astype(BF16), s5_b_glu[0].reshape(1, s5w))
    x1, h2, lg = _mix_out(rows, n_all, (x_lat, x_ctx), mods[0], o_rows, even_w_out[0].astype(BF16), g0, b0, rw_hi, rw_lo,
                          even_args)
    g1, b1 = lnv(0, 1)
    x2 = _moe(rows, n_all, x1, h2, lg, mods[0], router_bias.astype(F32), *moe_w(0), g1, b1)

    w = odd_w_in[0]
    wq, wk, wv = w[:, :odd_q], w[:, odd_q:odd_q + odd_kv], w[:, odd_q + odd_kv:]
    w_cat = jnp.concatenate([wq, _swap_halves(wq), wk, _swap_halves(wk), wv], axis=1).astype(BF16)

    def gains(gv, width):
        gs = jnp.concatenate([gv[HEAD_DIM // 2:], gv[:HEAD_DIM // 2]])
        return jnp.stack([jnp.tile(gv, width // HEAD_DIM), jnp.tile(gs, width // HEAD_DIM)]).astype(F32)

    head_mean = jnp.kron(jnp.eye(odd_q // HEAD_DIM, dtype=F32),
                         jnp.full((HEAD_DIM, HEAD_DIM), 1.0 / HEAD_DIM, F32)).astype(BF16)
    q1, k1, v1 = _odd_in(rows, x2, mods[1], w_cat, cos_t, sin_t, gains(odd_q_norm[0], odd_q),
                         gains(odd_k_norm[0], odd_kv), head_mean, odd_q, odd_kv)
    o1 = _dense_attn(q1, k1, v1, bsz, seq, n_ctx)
    g0, b0 = lnv(1, 0)
    x1, h2, lg = _mix_out(rows, n_lat, x2, mods[1], o1, odd_w_out[0].astype(BF16), g0, b0, rw_hi, rw_lo)
    g1, b1 = lnv(1, 1)
    out = _moe(rows, n_lat, x1, h2, lg, mods[1], router_bias.astype(F32), *moe_w(1), g1, b1)
    return out.reshape(bsz, seq, d)
```

```python
import functools
import math

import jax
import jax.numpy as jnp
from jax import lax
from jax.experimental import pallas as pl
from jax.experimental.pallas import tpu as pltpu

F32 = jnp.float32
BF16 = jnp.bfloat16

HEAD_DIM = 64
GRID_W = 64
ROPE_THETA = 10000.0
S5_GROUP = 16
S5_STATE = 64
WINDOW = 128
N_EXPERTS = 32
N_GROUPS = 8
EXPERTS_PER_GROUP = N_EXPERTS // N_GROUPS
TOP_K = 2
PAIR_ORDER = ((0, 1), (0, 2), (0, 3), (1, 3), (1, 2), (2, 3))
N_CLASSES = N_GROUPS * len(PAIR_ORDER)
DEPTH = 2
ALPHA = (2 * DEPTH) ** 0.25
LN_EPS = 1e-5
LOG2_E = math.log2(math.e)
RMS_EPS = 1e-6
NEG_INF = -1e30

ROW_TILE = 256
S5_CHUNK = 64
S5_LANE_GROUP = 512
EXPERT_TILE = 256
RANK_BLOCK = 512
DENSE_TQ = 256
DENSE_KEY_CHUNK = 512
VMEM_LIMIT = 48 * 1024 * 1024


def _params(sem):
    return pltpu.CompilerParams(dimension_semantics=sem, vmem_limit_bytes=VMEM_LIMIT)


def _full(shape):
    n = len(shape)
    return pl.BlockSpec(shape, lambda *_: (0,) * n)


def _ada_kernel(c_ref, w_ref, b_ref, o_ref):
    c = c_ref[...]
    s = c * jax.nn.sigmoid(c)
    o_ref[...] = jnp.dot(s, w_ref[0], preferred_element_type=F32, precision=lax.Precision.HIGHEST) + b_ref[0]


def _ada(cond, ada_w, ada_b):
    g, d = cond.shape
    depth, _, n = ada_w.shape
    bn = 1024
    return pl.pallas_call(
        _ada_kernel,
        grid=(depth, n // bn),
        in_specs=[pl.BlockSpec((g, d), lambda i, j: (0, 0)),
                  pl.BlockSpec((1, d, bn), lambda i, j: (i, 0, j)),
                  pl.BlockSpec((1, 1, bn), lambda i, j: (i, 0, j))],
        out_specs=pl.BlockSpec((None, g, bn), lambda i, j: (i, 0, j)),
        out_shape=jax.ShapeDtypeStruct((depth, g, n), F32),
        compiler_params=_params(("arbitrary", "arbitrary")),
        name="ada",
    )(cond, ada_w, ada_b.reshape(depth, 1, n))


class _Rows:
    def __init__(self, bsz, seq, ctx, tm=ROW_TILE):
        assert seq % tm == 0 and ctx % tm == 0
        self.bsz, self.seq, self.ctx, self.tm = bsz, seq, ctx, tm
        self.tpb = seq // tm
        self.cpb = ctx // tm
        self.nl = bsz * self.tpb
        self.nc = bsz * self.cpb
        self.n_lat = bsz * seq
        self.n_all = bsz * (seq + ctx)

    def two_source(self, width):
        return (pl.BlockSpec((self.tm, width), lambda i: (jnp.minimum(i, self.nl - 1), 0)),
                pl.BlockSpec((self.tm, width), lambda i: (jnp.maximum(i - self.nl, 0), 0)))

    def group(self, i):
        return jnp.where(i < self.nl, i // self.tpb, self.bsz)

    def rope_block(self, i):
        return jnp.where(i < self.nl, i % self.tpb, self.tpb)

    def time_major(self, i):
        lat = i < self.nl
        j = i - self.nl
        return (jnp.where(lat, self.cpb + i % self.tpb, j % self.cpb),
                jnp.where(lat, i // self.tpb, j // self.cpb))


def _layer_norm(r, g, b):
    mu = jnp.mean(r, axis=-1, keepdims=True)
    rc = r - mu
    var = jnp.mean(rc * rc, axis=-1, keepdims=True)
    return rc * lax.rsqrt(var + LN_EPS) * g + b


def _even_in_kernel(xl_ref, xc_ref, mod_ref, w_ref, cos_ref, sin_ref, u32_ref, utm_ref, q_ref, k_ref, v_ref, *,
                    s5w, qw, kw, n_lat_tiles):
    m = mod_ref[0]
    x = jnp.where(pl.program_id(0) < n_lat_tiles, xl_ref[...], xc_ref[...])
    h = (x * (1.0 + m[1:2, :]) + m[0:1, :]).astype(BF16)
    r = jnp.dot(h, w_ref[...], preferred_element_type=F32)
    u = r[:, :s5w]
    u32_ref[...] = u
    utm_ref[...] = u.astype(BF16)
    o = s5w
    q, q_sw = r[:, o:o + qw], r[:, o + qw:o + 2 * qw]
    o += 2 * qw
    k, k_sw = r[:, o:o + kw], r[:, o + kw:o + 2 * kw]
    o += 2 * kw
    v = r[:, o:o + kw]
    cos, sin = cos_ref[...], sin_ref[...]
    cq = jnp.concatenate([cos] * (qw // 128), axis=1)
    sq = jnp.concatenate([sin] * (qw // 128), axis=1)
    q_ref[...] = ((q * cq + q_sw * sq) * (HEAD_DIM ** -0.5)).astype(BF16)
    k_ref[...] = (k * cos + k_sw * sin).astype(BF16)
    v_ref[...] = v.astype(BF16)


def _even_in(rows, x_lat, x_ctx, mods, w_cat, cos_t, sin_t, s5w, qw, kw):
    tm, d = rows.tm, x_lat.shape[1]
    n = rows.n_all
    t_all = rows.seq + rows.ctx
    kern = functools.partial(_even_in_kernel, s5w=s5w, qw=qw, kw=kw, n_lat_tiles=rows.nl)
    row = lambda w: pl.BlockSpec((tm, w), lambda i: (i, 0))
    return pl.pallas_call(
        kern,
        grid=(n // tm,),
        in_specs=list(rows.two_source(d)) + [
                  pl.BlockSpec((1, 6, d), lambda i: (rows.group(i), 0, 0)),
                  _full(w_cat.shape),
                  pl.BlockSpec((tm, 128), lambda i: (rows.rope_block(i), 0)),
                  pl.BlockSpec((tm, 128), lambda i: (rows.rope_block(i), 0))],
        out_specs=[row(s5w),
                   pl.BlockSpec((tm, s5w), lambda i: rows.time_major(i)),
                   row(qw), row(kw), row(kw)],
        out_shape=[jax.ShapeDtypeStruct((n, s5w), F32),
                   jax.ShapeDtypeStruct((t_all, rows.bsz * s5w), BF16),
                   jax.ShapeDtypeStruct((n, qw), BF16),
                   jax.ShapeDtypeStruct((n, kw), BF16),
                   jax.ShapeDtypeStruct((n, kw), BF16)],
        compiler_params=_params(("parallel",)),
        name="even_in",
    )(x_lat, x_ctx, mods, w_cat, cos_t, sin_t)


def _s5_kernel(u_ref, wb_ref, a_ref, wc_ref, y_ref, bu_ref, h_ref, *, tc, nb, width, n_state):
    lg = S5_LANE_GROUP
    n_lg = n_state // lg
    kch = width // n_lg
    d = pl.program_id(0)

    @pl.when(pl.program_id(1) == 0)
    def _():
        h_ref[...] = jnp.zeros_like(h_ref)

    u = u_ref[...]
    for g in range(n_lg):
        r = jnp.dot(u[:, kch * g:kch * (g + 1)], wb_ref[0, g], preferred_element_type=F32)
        bu_ref[:, lg * g:lg * (g + 1)] = r[:, :lg]
        bu_ref[:, n_state + lg * g:n_state + lg * (g + 1)] = r[:, lg:]

    rev = d == 1
    for g in range(n_lg):
        re = slice(lg * g, lg * (g + 1))
        im = slice(n_state + lg * g, n_state + lg * (g + 1))
        a_re = a_ref[0, :, re]
        a_im = a_ref[0, :, im]

        def body(i, carry, re=re, im=im, a_re=a_re, a_im=a_im):
            hr, hi = carry
            t = jnp.where(rev, tc - 1 - i, i)
            row = pl.multiple_of(t * nb, nb)
            br = bu_ref[pl.ds(row, nb), re]
            bi = bu_ref[pl.ds(row, nb), im]
            nr = a_re * hr - a_im * hi + br
            ni = a_re * hi + a_im * hr + bi
            bu_ref[pl.ds(row, nb), re] = nr
            bu_ref[pl.ds(row, nb), im] = ni
            return nr, ni

        hr, hi = lax.fori_loop(0, tc, body, (h_ref[:, re], h_ref[:, im]), unroll=4)
        h_ref[:, re] = hr
        h_ref[:, im] = hi

    for g in range(n_lg):
        hc = jnp.concatenate([bu_ref[:, lg * g:lg * (g + 1)],
                              bu_ref[:, n_state + lg * g:n_state + lg * (g + 1)]], axis=1).astype(BF16)
        y_ref[0, :, kch * g:kch * (g + 1)] = jnp.dot(hc, wc_ref[0, g], preferred_element_type=F32)


def _s5(u_tm, wb, a_b, wc, nb, ctx, seq):
    tc = S5_CHUNK
    t_all = ctx + seq
    width = u_tm.shape[1]
    n_state = a_b.shape[2] // 2
    assert ctx % tc == 0 and seq % tc == 0 and nb == 8
    n_ctx, n_chunks = ctx // tc, t_all // tc

    def chunk(d, s):
        back = jnp.where(s < n_ctx, n_ctx - 1 - s, n_chunks - 1 - (s - n_ctx))
        return jnp.where(d == 0, s, back)

    kern = functools.partial(_s5_kernel, tc=tc, nb=nb, width=width, n_state=n_state)
    return pl.pallas_call(
        kern,
        grid=(2, n_chunks),
        in_specs=[pl.BlockSpec((tc * nb, width), lambda d, s: (chunk(d, s), 0)),
                  pl.BlockSpec((1,) + wb.shape[1:], lambda d, s: (d, 0, 0, 0)),
                  pl.BlockSpec((1,) + a_b.shape[1:], lambda d, s: (d, 0, 0)),
                  pl.BlockSpec((1,) + wc.shape[1:], lambda d, s: (d, 0, 0, 0))],
        out_specs=pl.BlockSpec((1, tc * nb, width), lambda d, s: (d, chunk(d, s), 0)),
        out_shape=jax.ShapeDtypeStruct((2, t_all * nb, width), F32),
        scratch_shapes=[pltpu.VMEM((tc * nb, 2 * n_state), F32), pltpu.VMEM((nb, 2 * n_state), F32)],
        compiler_params=_params(("arbitrary", "arbitrary")),
        name="s5_scan",
    )(u_tm, wb, a_b, wc)


def _s5_weights(lam_re, lam_im, log_step, b_re, b_im, c_re, c_im, nb):
    n_dir, n_grp, n_p = lam_re.shape
    ch = b_re.shape[-1]
    lr, li = lam_re.astype(F32), lam_im.astype(F32)
    dt = jnp.exp(log_step.astype(F32))[..., None]
    mag = jnp.exp(lr * dt)
    ab_re, ab_im = mag * jnp.cos(li * dt), mag * jnp.sin(li * dt)
    den = lr * lr + li * li
    num_re, num_im = ab_re - 1.0, ab_im
    f_re = (num_re * lr + num_im * li) / den
    f_im = (num_im * lr - num_re * li) / den
    br, bi = b_re.astype(F32), b_im.astype(F32)
    bb_re = f_re[..., None] * br - f_im[..., None] * bi
    bb_im = f_re[..., None] * bi + f_im[..., None] * br
    n_state = n_grp * n_p
    n_lg = n_state // S5_LANE_GROUP
    gpl = n_grp // n_lg
    eye = jnp.eye(gpl, dtype=F32)

    def pack_in(bb):
        t = bb.reshape(n_dir, n_lg, gpl, n_p, ch)
        m = jnp.einsum('dlgpc,gh->dlgchp', t, eye)
        return m.reshape(n_dir, n_lg, gpl * ch, gpl * n_p)

    def pack_out(c):
        t = c.astype(F32).reshape(n_dir, n_lg, gpl, ch, n_p)
        m = jnp.einsum('dlgcp,gh->dlgphc', t, eye)
        return m.reshape(n_dir, n_lg, gpl * n_p, gpl * ch)

    wb = jnp.concatenate([pack_in(bb_re), pack_in(bb_im)], axis=-1).astype(BF16)
    wc = jnp.concatenate([pack_out(c_re), -pack_out(c_im)], axis=-2).astype(BF16)
    a = jnp.concatenate([ab_re.reshape(n_dir, n_state), ab_im.reshape(n_dir, n_state)], axis=-1)
    a_b = jnp.broadcast_to(a[:, None, :], (n_dir, nb, 2 * n_state))
    return wb, a_b, wc


def _nt_dot(a, b):
    return lax.dot_general(a, b, (((1,), (1,)), ((), ())), preferred_element_type=F32)


def _win_attn_kernel(sink_ref, q_ref, kc_ref, vc_ref, kl_ref, vl_ref, o_ref, *, tq, seq, grp, n_lat_tiles):
    i = pl.program_id(0)
    n_heads = q_ref.shape[1] // HEAD_DIM
    kc, vc = kc_ref[...], vc_ref[...]
    zeros = jnp.zeros((tq, HEAD_DIM), q_ref.dtype)

    def attend(local):
        if local:
            band = tq + 2 * WINDOW
            j = i % (seq // tq)
            start = pl.multiple_of(jnp.clip(j * tq - WINDOW, 0, seq - band), 8)
            kb = kl_ref[pl.ds(start, band), :]
            vb = vl_ref[pl.ds(start, band), :]
            qpos = j * tq + lax.broadcasted_iota(jnp.int32, (tq, band), 0)
            kpos = start + lax.broadcasted_iota(jnp.int32, (tq, band), 1)
            valid = jnp.abs(qpos - kpos) <= WINDOW
        outs = []
        for hh in range(n_heads):
            kv_half = (hh // grp) % 2
            qh = q_ref[:, hh * HEAD_DIM:(hh + 1) * HEAD_DIM]
            q = jnp.concatenate([qh, zeros] if kv_half == 0 else [zeros, qh], axis=1)
            sink = sink_ref[hh]
            s_c = _nt_dot(q, kc)
            m = jnp.maximum(jnp.max(s_c, axis=-1, keepdims=True), sink)
            if local:
                s_l = jnp.where(valid, _nt_dot(q, kb), NEG_INF)
                m = jnp.maximum(m, jnp.max(s_l, axis=-1, keepdims=True))
            p_c = jnp.exp(s_c - m)
            l = jnp.sum(p_c, axis=-1, keepdims=True) + jnp.exp(sink - m)
            acc = jnp.dot(p_c.astype(BF16), vc, preferred_element_type=F32)
            if local:
                p_l = jnp.exp(s_l - m)
                l = l + jnp.sum(p_l, axis=-1, keepdims=True)
                acc = acc + jnp.dot(p_l.astype(BF16), vb, preferred_element_type=F32)
            outs.append((acc / l)[:, kv_half * HEAD_DIM:(kv_half + 1) * HEAD_DIM])
        o_ref[...] = jnp.concatenate(outs, axis=1).astype(o_ref.dtype)

    @pl.when(i < n_lat_tiles)
    def _():
        attend(True)

    @pl.when(i >= n_lat_tiles)
    def _():
        attend(False)


def _win_attn(rows, q_rows, k_rows, v_rows, sink):
    tq, seq, ctx, bsz = rows.tm, rows.seq, rows.ctx, rows.bsz
    assert k_rows.shape[1] == 2 * HEAD_DIM and ctx == tq and seq >= tq + 2 * WINDOW
    grp = q_rows.shape[1] // k_rows.shape[1]
    nq = seq // tq
    n_lat_tiles = bsz * nq
    sample = lambda i: jnp.where(i < n_lat_tiles, i // nq, i - n_lat_tiles)
    kv_ctx = pl.BlockSpec((ctx, 128), lambda i: (rows.n_lat // ctx + sample(i), 0))
    kv_lat = pl.BlockSpec((seq, 128), lambda i: (sample(i), 0))
    qs = pl.BlockSpec((tq, q_rows.shape[1]), lambda i: (i, 0))
    return pl.pallas_call(
        functools.partial(_win_attn_kernel, tq=tq, seq=seq, grp=grp, n_lat_tiles=n_lat_tiles),
        grid=(n_lat_tiles + bsz,),
        in_specs=[pl.BlockSpec(memory_space=pltpu.SMEM), qs, kv_ctx, kv_ctx, kv_lat, kv_lat],
        out_specs=qs,
        out_shape=jax.ShapeDtypeStruct(q_rows.shape, BF16),
        compiler_params=_params(("parallel",)),
        name="win_attn",
    )(sink, q_rows, k_rows, v_rows, k_rows, v_rows)


def _gelu_tanh(y):
    return 0.5 * y * (1.0 + jnp.tanh(math.sqrt(2.0 / math.pi) * (y + 0.044715 * (y * y * y))))


def _split_bf16(v):
    hi = v.astype(BF16)
    return hi, (v - hi.astype(F32)).astype(BF16)


def _mix_out_kernel(*refs, even, n_lat_tiles):
    if even:
        (xl_ref, xc_ref, mod_ref, o_ref, wout_ref, lng_ref, lnb_ref, rwh_ref, rwl_ref,
         u_ref, yf_ref, yb_ref, dsk_ref, wglu_ref, bglu_ref, x1_ref, h2_ref, lg_ref) = refs
        x = jnp.where(pl.program_id(0) < n_lat_tiles, xl_ref[...], xc_ref[...])
        y = u_ref[...] * dsk_ref[...] + yf_ref[0] + yb_ref[0]
        z = _gelu_tanh(y)
        gate = jax.nn.sigmoid(jnp.dot(z.astype(BF16), wglu_ref[...], preferred_element_type=F32) + bglu_ref[...])
        mix = jnp.concatenate([(z * gate).astype(BF16), o_ref[...]], axis=1)
    else:
        (x_ref, mod_ref, o_ref, wout_ref, lng_ref, lnb_ref, rwh_ref, rwl_ref, x1_ref, h2_ref, lg_ref) = refs
        x = x_ref[...]
        mix = o_ref[...]
    m = mod_ref[0]
    ol = jnp.dot(mix, wout_ref[...], preferred_element_type=F32)
    x1 = _layer_norm(ALPHA * x + m[2:3, :] * ol, lng_ref[...], lnb_ref[...])
    x1_ref[...] = x1
    h2 = x1 * (1.0 + m[4:5, :]) + m[3:4, :]
    h2_ref[...] = h2
    hh, hl = _split_bf16(h2)
    rwh, rwl = rwh_ref[...], rwl_ref[...]
    lg_ref[...] = _nt_dot(rwh, hh) + (_nt_dot(rwh, hl) + _nt_dot(rwl, hh))


def _mix_out(rows, n_rows, x_src, mods, o_rows, w_out, ln_g, ln_b, rw_hi, rw_lo, even_args=None):
    even = even_args is not None
    tm, d = rows.tm, w_out.shape[1]
    row = lambda w: pl.BlockSpec((tm, w), lambda i: (i, 0))
    x_ins, x_specs = (list(x_src), list(rows.two_source(d))) if even else ([x_src], [row(d)])
    ins = x_ins + [mods, o_rows, w_out, ln_g, ln_b, rw_hi, rw_lo]
    specs = x_specs + [pl.BlockSpec((1, 6, d), lambda i: (rows.group(i), 0, 0)), row(o_rows.shape[1]),
                       _full(w_out.shape), _full(ln_g.shape), _full(ln_b.shape), _full(rw_hi.shape),
                       _full(rw_lo.shape)]
    if even:
        u32, y_tm, d_skip, w_glu, b_glu = even_args
        s5w = u32.shape[1]
        tmaj = lambda dr: pl.BlockSpec((1, tm, s5w), lambda i: (dr,) + tuple(rows.time_major(i)))
        ins += [u32, y_tm, y_tm, d_skip, w_glu, b_glu]
        specs += [row(s5w), tmaj(0), tmaj(1), _full(d_skip.shape), _full(w_glu.shape), _full(b_glu.shape)]
    n_e = rw_hi.shape[0]
    return pl.pallas_call(
        functools.partial(_mix_out_kernel, even=even, n_lat_tiles=rows.nl),
        grid=(n_rows // tm,),
        in_specs=specs,
        out_specs=[row(d), row(d), pl.BlockSpec((n_e, tm), lambda i: (0, i))],
        out_shape=[jax.ShapeDtypeStruct((n_rows, d), F32), jax.ShapeDtypeStruct((n_rows, d), F32),
                   jax.ShapeDtypeStruct((n_e, n_rows), F32)],
        compiler_params=_params(("parallel",)),
        name="mix_out_even" if even else "mix_out_odd",
    )(*ins)


def _pair_index(lo, hi):
    out = jnp.zeros(lo.shape, jnp.int32)
    for idx, (a, b) in enumerate(PAIR_ORDER):
        out = jnp.where((lo == a) & (hi == b), idx, out)
    return out


def _route_kernel(lg_ref, bias_ref, cls_ref, wlo_ref, whi_ref):
    epg = EXPERTS_PER_GROUP
    sc = [jax.nn.sigmoid(lg_ref[e]) for e in range(N_EXPERTS)]
    sel = [sc[e] + bias_ref[e] for e in range(N_EXPERTS)]
    best, gidx = None, None
    for g in range(N_GROUPS):
        a = sel[g * epg:(g + 1) * epg]
        pair = None
        for i in range(epg):
            for j in range(i + 1, epg):
                s2 = a[i] + a[j]
                pair = s2 if pair is None else jnp.maximum(pair, s2)
        if best is None:
            best, gidx = pair, jnp.zeros(pair.shape, jnp.int32)
        else:
            take = pair > best
            best = jnp.where(take, pair, best)
            gidx = jnp.where(take, g, gidx)
    sg = [sel[i] for i in range(epg)]
    cg = [sc[i] for i in range(epg)]
    for g in range(1, N_GROUPS):
        hit = gidx == g
        sg = [jnp.where(hit, sel[g * epg + i], sg[i]) for i in range(epg)]
        cg = [jnp.where(hit, sc[g * epg + i], cg[i]) for i in range(epg)]

    def first_max(vals, skip=None):
        bv, bi, bs = None, None, None
        for i in range(epg):
            v = vals[i] if skip is None else jnp.where(skip == i, -jnp.inf, vals[i])
            if bv is None:
                bv, bi, bs = v, jnp.zeros(v.shape, jnp.int32), cg[0]
            else:
                take = v > bv
                bv = jnp.where(take, v, bv)
                bi = jnp.where(take, i, bi)
                bs = jnp.where(take, cg[i], bs)
        return bi, bs

    i0, s0 = first_max(sg)
    i1, s1 = first_max(sg, skip=i0)
    tot = s0 + s1
    w0, w1 = s0 / tot, s1 / tot
    first_low = i0 < i1
    lo, hi = jnp.where(first_low, i0, i1), jnp.where(first_low, i1, i0)
    cls_ref[...] = gidx * len(PAIR_ORDER) + _pair_index(lo, hi)
    wlo_ref[...] = jnp.where(first_low, w0, w1)
    whi_ref[...] = jnp.where(first_low, w1, w0)


def _route(logits_t, bias):
    n_e, n = logits_t.shape
    r = n // 128
    rb = 8
    assert n % (128 * rb) == 0
    lg3 = logits_t.reshape(n_e, r, 128)
    blk = pl.BlockSpec((rb, 128), lambda i: (i, 0))
    outs = pl.pallas_call(
        _route_kernel,
        grid=(r // rb,),
        in_specs=[pl.BlockSpec((n_e, rb, 128), lambda i: (0, i, 0)), pl.BlockSpec(memory_space=pltpu.SMEM)],
        out_specs=[blk, blk, blk],
        out_shape=[jax.ShapeDtypeStruct((r, 128), jnp.int32)] + [jax.ShapeDtypeStruct((r, 128), F32)] * 2,
        compiler_params=_params(("parallel",)),
        name="route",
    )(lg3, bias)
    return [o.reshape(n) for o in outs]


def _rank_kernel(c_ref, tri_ref, rank_ref, cnt_ref, carry_ref, *, bw):
    @pl.when(pl.program_id(0) == 0)
    def _():
        carry_ref[...] = jnp.zeros_like(carry_ref)

    hit = lax.broadcasted_iota(jnp.int32, (N_CLASSES, bw), 0) == c_ref[...]
    onehot = jnp.where(hit, 1.0, 0.0).astype(BF16)
    run = jnp.dot(onehot, tri_ref[...], preferred_element_type=F32) + carry_ref[...]
    rank_ref[...] = (jnp.sum(jnp.where(hit, run, 0.0), axis=0, keepdims=True) - 1.0).astype(jnp.int32)
    carry_ref[...] = run[:, bw - 1:bw]
    cnt_ref[...] = run[:, bw - 1:bw]


def _class_ranks(cls):
    n = cls.shape[0]
    bw = RANK_BLOCK
    assert n % bw == 0
    tri = (jnp.arange(bw)[:, None] <= jnp.arange(bw)[None, :]).astype(BF16)
    rank, cnt = pl.pallas_call(
        functools.partial(_rank_kernel, bw=bw),
        grid=(n // bw,),
        in_specs=[pl.BlockSpec((1, bw), lambda i: (0, i)), _full((bw, bw))],
        out_specs=[pl.BlockSpec((1, bw), lambda i: (0, i)), _full((N_CLASSES, 1))],
        out_shape=[jax.ShapeDtypeStruct((1, n), jnp.int32), jax.ShapeDtypeStruct((N_CLASSES, 1), F32)],
        scratch_shapes=[pltpu.VMEM((N_CLASSES, 1), F32)],
        compiler_params=_params(("arbitrary",)),
        name="class_ranks",
    )(cls.reshape(1, n), tri)
    return rank.reshape(n), cnt.reshape(N_CLASSES).astype(jnp.int32)


def _dispatch_plan(cls, n_tok):
    te = EXPERT_TILE
    n_pairs = len(PAIR_ORDER)
    p_max = -(-(n_tok + N_CLASSES * (te - 1)) // te) * te
    rank, counts = _class_ranks(cls)
    padded = (counts + te - 1) // te * te
    pad_end = jnp.cumsum(padded)
    pad_start = pad_end - padded
    classes = jnp.arange(N_CLASSES, dtype=jnp.int32)
    dest = rank + jnp.sum(jnp.where(cls[:, None] == classes[None, :], pad_start[None, :], 0), axis=1)
    tiles = jnp.arange(p_max // te, dtype=jnp.int32) * te
    tile_class = jnp.minimum(jnp.sum(tiles[:, None] >= pad_end[None, :], axis=1), N_CLASSES - 1)
    e_lo = jnp.array([EXPERTS_PER_GROUP * (c // n_pairs) + PAIR_ORDER[c % n_pairs][0] for c in range(N_CLASSES)],
                     jnp.int32)
    e_hi = jnp.array([EXPERTS_PER_GROUP * (c // n_pairs) + PAIR_ORDER[c % n_pairs][1] for c in range(N_CLASSES)],
                     jnp.int32)
    is_c = tile_class[:, None] == classes[None, :]
    tile_lo = jnp.sum(jnp.where(is_c, e_lo[None, :], 0), axis=1).astype(jnp.int32)
    tile_hi = jnp.sum(jnp.where(is_c, e_hi[None, :], 0), axis=1).astype(jnp.int32)
    n_used = (pad_end[-1] // te).astype(jnp.int32).reshape(1)
    return dest, pad_start + counts, pad_end, tile_lo, tile_hi, n_used, p_max


def _dispatch_kernel(lo_ref, hi_ref, nu_ref, d_ref, h_ref, w_ref, xs_ref, stage_ref, zero_ref, sems, zsem, *, tm):
    i = pl.program_id(0)
    slot = i % 2
    d = h_ref.shape[1]
    stage_ref[slot, :, :d] = h_ref[...]
    stage_ref[slot, :, d:] = w_ref[...]

    def issue(r2, _):
        for k in range(2):
            r = 2 * r2 + k
            pltpu.make_async_copy(stage_ref.at[slot, pl.ds(r, 1)], xs_ref.at[pl.ds(d_ref[0, 0, r], 1)],
                                  sems.at[slot]).start(priority=k)
        return 0

    lax.fori_loop(0, tm // 2, issue, 0, unroll=4)

    def wait_rows(s):
        pltpu.make_async_copy(xs_ref.at[pl.ds(0, tm)], xs_ref.at[pl.ds(0, tm)], sems.at[s]).wait()

    @pl.when(i > 0)
    def _():
        wait_rows(1 - slot)

    @pl.when(pl.program_id(0) == pl.num_programs(0) - 1)
    def _():
        zero_ref[...] = jnp.zeros_like(zero_ref)

        def pad_copy(r):
            return pltpu.make_async_copy(zero_ref.at[pl.ds(0, 1)], xs_ref.at[pl.ds(r, 1)], zsem)

        def per_class(c, _):
            lax.fori_loop(lo_ref[c], hi_ref[c], lambda r, a: (pad_copy(r).start(), a)[1], 0)
            lax.fori_loop(lo_ref[c], hi_ref[c], lambda r, a: (pad_copy(r).wait(), a)[1], 0)
            return 0

        lax.fori_loop(0, N_CLASSES, per_class, 0)

        te = zero_ref.shape[0]

        def tail_copy(t):
            return pltpu.make_async_copy(zero_ref, xs_ref.at[pl.ds(pl.multiple_of(t * te, te), te)], zsem)

        n_tiles = xs_ref.shape[0] // te
        lax.fori_loop(nu_ref[0], n_tiles, lambda t, c: (tail_copy(t).start(), c)[1], 0)
        lax.fori_loop(nu_ref[0], n_tiles, lambda t, c: (tail_copy(t).wait(), c)[1], 0)
        wait_rows(slot)


def _dispatch(h2, w_cols, n_rows, dest, pad_lo, pad_hi, n_used, p_max, tm):
    d, wl = h2.shape[1], w_cols.shape[1]
    nt = n_rows // tm
    grid_spec = pltpu.PrefetchScalarGridSpec(
        num_scalar_prefetch=3,
        grid=(nt,),
        in_specs=[pl.BlockSpec((1, 1, tm), lambda i, *_: (i, 0, 0), memory_space=pltpu.SMEM),
                  pl.BlockSpec((tm, d), lambda i, *_: (i, 0)), pl.BlockSpec((tm, wl), lambda i, *_: (i, 0))],
        out_specs=pl.BlockSpec(memory_space=pl.ANY),
        scratch_shapes=[pltpu.VMEM((2, tm, d + wl), h2.dtype), pltpu.VMEM((EXPERT_TILE, d + wl), h2.dtype),
                        pltpu.SemaphoreType.DMA((2,)), pltpu.SemaphoreType.DMA(())],
    )
    return pl.pallas_call(
        functools.partial(_dispatch_kernel, tm=tm),
        grid_spec=grid_spec,
        out_shape=jax.ShapeDtypeStruct((p_max, d + wl), h2.dtype),
        compiler_params=_params(("arbitrary",)),
        name="dispatch",
    )(pad_lo, pad_hi, n_used, dest.reshape(nt, 1, tm), h2, w_cols)


def _expert_kernel(tlo_ref, thi_ref, nu_ref, x_ref, *refs, d, ff):
    w_lo, w_hi = refs[0:3], refs[3:6]
    y_ref = refs[6]
    bf_lo, bf_hi = refs[7:9], refs[9:11]
    i = pl.program_id(0)
    prev = jnp.maximum(i - 1, 0)

    def refresh(t_ref, w_refs, bf_refs):
        @pl.when((i == 0) | (t_ref[i] != t_ref[prev]))
        def _():
            bf_refs[0][:, :ff] = w_refs[0][0, 0].astype(BF16)
            bf_refs[0][:, ff:] = w_refs[1][0, 0].astype(BF16)
            bf_refs[1][...] = w_refs[2][0, 0].astype(BF16)

    refresh(tlo_ref, w_lo, bf_lo)
    refresh(thi_ref, w_hi, bf_hi)

    @pl.when(i < nu_ref[0])
    def _():
        x = x_ref[:, :d].astype(BF16)

        def ffn(bf_refs):
            gu = jnp.dot(x, bf_refs[0][...], preferred_element_type=F32)
            g, u = gu[:, :ff], gu[:, ff:]
            hid = (g * jax.nn.sigmoid(g) * u).astype(BF16)
            return jnp.dot(hid, bf_refs[1][...], preferred_element_type=F32)

        y_ref[...] = x_ref[:, d:d + 1] * ffn(bf_lo) + x_ref[:, d + 1:d + 2] * ffn(bf_hi)

    @pl.when(i >= nu_ref[0])
    def _():
        y_ref[...] = jnp.zeros_like(y_ref)


def _expert_ffn(xs, tile_lo, tile_hi, n_used, layer, w_gate, w_up, w_down):
    p, width = xs.shape
    te = EXPERT_TILE
    d, ff = w_gate.shape[2], w_gate.shape[3]
    lo_w = lambda shape: pl.BlockSpec((1, 1) + shape, lambda i, tl, th, n: (layer, tl[i], 0, 0))
    hi_w = lambda shape: pl.BlockSpec((1, 1) + shape, lambda i, tl, th, n: (layer, th[i], 0, 0))
    grid_spec = pltpu.PrefetchScalarGridSpec(
        num_scalar_prefetch=3,
        grid=(p // te,),
        in_specs=[pl.BlockSpec((te, width), lambda i, tl, th, n: (jnp.minimum(i, n[0] - 1), 0)),
                  lo_w((d, ff)), lo_w((d, ff)), lo_w((ff, d)), hi_w((d, ff)), hi_w((d, ff)), hi_w((ff, d))],
        out_specs=pl.BlockSpec((te, d), lambda i, tl, th, n: (i, 0)),
        scratch_shapes=[pltpu.VMEM((d, 2 * ff), BF16), pltpu.VMEM((ff, d), BF16),
                        pltpu.VMEM((d, 2 * ff), BF16), pltpu.VMEM((ff, d), BF16)],
    )
    return pl.pallas_call(
        functools.partial(_expert_kernel, d=d, ff=ff),
        grid_spec=grid_spec,
        out_shape=jax.ShapeDtypeStruct((p, d), F32),
        compiler_params=_params(("arbitrary",)),
        name="expert_ffn",
    )(tile_lo, tile_hi, n_used, xs, w_gate, w_up, w_down, w_gate, w_up, w_down)


def _moe_out_kernel(d_ref, n_ref, x_ref, mod_ref, ys_ref, lng_ref, lnb_ref, o_ref, y_ref, sems, *, tm):
    i = pl.program_id(0)
    slot = i % 2

    def start_rows(idx_ref, dst):
        def issue(r2, _):
            for k in range(2):
                r = 2 * r2 + k
                pltpu.make_async_copy(ys_ref.at[pl.ds(idx_ref[0, 0, r], 1)], y_ref.at[dst, pl.ds(r, 1)],
                                      sems.at[dst]).start(priority=k)
            return 0

        lax.fori_loop(0, tm // 2, issue, 0, unroll=4)

    @pl.when(i == 0)
    def _():
        start_rows(d_ref, 0)

    @pl.when(i + 1 < pl.num_programs(0))
    def _():
        start_rows(n_ref, 1 - slot)

    pltpu.make_async_copy(ys_ref.at[pl.ds(0, tm)], y_ref.at[slot], sems.at[slot]).wait()
    m = mod_ref[0]
    o_ref[...] = _layer_norm(ALPHA * x_ref[...] + m[5:6, :] * y_ref[slot], lng_ref[...], lnb_ref[...])


def _moe_out(rows, n_rows, x1, mods, ys, dest, ln_g, ln_b):
    tm, d = rows.tm, x1.shape[1]
    nt = n_rows // tm
    row = pl.BlockSpec((tm, d), lambda i: (i, 0))
    idx = pl.BlockSpec((1, 1, tm), lambda i: (i, 0, 0), memory_space=pltpu.SMEM)
    idx_next = pl.BlockSpec((1, 1, tm), lambda i: (jnp.minimum(i + 1, nt - 1), 0, 0), memory_space=pltpu.SMEM)
    d3 = dest.reshape(nt, 1, tm)
    return pl.pallas_call(
        functools.partial(_moe_out_kernel, tm=tm),
        grid=(nt,),
        in_specs=[idx, idx_next, row, pl.BlockSpec((1, 6, d), lambda i: (rows.group(i), 0, 0)),
                  pl.BlockSpec(memory_space=pl.ANY), _full(ln_g.shape), _full(ln_b.shape)],
        out_specs=row,
        out_shape=jax.ShapeDtypeStruct((n_rows, d), F32),
        scratch_shapes=[pltpu.VMEM((2, tm, d), F32), pltpu.SemaphoreType.DMA((2,))],
        compiler_params=_params(("arbitrary",)),
        name="moe_out",
    )(d3, d3, x1, mods, ys, ln_g, ln_b)


def _moe(rows, n_rows, x1, h2, logits_t, mods, router_bias, layer, w_gate, w_up, w_down, ln_g, ln_b):
    cls, w_lo, w_hi = _route(logits_t, router_bias)
    dest, pad_lo, pad_hi, tile_lo, tile_hi, n_used, p_max = _dispatch_plan(cls, n_rows)
    lane = jnp.arange(128, dtype=jnp.int32)[None, :]
    w_cols = jnp.where(lane == 0, w_lo[:, None], jnp.where(lane == 1, w_hi[:, None], 0.0))
    xs = _dispatch(h2, w_cols, n_rows, dest, pad_lo, pad_hi, n_used, p_max, rows.tm)
    ys = _expert_ffn(xs, tile_lo, tile_hi, n_used, layer, w_gate, w_up, w_down)
    return _moe_out(rows, n_rows, x1, mods, ys, dest, ln_g, ln_b)


def _odd_in_kernel(x_ref, mod_ref, w_ref, cos_ref, sin_ref, gq_ref, gk_ref, hm_ref, q_ref, k_ref, v_ref, *, qw, kw):
    m = mod_ref[0]
    h = (x_ref[...] * (1.0 + m[1:2, :]) + m[0:1, :]).astype(BF16)
    r = jnp.dot(h, w_ref[...], preferred_element_type=F32)
    q, q_sw = r[:, :qw], r[:, qw:2 * qw]
    o = 2 * qw
    k, k_sw = r[:, o:o + kw], r[:, o + kw:o + 2 * kw]
    v = r[:, o + 2 * kw:o + 3 * kw]
    cos, sin = cos_ref[...], sin_ref[...]
    hm = hm_ref[...]

    def norm_rope(t, t_sw, gains, width):
        hi, lo = _split_bf16(t * t)
        ms = (jnp.dot(hi, hm[:width, :width], preferred_element_type=F32)
              + jnp.dot(lo, hm[:width, :width], preferred_element_type=F32))
        rs = lax.rsqrt(ms + RMS_EPS)
        c = jnp.concatenate([cos] * (width // 128), axis=1)
        s = jnp.concatenate([sin] * (width // 128), axis=1)
        return (t * gains[0:1, :] * c + t_sw * gains[1:2, :] * s) * rs

    q_ref[...] = (norm_rope(q, q_sw, gq_ref[...], qw) * (HEAD_DIM ** -0.5 * LOG2_E)).astype(BF16)
    k_ref[...] = norm_rope(k, k_sw, gk_ref[...], kw).astype(BF16)
    v_ref[...] = v.astype(BF16)


def _odd_in(rows, x_all, mods, w_cat, cos_t, sin_t, gq, gk, head_mean, qw, kw):
    tm, d = rows.tm, x_all.shape[1]
    n = rows.n_all
    row = lambda w: pl.BlockSpec((tm, w), lambda i: (i, 0))
    tab = pl.BlockSpec((tm, 128), lambda i: (rows.rope_block(i), 0))
    return pl.pallas_call(
        functools.partial(_odd_in_kernel, qw=qw, kw=kw),
        grid=(n // tm,),
        in_specs=[row(d), pl.BlockSpec((1, 6, d), lambda i: (rows.group(i), 0, 0)), _full(w_cat.shape), tab, tab,
                  _full(gq.shape), _full(gk.shape), _full(head_mean.shape)],
        out_specs=[row(qw), row(kw), row(kw)],
        out_shape=[jax.ShapeDtypeStruct((n, qw), BF16), jax.ShapeDtypeStruct((n, kw), BF16),
                   jax.ShapeDtypeStruct((n, kw), BF16)],
        compiler_params=_params(("parallel",)),
        name="odd_in",
    )(x_all, mods, w_cat, cos_t, sin_t, gq, gk, head_mean)


def _dense_attn_kernel(q_ref, kc_ref, kl_ref, vc_ref, vl_ref, o_ref, s0_ref, s1_ref, m0_ref, m1_ref, *, grp):
    half = HEAD_DIM
    ck = DENSE_KEY_CHUNK
    t = pl.program_id(2)
    s_refs, m_refs = (s0_ref, s1_ref), (m0_ref, m1_ref)
    kv_half = ((2 * pl.program_id(1)) // grp) % 2

    @pl.when((pl.program_id(0) == 0) & (pl.program_id(1) == 0) & (t == 0))
    def _():
        for r in s_refs + m_refs:
            r[...] = jnp.zeros_like(r)

    n_ctx = kc_ref.shape[0]
    chunks = [(kc_ref, vc_ref, 0, n_ctx, 0)]
    chunks += [(kl_ref, vl_ref, c, ck, n_ctx + c) for c in range(0, kl_ref.shape[0], ck)]

    def run(new, old):
        qb = q_ref[...]
        sw = jnp.concatenate([qb[:, half:], qb[:, :half]], axis=1)
        lane_half = lax.broadcasted_iota(jnp.int32, qb.shape, 1) // half
        q = jnp.where(lane_half == kv_half, jnp.where(kv_half == new, qb, sw), jnp.zeros_like(qb))
        m_old = m_refs[old][0:1, :]
        m_new, l, o_t = None, None, None
        for k_ref, vt_ref, start, size, col in chunks:
            s_new = _nt_dot(k_ref[start:start + size, :], q)
            s_refs[new][col:col + size, :] = s_new
            mc = jnp.max(s_new, axis=0, keepdims=True)
            m_new = mc if m_new is None else jnp.maximum(m_new, mc)
            p = jnp.exp2(s_refs[old][col:col + size, :] - m_old)
            lc = jnp.sum(p, axis=0, keepdims=True)
            oc = jnp.dot(vt_ref[:, start:start + size], p.astype(BF16), preferred_element_type=F32)
            l, o_t = (lc, oc) if l is None else (l + lc, o_t + oc)
        m_refs[new][...] = jnp.broadcast_to(m_new, m_refs[new].shape)
        o = (o_t / l).T
        o = jnp.where(kv_half == old, o, pltpu.roll(o, half, 1))
        o_ref[:, old * half:(old + 1) * half] = o[:, old * half:(old + 1) * half].astype(o_ref.dtype)

    @pl.when(t % 2 == 0)
    def _():
        run(0, 1)

    @pl.when(t % 2 == 1)
    def _():
        run(1, 0)


def _dense_attn(q_rows, k_rows, v_cols, bsz, seq, ctx):
    n_lat = bsz * seq
    n_q = q_rows.shape[1] // HEAD_DIM
    grp = n_q // (k_rows.shape[1] // HEAD_DIM)
    tq = DENSE_TQ
    nq = seq // tq
    n_tiles = 2 * nq
    assert seq % tq == 0 and n_lat % ctx == 0 and grp % 2 == 0 and seq % DENSE_KEY_CHUNK == 0
    kv_blk = lambda pair: pair // grp
    q_blk = lambda t: jnp.minimum(t, n_tiles - 1) // 2
    o_blk = lambda t: jnp.maximum(t - 1, 0) // 2
    k_ctx = pl.BlockSpec((ctx, 128), lambda b, p, t: (n_lat // ctx + b, kv_blk(p)))
    k_lat = pl.BlockSpec((seq, 128), lambda b, p, t: (b, kv_blk(p)))
    v_ctx = pl.BlockSpec((128, ctx), lambda b, p, t: (kv_blk(p), n_lat // ctx + b))
    v_lat = pl.BlockSpec((128, seq), lambda b, p, t: (kv_blk(p), b))
    return pl.pallas_call(
        functools.partial(_dense_attn_kernel, grp=grp),
        grid=(bsz, n_q // 2, n_tiles + 1),
        in_specs=[pl.BlockSpec((tq, 128), lambda b, p, t: (b * nq + q_blk(t), p)), k_ctx, k_lat, v_ctx, v_lat],
        out_specs=pl.BlockSpec((tq, 128), lambda b, p, t: (b * nq + o_blk(t), p)),
        out_shape=jax.ShapeDtypeStruct((n_lat, q_rows.shape[1]), BF16),
        scratch_shapes=[pltpu.VMEM((ctx + seq, tq), F32), pltpu.VMEM((ctx + seq, tq), F32),
                        pltpu.VMEM((8, tq), F32), pltpu.VMEM((8, tq), F32)],
        compiler_params=_params(("arbitrary", "arbitrary", "arbitrary")),
        name="dense_attn",
    )(q_rows, k_rows, k_rows, v_cols, v_cols)


def _rope_tables(seq, tm):
    n_freq = HEAD_DIM // 4
    inv_freq = ROPE_THETA ** (-jnp.arange(n_freq, dtype=F32) / n_freq)
    rows = seq // GRID_W
    r = jnp.repeat(jnp.arange(rows, dtype=F32), GRID_W)
    col = jnp.tile(jnp.arange(GRID_W, dtype=F32), rows)
    ang = jnp.concatenate([r[:, None] * inv_freq, col[:, None] * inv_freq], -1)
    cos, sin = jnp.cos(ang), jnp.sin(ang)
    cos_t = jnp.concatenate([jnp.tile(cos, (1, 4)), jnp.ones((tm, 128), F32)], axis=0)
    sin_t = jnp.concatenate([jnp.tile(sin, (1, 4)), jnp.zeros((tm, 128), F32)], axis=0)
    return cos_t, sin_t


def _swap_halves(w):
    d, n = w.shape
    t = w.reshape(d, n // HEAD_DIM, 2, HEAD_DIM // 2)
    return jnp.stack([-t[:, :, 1], t[:, :, 0]], axis=2).reshape(d, n)


def kernel(x, c, ctx, c_ctx, ada_w, ada_b, ln_g, ln_b, even_w_in, even_w_out, s5_lam_re, s5_lam_im, s5_log_step,
           s5_b_re, s5_b_im, s5_c_re, s5_c_im, s5_d, s5_w_glu, s5_b_glu, win_sink, odd_w_in, odd_w_out,
           odd_q_norm, odd_k_norm, router_w, router_bias, moe_w_gate, moe_w_up, moe_w_down):
    bsz, seq, d = x.shape
    n_ctx = ctx.shape[1]
    assert ada_w.shape[0] == DEPTH == 2
    rows = _Rows(bsz, seq, n_ctx)
    n_lat, n_all = rows.n_lat, rows.n_all
    s5w = s5_d.shape[1]
    win_q = win_sink.shape[1] * HEAD_DIM
    win_kv = (even_w_in.shape[2] - s5w - win_q) // 2
    odd_q = odd_w_out.shape[1]
    odd_kv = (odd_w_in.shape[2] - odd_q) // 2

    cond = jnp.zeros((16, d), F32).at[:bsz].set(c).at[bsz].set(c_ctx)
    mods = _ada(cond, ada_w, ada_b).reshape(DEPTH, 16, 6, d)
    cos_t, sin_t = _rope_tables(seq, rows.tm)
    rw_hi, rw_lo = _split_bf16(router_w.T)
    x_lat, x_ctx = x.reshape(n_lat, d), ctx.reshape(bsz * n_ctx, d)
    lnv = lambda i, j: (ln_g[i, j].reshape(1, d), ln_b[i, j].reshape(1, d))
    moe_w = lambda i: (i, moe_w_gate, moe_w_up, moe_w_down)

    w = even_w_in[0]
    wu, wq, wk, wv = (w[:, :s5w], w[:, s5w:s5w + win_q], w[:, s5w + win_q:s5w + win_q + win_kv],
                      w[:, s5w + win_q + win_kv:])
    w_cat = jnp.concatenate([wu, wq, _swap_halves(wq), wk, _swap_halves(wk), wv], axis=1).astype(BF16)
    u32, u_tm, q0, k0, v0 = _even_in(rows, x_lat, x_ctx, mods[0], w_cat, cos_t, sin_t, s5w, win_q, win_kv)
    wb, a_b, wc = _s5_weights(s5_lam_re[0], s5_lam_im[0], s5_log_step[0], s5_b_re[0], s5_b_im[0],
                              s5_c_re[0], s5_c_im[0], bsz)
    t_all = seq + n_ctx
    y_tm = _s5(u_tm.reshape(t_all * bsz, s5w), wb, a_b, wc, bsz, n_ctx, seq).reshape(2, t_all, bsz * s5w)
    o_rows = _win_attn(rows, q0, k0, v0, win_sink[0].astype(F32))
    g0, b0 = lnv(0, 0)
    even_args = (u32, y_tm, s5_d[0].reshape(1, s5w), s5_w_glu[0].astype(BF16), s5_b_glu[0].reshape(1, s5w))
    x1, h2, lg = _mix_out(rows, n_all, (x_lat, x_ctx), mods[0], o_rows, even_w_out[0].astype(BF16), g0, b0, rw_hi, rw_lo,
                          even_args)
    g1, b1 = lnv(0, 1)
    x2 = _moe(rows, n_all, x1, h2, lg, mods[0], router_bias.astype(F32), *moe_w(0), g1, b1)

    w = odd_w_in[0]
    wq, wk, wv = w[:, :odd_q], w[:, odd_q:odd_q + odd_kv], w[:, odd_q + odd_kv:]
    w_cat = jnp.concatenate([wq, _swap_halves(wq), wk, _swap_halves(wk), wv], axis=1).astype(BF16)

    def gains(gv, width):
        gs = jnp.concatenate([gv[HEAD_DIM // 2:], gv[:HEAD_DIM // 2]])
        return jnp.stack([jnp.tile(gv, width // HEAD_DIM), jnp.tile(gs, width // HEAD_DIM)]).astype(F32)

    head_mean = jnp.kron(jnp.eye(odd_q // HEAD_DIM, dtype=F32),
                         jnp.full((HEAD_DIM, HEAD_DIM), 1.0 / HEAD_DIM, F32)).astype(BF16)
    q1, k1, v1 = _odd_in(rows, x2, mods[1], w_cat, cos_t, sin_t, gains(odd_q_norm[0], odd_q),
                         gains(odd_k_norm[0], odd_kv), head_mean, odd_q, odd_kv)
    o1 = _dense_attn(q1, k1, v1.T, bsz, seq, n_ctx)
    g0, b0 = lnv(1, 0)
    x1, h2, lg = _mix_out(rows, n_lat, x2, mods[1], o1, odd_w_out[0].astype(BF16), g0, b0, rw_hi, rw_lo)
    g1, b1 = lnv(1, 1)
    out = _moe(rows, n_lat, x1, h2, lg, mods[1], router_bias.astype(F32), *moe_w(1), g1, b1)
    return out.reshape(bsz, seq, d)
```

```python
import functools
import math

import jax
import jax.numpy as jnp
from jax import lax
from jax.experimental import pallas as pl
from jax.experimental.pallas import tpu as pltpu

F32 = jnp.float32
BF16 = jnp.bfloat16

HEAD_DIM = 64
GRID_W = 64
ROPE_THETA = 10000.0
S5_GROUP = 16
S5_STATE = 64
WINDOW = 128
N_EXPERTS = 32
N_GROUPS = 8
EXPERTS_PER_GROUP = N_EXPERTS // N_GROUPS
TOP_K = 2
PAIR_ORDER = ((0, 1), (0, 2), (0, 3), (1, 3), (1, 2), (2, 3))
N_CLASSES = N_GROUPS * len(PAIR_ORDER)
DEPTH = 2
ALPHA = (2 * DEPTH) ** 0.25
LN_EPS = 1e-5
LOG2_E = math.log2(math.e)
RMS_EPS = 1e-6
NEG_INF = -1e30

ROW_TILE = 256
WIDE_ROW_TILE = 512
S5_CHUNK = 64
S5_LANE_GROUP = 512
EXPERT_TILE = 256
RANK_BLOCK = 512
DENSE_TQ = 512
DENSE_KEY_CHUNK = 256
VMEM_LIMIT = 48 * 1024 * 1024


def _params(sem):
    return pltpu.CompilerParams(dimension_semantics=sem, vmem_limit_bytes=VMEM_LIMIT)


def _full(shape):
    n = len(shape)
    return pl.BlockSpec(shape, lambda *_: (0,) * n)


def _ada_kernel(c_ref, w_ref, b_ref, o_ref):
    c = c_ref[...]
    s = c * jax.nn.sigmoid(c)
    o_ref[...] = jnp.dot(s, w_ref[0], preferred_element_type=F32, precision=lax.Precision.HIGHEST) + b_ref[0]


def _ada(cond, ada_w, ada_b):
    g, d = cond.shape
    depth, _, n = ada_w.shape
    bn = 1024
    return pl.pallas_call(
        _ada_kernel,
        grid=(depth, n // bn),
        in_specs=[pl.BlockSpec((g, d), lambda i, j: (0, 0)),
                  pl.BlockSpec((1, d, bn), lambda i, j: (i, 0, j)),
                  pl.BlockSpec((1, 1, bn), lambda i, j: (i, 0, j))],
        out_specs=pl.BlockSpec((None, g, bn), lambda i, j: (i, 0, j)),
        out_shape=jax.ShapeDtypeStruct((depth, g, n), F32),
        compiler_params=_params(("arbitrary", "arbitrary")),
        name="ada",
    )(cond, ada_w, ada_b.reshape(depth, 1, n))


class _Rows:
    def __init__(self, bsz, seq, ctx, tm):
        assert seq % tm == 0 and (bsz * ctx) % tm == 0
        self.bsz, self.seq, self.ctx, self.tm = bsz, seq, ctx, tm
        self.tpb = seq // tm
        self.cpb = ctx // tm
        self.nl = bsz * self.tpb
        self.n_lat = bsz * seq
        self.n_all = bsz * (seq + ctx)

    def two_source(self, width):
        return (pl.BlockSpec((self.tm, width), lambda i: (jnp.minimum(i, self.nl - 1), 0)),
                pl.BlockSpec((self.tm, width), lambda i: (jnp.maximum(i - self.nl, 0), 0)))

    def group(self, i):
        return jnp.where(i < self.nl, i // self.tpb, self.bsz)

    def rope_block(self, i):
        return jnp.where(i < self.nl, i % self.tpb, self.tpb)

    def time_major(self, i):
        assert self.ctx % self.tm == 0
        lat = i < self.nl
        j = i - self.nl
        return (jnp.where(lat, self.cpb + i % self.tpb, j % self.cpb),
                jnp.where(lat, i // self.tpb, j // self.cpb))


def _layer_norm(r, g, b):
    mu = jnp.mean(r, axis=-1, keepdims=True)
    rc = r - mu
    var = jnp.mean(rc * rc, axis=-1, keepdims=True)
    return rc * lax.rsqrt(var + LN_EPS) * g + b


def _even_in_kernel(xl_ref, xc_ref, mod_ref, w_ref, cos_ref, sin_ref, u32_ref, utm_ref, q_ref, k_ref, v_ref, *,
                    s5w, qw, kw, n_lat_tiles):
    m = mod_ref[0]
    x = jnp.where(pl.program_id(0) < n_lat_tiles, xl_ref[...], xc_ref[...])
    h = (x * (1.0 + m[1:2, :]) + m[0:1, :]).astype(BF16)
    r = jnp.dot(h, w_ref[...], preferred_element_type=F32)
    u = r[:, :s5w]
    u32_ref[...] = u
    utm_ref[...] = u.astype(BF16)
    o = s5w
    q, q_sw = r[:, o:o + qw], r[:, o + qw:o + 2 * qw]
    o += 2 * qw
    k, k_sw = r[:, o:o + kw], r[:, o + kw:o + 2 * kw]
    o += 2 * kw
    v = r[:, o:o + kw]
    cos, sin = cos_ref[...], sin_ref[...]
    cq = jnp.concatenate([cos] * (qw // 128), axis=1)
    sq = jnp.concatenate([sin] * (qw // 128), axis=1)
    q_ref[...] = ((q * cq + q_sw * sq) * (HEAD_DIM ** -0.5)).astype(BF16)
    k_ref[...] = (k * cos + k_sw * sin).astype(BF16)
    v_ref[...] = v.astype(BF16)


def _even_in(rows, x_lat, x_ctx, mods, w_cat, cos_t, sin_t, s5w, qw, kw):
    tm, d = rows.tm, x_lat.shape[1]
    n = rows.n_all
    t_all = rows.seq + rows.ctx
    kern = functools.partial(_even_in_kernel, s5w=s5w, qw=qw, kw=kw, n_lat_tiles=rows.nl)
    row = lambda w: pl.BlockSpec((tm, w), lambda i: (i, 0))
    return pl.pallas_call(
        kern,
        grid=(n // tm,),
        in_specs=list(rows.two_source(d)) + [
                  pl.BlockSpec((1, 6, d), lambda i: (rows.group(i), 0, 0)),
                  _full(w_cat.shape),
                  pl.BlockSpec((tm, 128), lambda i: (rows.rope_block(i), 0)),
                  pl.BlockSpec((tm, 128), lambda i: (rows.rope_block(i), 0))],
        out_specs=[row(s5w),
                   pl.BlockSpec((tm, s5w), lambda i: rows.time_major(i)),
                   row(qw), row(kw), row(kw)],
        out_shape=[jax.ShapeDtypeStruct((n, s5w), F32),
                   jax.ShapeDtypeStruct((t_all, rows.bsz * s5w), BF16),
                   jax.ShapeDtypeStruct((n, qw), BF16),
                   jax.ShapeDtypeStruct((n, kw), BF16),
                   jax.ShapeDtypeStruct((n, kw), BF16)],
        compiler_params=_params(("parallel",)),
        name="even_in",
    )(x_lat, x_ctx, mods, w_cat, cos_t, sin_t)


def _s5_kernel(u_ref, wb_ref, a_ref, wc_ref, y_ref, bu_ref, h_ref, *, tc, nb, width, n_state):
    lg = S5_LANE_GROUP
    n_lg = n_state // lg
    kch = width // n_lg
    d = pl.program_id(0)

    @pl.when(pl.program_id(1) == 0)
    def _():
        h_ref[...] = jnp.zeros_like(h_ref)

    u = u_ref[...]
    for g in range(n_lg):
        r = jnp.dot(u[:, kch * g:kch * (g + 1)], wb_ref[0, g], preferred_element_type=F32)
        bu_ref[:, lg * g:lg * (g + 1)] = r[:, :lg]
        bu_ref[:, n_state + lg * g:n_state + lg * (g + 1)] = r[:, lg:]

    rev = d == 1
    for g in range(n_lg):
        re = slice(lg * g, lg * (g + 1))
        im = slice(n_state + lg * g, n_state + lg * (g + 1))
        a_re = a_ref[0, :, re]
        a_im = a_ref[0, :, im]

        def body(i, carry, re=re, im=im, a_re=a_re, a_im=a_im):
            hr, hi = carry
            t = jnp.where(rev, tc - 1 - i, i)
            row = pl.multiple_of(t * nb, nb)
            br = bu_ref[pl.ds(row, nb), re]
            bi = bu_ref[pl.ds(row, nb), im]
            nr = a_re * hr - a_im * hi + br
            ni = a_re * hi + a_im * hr + bi
            bu_ref[pl.ds(row, nb), re] = nr
            bu_ref[pl.ds(row, nb), im] = ni
            return nr, ni

        hr, hi = lax.fori_loop(0, tc, body, (h_ref[:, re], h_ref[:, im]), unroll=4)
        h_ref[:, re] = hr
        h_ref[:, im] = hi

    for g in range(n_lg):
        hc = jnp.concatenate([bu_ref[:, lg * g:lg * (g + 1)],
                              bu_ref[:, n_state + lg * g:n_state + lg * (g + 1)]], axis=1).astype(BF16)
        y_ref[0, :, kch * g:kch * (g + 1)] = jnp.dot(hc, wc_ref[0, g], preferred_element_type=F32)


def _s5(u_tm, wb, a_b, wc, nb, ctx, seq):
    tc = S5_CHUNK
    t_all = ctx + seq
    width = u_tm.shape[1]
    n_state = a_b.shape[2] // 2
    assert ctx % tc == 0 and seq % tc == 0 and nb == 8
    n_ctx, n_chunks = ctx // tc, t_all // tc

    def chunk(d, s):
        back = jnp.where(s < n_ctx, n_ctx - 1 - s, n_chunks - 1 - (s - n_ctx))
        return jnp.where(d == 0, s, back)

    kern = functools.partial(_s5_kernel, tc=tc, nb=nb, width=width, n_state=n_state)
    return pl.pallas_call(
        kern,
        grid=(2, n_chunks),
        in_specs=[pl.BlockSpec((tc * nb, width), lambda d, s: (chunk(d, s), 0)),
                  pl.BlockSpec((1,) + wb.shape[1:], lambda d, s: (d, 0, 0, 0)),
                  pl.BlockSpec((1,) + a_b.shape[1:], lambda d, s: (d, 0, 0)),
                  pl.BlockSpec((1,) + wc.shape[1:], lambda d, s: (d, 0, 0, 0))],
        out_specs=pl.BlockSpec((1, tc * nb, width), lambda d, s: (d, chunk(d, s), 0)),
        out_shape=jax.ShapeDtypeStruct((2, t_all * nb, width), F32),
        scratch_shapes=[pltpu.VMEM((tc * nb, 2 * n_state), F32), pltpu.VMEM((nb, 2 * n_state), F32)],
        compiler_params=_params(("arbitrary", "arbitrary")),
        name="s5_scan",
    )(u_tm, wb, a_b, wc)


def _s5_weights(lam_re, lam_im, log_step, b_re, b_im, c_re, c_im, nb):
    n_dir, n_grp, n_p = lam_re.shape
    ch = b_re.shape[-1]
    lr, li = lam_re.astype(F32), lam_im.astype(F32)
    dt = jnp.exp(log_step.astype(F32))[..., None]
    mag = jnp.exp(lr * dt)
    ab_re, ab_im = mag * jnp.cos(li * dt), mag * jnp.sin(li * dt)
    den = lr * lr + li * li
    num_re, num_im = ab_re - 1.0, ab_im
    f_re = (num_re * lr + num_im * li) / den
    f_im = (num_im * lr - num_re * li) / den
    br, bi = b_re.astype(F32), b_im.astype(F32)
    bb_re = f_re[..., None] * br - f_im[..., None] * bi
    bb_im = f_re[..., None] * bi + f_im[..., None] * br
    n_state = n_grp * n_p
    n_lg = n_state // S5_LANE_GROUP
    gpl = n_grp // n_lg
    eye = jnp.eye(gpl, dtype=F32)

    def pack_in(bb):
        t = bb.reshape(n_dir, n_lg, gpl, n_p, ch)
        m = jnp.einsum('dlgpc,gh->dlgchp', t, eye)
        return m.reshape(n_dir, n_lg, gpl * ch, gpl * n_p)

    def pack_out(c):
        t = c.astype(F32).reshape(n_dir, n_lg, gpl, ch, n_p)
        m = jnp.einsum('dlgcp,gh->dlgphc', t, eye)
        return m.reshape(n_dir, n_lg, gpl * n_p, gpl * ch)

    wb = jnp.concatenate([pack_in(bb_re), pack_in(bb_im)], axis=-1).astype(BF16)
    wc = jnp.concatenate([pack_out(c_re), -pack_out(c_im)], axis=-2).astype(BF16)
    a = jnp.concatenate([ab_re.reshape(n_dir, n_state), ab_im.reshape(n_dir, n_state)], axis=-1)
    a_b = jnp.broadcast_to(a[:, None, :], (n_dir, nb, 2 * n_state))
    return wb, a_b, wc


def _nt_dot(a, b):
    return lax.dot_general(a, b, (((1,), (1,)), ((), ())), preferred_element_type=F32)


def _win_attn_kernel(sink_ref, q_ref, kc_ref, vc_ref, kl_ref, vl_ref, o_ref, *, tq, seq, grp, n_lat_tiles):
    i = pl.program_id(0)
    n_heads = q_ref.shape[1] // HEAD_DIM
    kc, vc = kc_ref[...], vc_ref[...]
    zeros = jnp.zeros((tq, HEAD_DIM), q_ref.dtype)

    def attend(local):
        if local:
            band = tq + 2 * WINDOW
            j = i % (seq // tq)
            start = pl.multiple_of(jnp.clip(j * tq - WINDOW, 0, seq - band), 8)
            kb = kl_ref[pl.ds(start, band), :]
            vb = vl_ref[pl.ds(start, band), :]
            qpos = j * tq + lax.broadcasted_iota(jnp.int32, (tq, band), 0)
            kpos = start + lax.broadcasted_iota(jnp.int32, (tq, band), 1)
            valid = jnp.abs(qpos - kpos) <= WINDOW
        outs = []
        for hh in range(n_heads):
            kv_half = (hh // grp) % 2
            qh = q_ref[:, hh * HEAD_DIM:(hh + 1) * HEAD_DIM]
            q = jnp.concatenate([qh, zeros] if kv_half == 0 else [zeros, qh], axis=1)
            sink = sink_ref[hh]
            s_c = _nt_dot(q, kc)
            m = jnp.maximum(jnp.max(s_c, axis=-1, keepdims=True), sink)
            if local:
                s_l = jnp.where(valid, _nt_dot(q, kb), NEG_INF)
                m = jnp.maximum(m, jnp.max(s_l, axis=-1, keepdims=True))
            p_c = jnp.exp(s_c - m)
            l = jnp.sum(p_c, axis=-1, keepdims=True) + jnp.exp(sink - m)
            acc = jnp.dot(p_c.astype(BF16), vc, preferred_element_type=F32)
            if local:
                p_l = jnp.exp(s_l - m)
                l = l + jnp.sum(p_l, axis=-1, keepdims=True)
                acc = acc + jnp.dot(p_l.astype(BF16), vb, preferred_element_type=F32)
            outs.append((acc / l)[:, kv_half * HEAD_DIM:(kv_half + 1) * HEAD_DIM])
        o_ref[...] = jnp.concatenate(outs, axis=1).astype(o_ref.dtype)

    @pl.when(i < n_lat_tiles)
    def _():
        attend(True)

    @pl.when(i >= n_lat_tiles)
    def _():
        attend(False)


def _win_attn(rows, q_rows, k_rows, v_rows, sink):
    tq, seq, ctx, bsz = rows.tm, rows.seq, rows.ctx, rows.bsz
    assert k_rows.shape[1] == 2 * HEAD_DIM and ctx == tq and seq >= tq + 2 * WINDOW
    grp = q_rows.shape[1] // k_rows.shape[1]
    nq = seq // tq
    n_lat_tiles = bsz * nq
    sample = lambda i: jnp.where(i < n_lat_tiles, i // nq, i - n_lat_tiles)
    kv_ctx = pl.BlockSpec((ctx, 128), lambda i: (rows.n_lat // ctx + sample(i), 0))
    kv_lat = pl.BlockSpec((seq, 128), lambda i: (sample(i), 0))
    qs = pl.BlockSpec((tq, q_rows.shape[1]), lambda i: (i, 0))
    return pl.pallas_call(
        functools.partial(_win_attn_kernel, tq=tq, seq=seq, grp=grp, n_lat_tiles=n_lat_tiles),
        grid=(n_lat_tiles + bsz,),
        in_specs=[pl.BlockSpec(memory_space=pltpu.SMEM), qs, kv_ctx, kv_ctx, kv_lat, kv_lat],
        out_specs=qs,
        out_shape=jax.ShapeDtypeStruct(q_rows.shape, BF16),
        compiler_params=_params(("parallel",)),
        name="win_attn",
    )(sink, q_rows, k_rows, v_rows, k_rows, v_rows)


def _gelu_tanh(y):
    return 0.5 * y * (1.0 + jnp.tanh(math.sqrt(2.0 / math.pi) * (y + 0.044715 * (y * y * y))))


def _split_bf16(v):
    hi = v.astype(BF16)
    return hi, (v - hi.astype(F32)).astype(BF16)


def _mix_out_kernel(*refs, even, n_lat_tiles):
    if even:
        (xl_ref, xc_ref, mod_ref, o_ref, wout_ref, lng_ref, lnb_ref, rwh_ref, rwl_ref,
         u_ref, yf_ref, yb_ref, dsk_ref, wglu_ref, bglu_ref, x1_ref, h2_ref, lg_ref) = refs
        x = jnp.where(pl.program_id(0) < n_lat_tiles, xl_ref[...], xc_ref[...])
        y = u_ref[...] * dsk_ref[...] + yf_ref[0] + yb_ref[0]
        z = _gelu_tanh(y)
        gate = jax.nn.sigmoid(jnp.dot(z.astype(BF16), wglu_ref[...], preferred_element_type=F32) + bglu_ref[...])
        mix = jnp.concatenate([(z * gate).astype(BF16), o_ref[...]], axis=1)
    else:
        (x_ref, mod_ref, o_ref, wout_ref, lng_ref, lnb_ref, rwh_ref, rwl_ref, x1_ref, h2_ref, lg_ref) = refs
        x = x_ref[...]
        mix = o_ref[...]
    m = mod_ref[0]
    ol = jnp.dot(mix, wout_ref[...], preferred_element_type=F32)
    x1 = _layer_norm(ALPHA * x + m[2:3, :] * ol, lng_ref[...], lnb_ref[...])
    x1_ref[...] = x1
    h2 = x1 * (1.0 + m[4:5, :]) + m[3:4, :]
    h2_ref[...] = h2
    hh, hl = _split_bf16(h2)
    rwh, rwl = rwh_ref[...], rwl_ref[...]
    lg_ref[...] = _nt_dot(rwh, hh) + (_nt_dot(rwh, hl) + _nt_dot(rwl, hh))


def _mix_out(rows, n_rows, x_src, mods, o_rows, w_out, ln_g, ln_b, rw_hi, rw_lo, even_args=None):
    even = even_args is not None
    tm, d = rows.tm, w_out.shape[1]
    row = lambda w: pl.BlockSpec((tm, w), lambda i: (i, 0))
    x_ins, x_specs = (list(x_src), list(rows.two_source(d))) if even else ([x_src], [row(d)])
    ins = x_ins + [mods, o_rows, w_out, ln_g, ln_b, rw_hi, rw_lo]
    specs = x_specs + [pl.BlockSpec((1, 6, d), lambda i: (rows.group(i), 0, 0)), row(o_rows.shape[1]),
                       _full(w_out.shape), _full(ln_g.shape), _full(ln_b.shape), _full(rw_hi.shape),
                       _full(rw_lo.shape)]
    if even:
        u32, y_tm, d_skip, w_glu, b_glu = even_args
        s5w = u32.shape[1]
        tmaj = lambda dr: pl.BlockSpec((1, tm, s5w), lambda i: (dr,) + tuple(rows.time_major(i)))
        ins += [u32, y_tm, y_tm, d_skip, w_glu, b_glu]
        specs += [row(s5w), tmaj(0), tmaj(1), _full(d_skip.shape), _full(w_glu.shape), _full(b_glu.shape)]
    n_e = rw_hi.shape[0]
    return pl.pallas_call(
        functools.partial(_mix_out_kernel, even=even, n_lat_tiles=rows.nl),
        grid=(n_rows // tm,),
        in_specs=specs,
        out_specs=[row(d), row(d), pl.BlockSpec((n_e, tm), lambda i: (0, i))],
        out_shape=[jax.ShapeDtypeStruct((n_rows, d), F32), jax.ShapeDtypeStruct((n_rows, d), F32),
                   jax.ShapeDtypeStruct((n_e, n_rows), F32)],
        compiler_params=_params(("parallel",)),
        name="mix_out_even" if even else "mix_out_odd",
    )(*ins)


def _pair_index(lo, hi):
    out = jnp.zeros(lo.shape, jnp.int32)
    for idx, (a, b) in enumerate(PAIR_ORDER):
        out = jnp.where((lo == a) & (hi == b), idx, out)
    return out


def _route_kernel(lg_ref, bias_ref, cls_ref, wlo_ref, whi_ref):
    epg = EXPERTS_PER_GROUP
    sc = [jax.nn.sigmoid(lg_ref[e]) for e in range(N_EXPERTS)]
    sel = [sc[e] + bias_ref[e] for e in range(N_EXPERTS)]
    best, gidx = None, None
    for g in range(N_GROUPS):
        a = sel[g * epg:(g + 1) * epg]
        pair = None
        for i in range(epg):
            for j in range(i + 1, epg):
                s2 = a[i] + a[j]
                pair = s2 if pair is None else jnp.maximum(pair, s2)
        if best is None:
            best, gidx = pair, jnp.zeros(pair.shape, jnp.int32)
        else:
            take = pair > best
            best = jnp.where(take, pair, best)
            gidx = jnp.where(take, g, gidx)
    sg = [sel[i] for i in range(epg)]
    cg = [sc[i] for i in range(epg)]
    for g in range(1, N_GROUPS):
        hit = gidx == g
        sg = [jnp.where(hit, sel[g * epg + i], sg[i]) for i in range(epg)]
        cg = [jnp.where(hit, sc[g * epg + i], cg[i]) for i in range(epg)]

    def first_max(vals, skip=None):
        bv, bi, bs = None, None, None
        for i in range(epg):
            v = vals[i] if skip is None else jnp.where(skip == i, -jnp.inf, vals[i])
            if bv is None:
                bv, bi, bs = v, jnp.zeros(v.shape, jnp.int32), cg[0]
            else:
                take = v > bv
                bv = jnp.where(take, v, bv)
                bi = jnp.where(take, i, bi)
                bs = jnp.where(take, cg[i], bs)
        return bi, bs

    i0, s0 = first_max(sg)
    i1, s1 = first_max(sg, skip=i0)
    tot = s0 + s1
    w0, w1 = s0 / tot, s1 / tot
    first_low = i0 < i1
    lo, hi = jnp.where(first_low, i0, i1), jnp.where(first_low, i1, i0)
    cls_ref[...] = gidx * len(PAIR_ORDER) + _pair_index(lo, hi)
    wlo_ref[...] = jnp.where(first_low, w0, w1)
    whi_ref[...] = jnp.where(first_low, w1, w0)


def _route(logits_t, bias):
    n_e, n = logits_t.shape
    r = n // 128
    rb = 8
    assert n % (128 * rb) == 0
    lg3 = logits_t.reshape(n_e, r, 128)
    blk = pl.BlockSpec((rb, 128), lambda i: (i, 0))
    outs = pl.pallas_call(
        _route_kernel,
        grid=(r // rb,),
        in_specs=[pl.BlockSpec((n_e, rb, 128), lambda i: (0, i, 0)), pl.BlockSpec(memory_space=pltpu.SMEM)],
        out_specs=[blk, blk, blk],
        out_shape=[jax.ShapeDtypeStruct((r, 128), jnp.int32)] + [jax.ShapeDtypeStruct((r, 128), F32)] * 2,
        compiler_params=_params(("parallel",)),
        name="route",
    )(lg3, bias)
    return [o.reshape(n) for o in outs]


def _rank_kernel(c_ref, tri_ref, rank_ref, cnt_ref, carry_ref, *, bw):
    @pl.when(pl.program_id(0) == 0)
    def _():
        carry_ref[...] = jnp.zeros_like(carry_ref)

    hit = lax.broadcasted_iota(jnp.int32, (N_CLASSES, bw), 0) == c_ref[...]
    onehot = jnp.where(hit, 1.0, 0.0).astype(BF16)
    run = jnp.dot(onehot, tri_ref[...], preferred_element_type=F32) + carry_ref[...]
    rank_ref[...] = (jnp.sum(jnp.where(hit, run, 0.0), axis=0, keepdims=True) - 1.0).astype(jnp.int32)
    carry_ref[...] = run[:, bw - 1:bw]
    cnt_ref[...] = run[:, bw - 1:bw]


def _class_ranks(cls):
    n = cls.shape[0]
    bw = RANK_BLOCK
    assert n % bw == 0
    tri = (jnp.arange(bw)[:, None] <= jnp.arange(bw)[None, :]).astype(BF16)
    rank, cnt = pl.pallas_call(
        functools.partial(_rank_kernel, bw=bw),
        grid=(n // bw,),
        in_specs=[pl.BlockSpec((1, bw), lambda i: (0, i)), _full((bw, bw))],
        out_specs=[pl.BlockSpec((1, bw), lambda i: (0, i)), _full((N_CLASSES, 1))],
        out_shape=[jax.ShapeDtypeStruct((1, n), jnp.int32), jax.ShapeDtypeStruct((N_CLASSES, 1), F32)],
        scratch_shapes=[pltpu.VMEM((N_CLASSES, 1), F32)],
        compiler_params=_params(("arbitrary",)),
        name="class_ranks",
    )(cls.reshape(1, n), tri)
    return rank.reshape(n), cnt.reshape(N_CLASSES).astype(jnp.int32)


def _dispatch_plan(cls, n_tok):
    te = EXPERT_TILE
    n_pairs = len(PAIR_ORDER)
    p_max = -(-(n_tok + N_CLASSES * (te - 1)) // te) * te
    rank, counts = _class_ranks(cls)
    padded = (counts + te - 1) // te * te
    pad_end = jnp.cumsum(padded)
    pad_start = pad_end - padded
    classes = jnp.arange(N_CLASSES, dtype=jnp.int32)
    dest = rank + jnp.sum(jnp.where(cls[:, None] == classes[None, :], pad_start[None, :], 0), axis=1)
    tiles = jnp.arange(p_max // te, dtype=jnp.int32) * te
    tile_class = jnp.minimum(jnp.sum(tiles[:, None] >= pad_end[None, :], axis=1), N_CLASSES - 1)
    e_lo = jnp.array([EXPERTS_PER_GROUP * (c // n_pairs) + PAIR_ORDER[c % n_pairs][0] for c in range(N_CLASSES)],
                     jnp.int32)
    e_hi = jnp.array([EXPERTS_PER_GROUP * (c // n_pairs) + PAIR_ORDER[c % n_pairs][1] for c in range(N_CLASSES)],
                     jnp.int32)
    is_c = tile_class[:, None] == classes[None, :]
    tile_lo = jnp.sum(jnp.where(is_c, e_lo[None, :], 0), axis=1).astype(jnp.int32)
    tile_hi = jnp.sum(jnp.where(is_c, e_hi[None, :], 0), axis=1).astype(jnp.int32)
    n_used = (pad_end[-1] // te).astype(jnp.int32).reshape(1)
    return dest, pad_start + counts, pad_end, tile_lo, tile_hi, n_used, p_max


def _dispatch_kernel(lo_ref, hi_ref, nu_ref, d_ref, h_ref, w_ref, xs_ref, stage_ref, zero_ref, sems, zsem, *, tm):
    i = pl.program_id(0)
    slot = i % 2
    d = h_ref.shape[1]
    stage_ref[slot, :, :d] = h_ref[...]
    stage_ref[slot, :, d:] = w_ref[...]

    def issue(r2, _):
        for k in range(2):
            r = 2 * r2 + k
            pltpu.make_async_copy(stage_ref.at[slot, pl.ds(r, 1)], xs_ref.at[pl.ds(d_ref[0, 0, r], 1)],
                                  sems.at[slot]).start(priority=k)
        return 0

    lax.fori_loop(0, tm // 2, issue, 0, unroll=4)

    def wait_rows(s):
        pltpu.make_async_copy(xs_ref.at[pl.ds(0, tm)], xs_ref.at[pl.ds(0, tm)], sems.at[s]).wait()

    @pl.when(i > 0)
    def _():
        wait_rows(1 - slot)

    @pl.when(pl.program_id(0) == pl.num_programs(0) - 1)
    def _():
        zero_ref[...] = jnp.zeros_like(zero_ref)

        def pad_copy(r):
            return pltpu.make_async_copy(zero_ref.at[pl.ds(0, 1)], xs_ref.at[pl.ds(r, 1)], zsem)

        def per_class(c, _):
            lax.fori_loop(lo_ref[c], hi_ref[c], lambda r, a: (pad_copy(r).start(), a)[1], 0)
            lax.fori_loop(lo_ref[c], hi_ref[c], lambda r, a: (pad_copy(r).wait(), a)[1], 0)
            return 0

        lax.fori_loop(0, N_CLASSES, per_class, 0)

        te = zero_ref.shape[0]

        def tail_copy(t):
            return pltpu.make_async_copy(zero_ref, xs_ref.at[pl.ds(pl.multiple_of(t * te, te), te)], zsem)

        n_tiles = xs_ref.shape[0] // te
        lax.fori_loop(nu_ref[0], n_tiles, lambda t, c: (tail_copy(t).start(), c)[1], 0)
        lax.fori_loop(nu_ref[0], n_tiles, lambda t, c: (tail_copy(t).wait(), c)[1], 0)
        wait_rows(slot)


def _dispatch(h2, w_cols, n_rows, dest, pad_lo, pad_hi, n_used, p_max, tm):
    d, wl = h2.shape[1], w_cols.shape[1]
    nt = n_rows // tm
    grid_spec = pltpu.PrefetchScalarGridSpec(
        num_scalar_prefetch=3,
        grid=(nt,),
        in_specs=[pl.BlockSpec((1, 1, tm), lambda i, *_: (i, 0, 0), memory_space=pltpu.SMEM),
                  pl.BlockSpec((tm, d), lambda i, *_: (i, 0)), pl.BlockSpec((tm, wl), lambda i, *_: (i, 0))],
        out_specs=pl.BlockSpec(memory_space=pl.ANY),
        scratch_shapes=[pltpu.VMEM((2, tm, d + wl), h2.dtype), pltpu.VMEM((EXPERT_TILE, d + wl), h2.dtype),
                        pltpu.SemaphoreType.DMA((2,)), pltpu.SemaphoreType.DMA(())],
    )
    return pl.pallas_call(
        functools.partial(_dispatch_kernel, tm=tm),
        grid_spec=grid_spec,
        out_shape=jax.ShapeDtypeStruct((p_max, d + wl), h2.dtype),
        compiler_params=_params(("arbitrary",)),
        name="dispatch",
    )(pad_lo, pad_hi, n_used, dest.reshape(nt, 1, tm), h2, w_cols)


def _expert_kernel(tlo_ref, thi_ref, nu_ref, x_ref, *refs, d, ff):
    w_lo, w_hi = refs[0:3], refs[3:6]
    y_ref = refs[6]
    bf_lo, bf_hi = refs[7:9], refs[9:11]
    i = pl.program_id(0)
    prev = jnp.maximum(i - 1, 0)

    def refresh(t_ref, w_refs, bf_refs):
        @pl.when((i == 0) | (t_ref[i] != t_ref[prev]))
        def _():
            bf_refs[0][:, :ff] = w_refs[0][0, 0].astype(BF16)
            bf_refs[0][:, ff:] = w_refs[1][0, 0].astype(BF16)
            bf_refs[1][...] = w_refs[2][0, 0].astype(BF16)

    refresh(tlo_ref, w_lo, bf_lo)
    refresh(thi_ref, w_hi, bf_hi)

    @pl.when(i < nu_ref[0])
    def _():
        x = x_ref[:, :d].astype(BF16)

        def ffn(bf_refs):
            gu = jnp.dot(x, bf_refs[0][...], preferred_element_type=F32)
            g, u = gu[:, :ff], gu[:, ff:]
            hid = (g * jax.nn.sigmoid(g) * u).astype(BF16)
            return jnp.dot(hid, bf_refs[1][...], preferred_element_type=F32)

        y_ref[...] = x_ref[:, d:d + 1] * ffn(bf_lo) + x_ref[:, d + 1:d + 2] * ffn(bf_hi)

    @pl.when(i >= nu_ref[0])
    def _():
        y_ref[...] = jnp.zeros_like(y_ref)


def _expert_ffn(xs, tile_lo, tile_hi, n_used, layer, w_gate, w_up, w_down):
    p, width = xs.shape
    te = EXPERT_TILE
    d, ff = w_gate.shape[2], w_gate.shape[3]
    lo_w = lambda shape: pl.BlockSpec((1, 1) + shape, lambda i, tl, th, n: (layer, tl[i], 0, 0))
    hi_w = lambda shape: pl.BlockSpec((1, 1) + shape, lambda i, tl, th, n: (layer, th[i], 0, 0))
    grid_spec = pltpu.PrefetchScalarGridSpec(
        num_scalar_prefetch=3,
        grid=(p // te,),
        in_specs=[pl.BlockSpec((te, width), lambda i, tl, th, n: (jnp.minimum(i, n[0] - 1), 0)),
                  lo_w((d, ff)), lo_w((d, ff)), lo_w((ff, d)), hi_w((d, ff)), hi_w((d, ff)), hi_w((ff, d))],
        out_specs=pl.BlockSpec((te, d), lambda i, tl, th, n: (i, 0)),
        scratch_shapes=[pltpu.VMEM((d, 2 * ff), BF16), pltpu.VMEM((ff, d), BF16),
                        pltpu.VMEM((d, 2 * ff), BF16), pltpu.VMEM((ff, d), BF16)],
    )
    return pl.pallas_call(
        functools.partial(_expert_kernel, d=d, ff=ff),
        grid_spec=grid_spec,
        out_shape=jax.ShapeDtypeStruct((p, d), F32),
        compiler_params=_params(("arbitrary",)),
        name="expert_ffn",
    )(tile_lo, tile_hi, n_used, xs, w_gate, w_up, w_down, w_gate, w_up, w_down)


def _moe_out_kernel(d_ref, n_ref, x_ref, mod_ref, ys_ref, lng_ref, lnb_ref, o_ref, y_ref, sems, *, tm):
    i = pl.program_id(0)
    slot = i % 2

    def start_rows(idx_ref, dst):
        def issue(r2, _):
            for k in range(2):
                r = 2 * r2 + k
                pltpu.make_async_copy(ys_ref.at[pl.ds(idx_ref[0, 0, r], 1)], y_ref.at[dst, pl.ds(r, 1)],
                                      sems.at[dst]).start(priority=k)
            return 0

        lax.fori_loop(0, tm // 2, issue, 0, unroll=4)

    @pl.when(i == 0)
    def _():
        start_rows(d_ref, 0)

    @pl.when(i + 1 < pl.num_programs(0))
    def _():
        start_rows(n_ref, 1 - slot)

    pltpu.make_async_copy(ys_ref.at[pl.ds(0, tm)], y_ref.at[slot], sems.at[slot]).wait()
    m = mod_ref[0]
    o_ref[...] = _layer_norm(ALPHA * x_ref[...] + m[5:6, :] * y_ref[slot], lng_ref[...], lnb_ref[...])


def _moe_out(rows, n_rows, x1, mods, ys, dest, ln_g, ln_b):
    tm, d = rows.tm, x1.shape[1]
    nt = n_rows // tm
    row = pl.BlockSpec((tm, d), lambda i: (i, 0))
    idx = pl.BlockSpec((1, 1, tm), lambda i: (i, 0, 0), memory_space=pltpu.SMEM)
    idx_next = pl.BlockSpec((1, 1, tm), lambda i: (jnp.minimum(i + 1, nt - 1), 0, 0), memory_space=pltpu.SMEM)
    d3 = dest.reshape(nt, 1, tm)
    return pl.pallas_call(
        functools.partial(_moe_out_kernel, tm=tm),
        grid=(nt,),
        in_specs=[idx, idx_next, row, pl.BlockSpec((1, 6, d), lambda i: (rows.group(i), 0, 0)),
                  pl.BlockSpec(memory_space=pl.ANY), _full(ln_g.shape), _full(ln_b.shape)],
        out_specs=row,
        out_shape=jax.ShapeDtypeStruct((n_rows, d), F32),
        scratch_shapes=[pltpu.VMEM((2, tm, d), F32), pltpu.SemaphoreType.DMA((2,))],
        compiler_params=_params(("arbitrary",)),
        name="moe_out",
    )(d3, d3, x1, mods, ys, ln_g, ln_b)


def _moe(rows, n_rows, x1, h2, logits_t, mods, router_bias, layer, w_gate, w_up, w_down, ln_g, ln_b):
    cls, w_lo, w_hi = _route(logits_t, router_bias)
    dest, pad_lo, pad_hi, tile_lo, tile_hi, n_used, p_max = _dispatch_plan(cls, n_rows)
    lane = jnp.arange(128, dtype=jnp.int32)[None, :]
    w_cols = jnp.where(lane == 0, w_lo[:, None], jnp.where(lane == 1, w_hi[:, None], 0.0))
    xs = _dispatch(h2, w_cols, n_rows, dest, pad_lo, pad_hi, n_used, p_max, rows.tm)
    ys = _expert_ffn(xs, tile_lo, tile_hi, n_used, layer, w_gate, w_up, w_down)
    return _moe_out(rows, n_rows, x1, mods, ys, dest, ln_g, ln_b)


def _odd_in_kernel(x_ref, mod_ref, w_ref, cos_ref, sin_ref, gq_ref, gk_ref, hm_ref, q_ref, k_ref, v_ref, *, qw, kw):
    m = mod_ref[0]
    h = (x_ref[...] * (1.0 + m[1:2, :]) + m[0:1, :]).astype(BF16)
    r = jnp.dot(h, w_ref[...], preferred_element_type=F32)
    q, q_sw = r[:, :qw], r[:, qw:2 * qw]
    o = 2 * qw
    k, k_sw = r[:, o:o + kw], r[:, o + kw:o + 2 * kw]
    v = r[:, o + 2 * kw:o + 3 * kw]
    cos, sin = cos_ref[...], sin_ref[...]
    hm = hm_ref[...]

    def norm_rope(t, t_sw, gains, width):
        hi, lo = _split_bf16(t * t)
        ms = (jnp.dot(hi, hm[:width, :width], preferred_element_type=F32)
              + jnp.dot(lo, hm[:width, :width], preferred_element_type=F32))
        rs = lax.rsqrt(ms + RMS_EPS)
        c = jnp.concatenate([cos] * (width // 128), axis=1)
        s = jnp.concatenate([sin] * (width // 128), axis=1)
        return (t * gains[0:1, :] * c + t_sw * gains[1:2, :] * s) * rs

    q_ref[...] = (norm_rope(q, q_sw, gq_ref[...], qw) * (HEAD_DIM ** -0.5 * LOG2_E)).astype(BF16)
    k_ref[...] = norm_rope(k, k_sw, gk_ref[...], kw).astype(BF16)
    v_ref[...] = v.astype(BF16)


def _odd_in(rows, x_all, mods, w_cat, cos_t, sin_t, gq, gk, head_mean, qw, kw):
    tm, d = rows.tm, x_all.shape[1]
    n = rows.n_all
    row = lambda w: pl.BlockSpec((tm, w), lambda i: (i, 0))
    tab = pl.BlockSpec((tm, 128), lambda i: (rows.rope_block(i), 0))
    return pl.pallas_call(
        functools.partial(_odd_in_kernel, qw=qw, kw=kw),
        grid=(n // tm,),
        in_specs=[row(d), pl.BlockSpec((1, 6, d), lambda i: (rows.group(i), 0, 0)), _full(w_cat.shape), tab, tab,
                  _full(gq.shape), _full(gk.shape), _full(head_mean.shape)],
        out_specs=[row(qw), row(kw), row(kw)],
        out_shape=[jax.ShapeDtypeStruct((n, qw), BF16), jax.ShapeDtypeStruct((n, kw), BF16),
                   jax.ShapeDtypeStruct((n, kw), BF16)],
        compiler_params=_params(("parallel",)),
        name="odd_in",
    )(x_all, mods, w_cat, cos_t, sin_t, gq, gk, head_mean)


def _dense_attn_kernel(q_ref, kc_ref, kl_ref, vc_ref, vl_ref, o_ref, s0_ref, s1_ref, m0_ref, m1_ref, *,
                       grp, n_tiles):
    half = HEAD_DIM
    ck = DENSE_KEY_CHUNK
    t = pl.program_id(2)
    s_refs, m_refs = (s0_ref, s1_ref), (m0_ref, m1_ref)
    kv_half_of = lambda tile: ((2 * (tile // n_tiles)) // grp) % 2
    kv_new = kv_half_of(jnp.minimum(t, grp * n_tiles - 1))
    kv_old = kv_half_of(jnp.maximum(t - 1, 0))

    @pl.when((pl.program_id(0) == 0) & (pl.program_id(1) == 0) & (t == 0))
    def _():
        for r in s_refs + m_refs:
            r[...] = jnp.zeros_like(r)

    n_ctx = kc_ref.shape[0]
    chunks = [(kc_ref, vc_ref, 0, n_ctx, 0)]
    chunks += [(kl_ref, vl_ref, c, ck, n_ctx + c) for c in range(0, kl_ref.shape[0], ck)]

    def run(new, old):
        qb = q_ref[...]
        sw = jnp.concatenate([qb[:, half:], qb[:, :half]], axis=1)
        lane_half = lax.broadcasted_iota(jnp.int32, qb.shape, 1) // half
        q = jnp.where(lane_half == kv_new, jnp.where(kv_new == new, qb, sw), jnp.zeros_like(qb))
        m_old = m_refs[old][0:1, :]
        m_new, l, o_t = None, None, None
        for k_ref, vt_ref, start, size, col in chunks:
            s_new = _nt_dot(k_ref[start:start + size, :], q)
            s_refs[new][col:col + size, :] = s_new
            mc = jnp.max(s_new, axis=0, keepdims=True)
            m_new = mc if m_new is None else jnp.maximum(m_new, mc)
            p = jnp.exp2(s_refs[old][col:col + size, :] - m_old)
            lc = jnp.sum(p, axis=0, keepdims=True)
            oc = jnp.dot(vt_ref[:, start:start + size], p.astype(BF16), preferred_element_type=F32)
            l, o_t = (lc, oc) if l is None else (l + lc, o_t + oc)
        m_refs[new][...] = jnp.broadcast_to(m_new, m_refs[new].shape)
        o = (o_t / l).T
        o = jnp.where(kv_old == old, o, pltpu.roll(o, half, 1))
        o_ref[:, old * half:(old + 1) * half] = o[:, old * half:(old + 1) * half].astype(o_ref.dtype)

    @pl.when(t % 2 == 0)
    def _():
        run(0, 1)

    @pl.when(t % 2 == 1)
    def _():
        run(1, 0)


def _dense_attn(q_rows, k_rows, v_cols, bsz, seq, ctx):
    n_lat = bsz * seq
    n_q = q_rows.shape[1] // HEAD_DIM
    grp = n_q // (k_rows.shape[1] // HEAD_DIM)
    tq = DENSE_TQ
    nq = seq // tq
    n_tiles = 2 * nq
    n_set = grp * n_tiles
    assert seq % tq == 0 and n_lat % ctx == 0 and grp % 2 == 0 and seq % DENSE_KEY_CHUNK == 0
    assert n_q % (2 * grp) == 0

    def tile_block(b, kb, tile):
        return b * nq + (tile % n_tiles) // 2, kb * grp + tile // n_tiles

    k_ctx = pl.BlockSpec((ctx, 128), lambda b, kb, t: (n_lat // ctx + b, kb))
    k_lat = pl.BlockSpec((seq, 128), lambda b, kb, t: (b, kb))
    v_ctx = pl.BlockSpec((128, ctx), lambda b, kb, t: (kb, n_lat // ctx + b))
    v_lat = pl.BlockSpec((128, seq), lambda b, kb, t: (kb, b))
    return pl.pallas_call(
        functools.partial(_dense_attn_kernel, grp=grp, n_tiles=n_tiles),
        grid=(bsz, n_q // (2 * grp), n_set + 1),
        in_specs=[pl.BlockSpec((tq, 128), lambda b, kb, t: tile_block(b, kb, jnp.minimum(t, n_set - 1))),
                  k_ctx, k_lat, v_ctx, v_lat],
        out_specs=pl.BlockSpec((tq, 128), lambda b, kb, t: tile_block(b, kb, jnp.maximum(t - 1, 0))),
        out_shape=jax.ShapeDtypeStruct((n_lat, q_rows.shape[1]), BF16),
        scratch_shapes=[pltpu.VMEM((ctx + seq, tq), F32), pltpu.VMEM((ctx + seq, tq), F32),
                        pltpu.VMEM((8, tq), F32), pltpu.VMEM((8, tq), F32)],
        compiler_params=_params(("arbitrary", "arbitrary", "arbitrary")),
        name="dense_attn",
    )(q_rows, k_rows, k_rows, v_cols, v_cols)


def _rope_tables(seq, tm):
    n_freq = HEAD_DIM // 4
    inv_freq = ROPE_THETA ** (-jnp.arange(n_freq, dtype=F32) / n_freq)
    rows = seq // GRID_W
    r = jnp.repeat(jnp.arange(rows, dtype=F32), GRID_W)
    col = jnp.tile(jnp.arange(GRID_W, dtype=F32), rows)
    ang = jnp.concatenate([r[:, None] * inv_freq, col[:, None] * inv_freq], -1)
    cos, sin = jnp.cos(ang), jnp.sin(ang)
    cos_t = jnp.concatenate([jnp.tile(cos, (1, 4)), jnp.ones((tm, 128), F32)], axis=0)
    sin_t = jnp.concatenate([jnp.tile(sin, (1, 4)), jnp.zeros((tm, 128), F32)], axis=0)
    return cos_t, sin_t


def _swap_halves(w):
    d, n = w.shape
    t = w.reshape(d, n // HEAD_DIM, 2, HEAD_DIM // 2)
    return jnp.stack([-t[:, :, 1], t[:, :, 0]], axis=2).reshape(d, n)


def kernel(x, c, ctx, c_ctx, ada_w, ada_b, ln_g, ln_b, even_w_in, even_w_out, s5_lam_re, s5_lam_im, s5_log_step,
           s5_b_re, s5_b_im, s5_c_re, s5_c_im, s5_d, s5_w_glu, s5_b_glu, win_sink, odd_w_in, odd_w_out,
           odd_q_norm, odd_k_norm, router_w, router_bias, moe_w_gate, moe_w_up, moe_w_down):
    bsz, seq, d = x.shape
    n_ctx = ctx.shape[1]
    assert ada_w.shape[0] == DEPTH == 2
    rows = _Rows(bsz, seq, n_ctx, ROW_TILE)
    wide = _Rows(bsz, seq, n_ctx, WIDE_ROW_TILE)
    n_lat, n_all = rows.n_lat, rows.n_all
    s5w = s5_d.shape[1]
    win_q = win_sink.shape[1] * HEAD_DIM
    win_kv = (even_w_in.shape[2] - s5w - win_q) // 2
    odd_q = odd_w_out.shape[1]
    odd_kv = (odd_w_in.shape[2] - odd_q) // 2

    cond = jnp.zeros((16, d), F32).at[:bsz].set(c).at[bsz].set(c_ctx)
    mods = _ada(cond, ada_w, ada_b).reshape(DEPTH, 16, 6, d)
    cos_t, sin_t = _rope_tables(seq, wide.tm)
    rw_hi, rw_lo = _split_bf16(router_w.T)
    x_lat, x_ctx = x.reshape(n_lat, d), ctx.reshape(bsz * n_ctx, d)
    lnv = lambda i, j: (ln_g[i, j].reshape(1, d), ln_b[i, j].reshape(1, d))
    moe_w = lambda i: (i, moe_w_gate, moe_w_up, moe_w_down)

    w = even_w_in[0]
    wu, wq, wk, wv = (w[:, :s5w], w[:, s5w:s5w + win_q], w[:, s5w + win_q:s5w + win_q + win_kv],
                      w[:, s5w + win_q + win_kv:])
    w_cat = jnp.concatenate([wu, wq, _swap_halves(wq), wk, _swap_halves(wk), wv], axis=1).astype(BF16)
    u32, u_tm, q0, k0, v0 = _even_in(rows, x_lat, x_ctx, mods[0], w_cat, cos_t, sin_t, s5w, win_q, win_kv)
    wb, a_b, wc = _s5_weights(s5_lam_re[0], s5_lam_im[0], s5_log_step[0], s5_b_re[0], s5_b_im[0],
                              s5_c_re[0], s5_c_im[0], bsz)
    t_all = seq + n_ctx
    y_tm = _s5(u_tm.reshape(t_all * bsz, s5w), wb, a_b, wc, bsz, n_ctx, seq).reshape(2, t_all, bsz * s5w)
    o_rows = _win_attn(rows, q0, k0, v0, win_sink[0].astype(F32))
    g0, b0 = lnv(0, 0)
    even_args = (u32, y_tm, s5_d[0].reshape(1, s5w), s5_w_glu[0].astype(BF16), s5_b_glu[0].reshape(1, s5w))
    x1, h2, lg = _mix_out(rows, n_all, (x_lat, x_ctx), mods[0], o_rows, even_w_out[0].astype(BF16), g0, b0, rw_hi, rw_lo,
                          even_args)
    g1, b1 = lnv(0, 1)
    x2 = _moe(wide, n_all, x1, h2, lg, mods[0], router_bias.astype(F32), *moe_w(0), g1, b1)

    w = odd_w_in[0]
    wq, wk, wv = w[:, :odd_q], w[:, odd_q:odd_q + odd_kv], w[:, odd_q + odd_kv:]
    w_cat = jnp.concatenate([wq, _swap_halves(wq), wk, _swap_halves(wk), wv], axis=1).astype(BF16)

    def gains(gv, width):
        gs = jnp.concatenate([gv[HEAD_DIM // 2:], gv[:HEAD_DIM // 2]])
        return jnp.stack([jnp.tile(gv, width // HEAD_DIM), jnp.tile(gs, width // HEAD_DIM)]).astype(F32)

    head_mean = jnp.kron(jnp.eye(odd_q // HEAD_DIM, dtype=F32),
                         jnp.full((HEAD_DIM, HEAD_DIM), 1.0 / HEAD_DIM, F32)).astype(BF16)
    q1, k1, v1 = _odd_in(wide, x2, mods[1], w_cat, cos_t, sin_t, gains(odd_q_norm[0], odd_q),
                         gains(odd_k_norm[0], odd_kv), head_mean, odd_q, odd_kv)
    o1 = _dense_attn(q1, k1, v1.T, bsz, seq, n_ctx)
    g0, b0 = lnv(1, 0)
    x1, h2, lg = _mix_out(wide, n_lat, x2, mods[1], o1, odd_w_out[0].astype(BF16), g0, b0, rw_hi, rw_lo)
    g1, b1 = lnv(1, 1)
    out = _moe(wide, n_lat, x1, h2, lg, mods[1], router_bias.astype(F32), *moe_w(1), g1, b1)
    return out.reshape(bsz, seq, d)
```

```python
import functools
import math

import jax
import jax.numpy as jnp
from jax import lax
from jax.experimental import pallas as pl
from jax.experimental.pallas import tpu as pltpu

F32 = jnp.float32
BF16 = jnp.bfloat16

HEAD_DIM = 64
GRID_W = 64
ROPE_THETA = 10000.0
S5_GROUP = 16
S5_STATE = 64
WINDOW = 128
N_EXPERTS = 32
N_GROUPS = 8
EXPERTS_PER_GROUP = N_EXPERTS // N_GROUPS
TOP_K = 2
PAIR_ORDER = ((0, 1), (0, 2), (0, 3), (1, 3), (1, 2), (2, 3))
N_CLASSES = N_GROUPS * len(PAIR_ORDER)
DEPTH = 2
ALPHA = (2 * DEPTH) ** 0.25
LN_EPS = 1e-5
LOG2_E = math.log2(math.e)
RMS_EPS = 1e-6
NEG_INF = -1e30

ROW_TILE = 256
WIDE_ROW_TILE = 512
S5_CHUNK = 64
S5_LANE_GROUP = 512
EXPERT_TILE = 256
RANK_BLOCK = 512
DENSE_TQ = 512
DENSE_KEY_CHUNK = 256
VMEM_LIMIT = 48 * 1024 * 1024


def _params(sem):
    return pltpu.CompilerParams(dimension_semantics=sem, vmem_limit_bytes=VMEM_LIMIT)


def _full(shape):
    n = len(shape)
    return pl.BlockSpec(shape, lambda *_: (0,) * n)


def _ada_kernel(c_ref, w_ref, b_ref, o_ref):
    c = c_ref[...]
    s = c * jax.nn.sigmoid(c)
    o_ref[...] = jnp.dot(s, w_ref[0], preferred_element_type=F32, precision=lax.Precision.HIGHEST) + b_ref[0]


def _ada(cond, ada_w, ada_b):
    g, d = cond.shape
    depth, _, n = ada_w.shape
    bn = 1024
    return pl.pallas_call(
        _ada_kernel,
        grid=(depth, n // bn),
        in_specs=[pl.BlockSpec((g, d), lambda i, j: (0, 0)),
                  pl.BlockSpec((1, d, bn), lambda i, j: (i, 0, j)),
                  pl.BlockSpec((1, 1, bn), lambda i, j: (i, 0, j))],
        out_specs=pl.BlockSpec((None, g, bn), lambda i, j: (i, 0, j)),
        out_shape=jax.ShapeDtypeStruct((depth, g, n), F32),
        compiler_params=_params(("arbitrary", "arbitrary")),
        name="ada",
    )(cond, ada_w, ada_b.reshape(depth, 1, n))


class _Rows:
    def __init__(self, bsz, seq, ctx, tm):
        assert seq % tm == 0 and (bsz * ctx) % tm == 0
        self.bsz, self.seq, self.ctx, self.tm = bsz, seq, ctx, tm
        self.tpb = seq // tm
        self.cpb = ctx // tm
        self.nl = bsz * self.tpb
        self.n_lat = bsz * seq
        self.n_all = bsz * (seq + ctx)

    def two_source(self, width):
        return (pl.BlockSpec((self.tm, width), lambda i: (jnp.minimum(i, self.nl - 1), 0)),
                pl.BlockSpec((self.tm, width), lambda i: (jnp.maximum(i - self.nl, 0), 0)))

    def group(self, i):
        return jnp.where(i < self.nl, i // self.tpb, self.bsz)

    def rope_block(self, i):
        return jnp.where(i < self.nl, i % self.tpb, self.tpb)

    def time_major(self, i):
        assert self.ctx % self.tm == 0
        lat = i < self.nl
        j = i - self.nl
        return (jnp.where(lat, self.cpb + i % self.tpb, j % self.cpb),
                jnp.where(lat, i // self.tpb, j // self.cpb))


def _layer_norm(r, g, b):
    mu = jnp.mean(r, axis=-1, keepdims=True)
    rc = r - mu
    var = jnp.mean(rc * rc, axis=-1, keepdims=True)
    return rc * lax.rsqrt(var + LN_EPS) * g + b


def _even_in_kernel(xl_ref, xc_ref, mod_ref, w_ref, cos_ref, sin_ref, u32_ref, utm_ref, q_ref, k_ref, v_ref, *,
                    s5w, qw, kw, n_lat_tiles):
    m = mod_ref[0]
    x = jnp.where(pl.program_id(0) < n_lat_tiles, xl_ref[...], xc_ref[...])
    h = (x * (1.0 + m[1:2, :]) + m[0:1, :]).astype(BF16)
    r = jnp.dot(h, w_ref[...], preferred_element_type=F32)
    u = r[:, :s5w]
    u32_ref[...] = u
    utm_ref[...] = u.astype(BF16)
    o = s5w
    q, q_sw = r[:, o:o + qw], r[:, o + qw:o + 2 * qw]
    o += 2 * qw
    k, k_sw = r[:, o:o + kw], r[:, o + kw:o + 2 * kw]
    o += 2 * kw
    v = r[:, o:o + kw]
    cos, sin = cos_ref[...], sin_ref[...]
    cq = jnp.concatenate([cos] * (qw // 128), axis=1)
    sq = jnp.concatenate([sin] * (qw // 128), axis=1)
    q_ref[...] = ((q * cq + q_sw * sq) * (HEAD_DIM ** -0.5)).astype(BF16)
    k_ref[...] = (k * cos + k_sw * sin).astype(BF16)
    v_ref[...] = v.astype(BF16)


def _even_in(rows, x_lat, x_ctx, mods, w_cat, cos_t, sin_t, s5w, qw, kw):
    tm, d = rows.tm, x_lat.shape[1]
    n = rows.n_all
    t_all = rows.seq + rows.ctx
    kern = functools.partial(_even_in_kernel, s5w=s5w, qw=qw, kw=kw, n_lat_tiles=rows.nl)
    row = lambda w: pl.BlockSpec((tm, w), lambda i: (i, 0))
    return pl.pallas_call(
        kern,
        grid=(n // tm,),
        in_specs=list(rows.two_source(d)) + [
                  pl.BlockSpec((1, 6, d), lambda i: (rows.group(i), 0, 0)),
                  _full(w_cat.shape),
                  pl.BlockSpec((tm, 128), lambda i: (rows.rope_block(i), 0)),
                  pl.BlockSpec((tm, 128), lambda i: (rows.rope_block(i), 0))],
        out_specs=[row(s5w),
                   pl.BlockSpec((tm, s5w), lambda i: rows.time_major(i)),
                   row(qw), row(kw), row(kw)],
        out_shape=[jax.ShapeDtypeStruct((n, s5w), F32),
                   jax.ShapeDtypeStruct((t_all, rows.bsz * s5w), BF16),
                   jax.ShapeDtypeStruct((n, qw), BF16),
                   jax.ShapeDtypeStruct((n, kw), BF16),
                   jax.ShapeDtypeStruct((n, kw), BF16)],
        compiler_params=_params(("parallel",)),
        name="even_in",
    )(x_lat, x_ctx, mods, w_cat, cos_t, sin_t)


def _s5_kernel(u_ref, wb_ref, a_ref, wc_ref, y_ref, bu_ref, h_ref, *, tc, nb, width, n_state):
    lg = S5_LANE_GROUP
    n_lg = n_state // lg
    kch = width // n_lg
    d = pl.program_id(0)

    @pl.when(pl.program_id(1) == 0)
    def _():
        h_ref[...] = jnp.zeros_like(h_ref)

    u = u_ref[...]
    for g in range(n_lg):
        r = jnp.dot(u[:, kch * g:kch * (g + 1)], wb_ref[0, g], preferred_element_type=F32)
        bu_ref[:, lg * g:lg * (g + 1)] = r[:, :lg]
        bu_ref[:, n_state + lg * g:n_state + lg * (g + 1)] = r[:, lg:]

    rev = d == 1
    for g in range(n_lg):
        re = slice(lg * g, lg * (g + 1))
        im = slice(n_state + lg * g, n_state + lg * (g + 1))
        a_re = a_ref[0, :, re]
        a_im = a_ref[0, :, im]

        def body(i, carry, re=re, im=im, a_re=a_re, a_im=a_im):
            hr, hi = carry
            t = jnp.where(rev, tc - 1 - i, i)
            row = pl.multiple_of(t * nb, nb)
            br = bu_ref[pl.ds(row, nb), re]
            bi = bu_ref[pl.ds(row, nb), im]
            nr = a_re * hr - a_im * hi + br
            ni = a_re * hi + a_im * hr + bi
            bu_ref[pl.ds(row, nb), re] = nr
            bu_ref[pl.ds(row, nb), im] = ni
            return nr, ni

        hr, hi = lax.fori_loop(0, tc, body, (h_ref[:, re], h_ref[:, im]), unroll=4)
        h_ref[:, re] = hr
        h_ref[:, im] = hi

    for g in range(n_lg):
        hc = jnp.concatenate([bu_ref[:, lg * g:lg * (g + 1)],
                              bu_ref[:, n_state + lg * g:n_state + lg * (g + 1)]], axis=1).astype(BF16)
        y_ref[0, :, kch * g:kch * (g + 1)] = jnp.dot(hc, wc_ref[0, g], preferred_element_type=F32)


def _s5(u_tm, wb, a_b, wc, nb, ctx, seq):
    tc = S5_CHUNK
    t_all = ctx + seq
    width = u_tm.shape[1]
    n_state = a_b.shape[2] // 2
    assert ctx % tc == 0 and seq % tc == 0 and nb == 8
    n_ctx, n_chunks = ctx // tc, t_all // tc

    def chunk(d, s):
        back = jnp.where(s < n_ctx, n_ctx - 1 - s, n_chunks - 1 - (s - n_ctx))
        return jnp.where(d == 0, s, back)

    kern = functools.partial(_s5_kernel, tc=tc, nb=nb, width=width, n_state=n_state)
    return pl.pallas_call(
        kern,
        grid=(2, n_chunks),
        in_specs=[pl.BlockSpec((tc * nb, width), lambda d, s: (chunk(d, s), 0)),
                  pl.BlockSpec((1,) + wb.shape[1:], lambda d, s: (d, 0, 0, 0)),
                  pl.BlockSpec((1,) + a_b.shape[1:], lambda d, s: (d, 0, 0)),
                  pl.BlockSpec((1,) + wc.shape[1:], lambda d, s: (d, 0, 0, 0))],
        out_specs=pl.BlockSpec((1, tc * nb, width), lambda d, s: (d, chunk(d, s), 0)),
        out_shape=jax.ShapeDtypeStruct((2, t_all * nb, width), F32),
        scratch_shapes=[pltpu.VMEM((tc * nb, 2 * n_state), F32), pltpu.VMEM((nb, 2 * n_state), F32)],
        compiler_params=_params(("arbitrary", "arbitrary")),
        name="s5_scan",
    )(u_tm, wb, a_b, wc)


def _s5_weights(lam_re, lam_im, log_step, b_re, b_im, c_re, c_im, nb):
    n_dir, n_grp, n_p = lam_re.shape
    ch = b_re.shape[-1]
    lr, li = lam_re.astype(F32), lam_im.astype(F32)
    dt = jnp.exp(log_step.astype(F32))[..., None]
    mag = jnp.exp(lr * dt)
    ab_re, ab_im = mag * jnp.cos(li * dt), mag * jnp.sin(li * dt)
    den = lr * lr + li * li
    num_re, num_im = ab_re - 1.0, ab_im
    f_re = (num_re * lr + num_im * li) / den
    f_im = (num_im * lr - num_re * li) / den
    br, bi = b_re.astype(F32), b_im.astype(F32)
    bb_re = f_re[..., None] * br - f_im[..., None] * bi
    bb_im = f_re[..., None] * bi + f_im[..., None] * br
    n_state = n_grp * n_p
    n_lg = n_state // S5_LANE_GROUP
    gpl = n_grp // n_lg
    eye = jnp.eye(gpl, dtype=F32)

    def pack_in(bb):
        t = bb.reshape(n_dir, n_lg, gpl, n_p, ch)
        m = jnp.einsum('dlgpc,gh->dlgchp', t, eye)
        return m.reshape(n_dir, n_lg, gpl * ch, gpl * n_p)

    def pack_out(c):
        t = c.astype(F32).reshape(n_dir, n_lg, gpl, ch, n_p)
        m = jnp.einsum('dlgcp,gh->dlgphc', t, eye)
        return m.reshape(n_dir, n_lg, gpl * n_p, gpl * ch)

    wb = jnp.concatenate([pack_in(bb_re), pack_in(bb_im)], axis=-1).astype(BF16)
    wc = jnp.concatenate([pack_out(c_re), -pack_out(c_im)], axis=-2).astype(BF16)
    a = jnp.concatenate([ab_re.reshape(n_dir, n_state), ab_im.reshape(n_dir, n_state)], axis=-1)
    a_b = jnp.broadcast_to(a[:, None, :], (n_dir, nb, 2 * n_state))
    return wb, a_b, wc


def _nt_dot(a, b):
    return lax.dot_general(a, b, (((1,), (1,)), ((), ())), preferred_element_type=F32)


def _win_attn_kernel(sink_ref, q_ref, kc_ref, vc_ref, kl_ref, vl_ref, o_ref, *, tq, seq, grp, n_lat_tiles):
    i = pl.program_id(0)
    n_heads = q_ref.shape[1] // HEAD_DIM
    kc, vc = kc_ref[...], vc_ref[...]
    zeros = jnp.zeros((tq, HEAD_DIM), q_ref.dtype)

    def attend(local):
        if local:
            band = tq + 2 * WINDOW
            j = i % (seq // tq)
            start = pl.multiple_of(jnp.clip(j * tq - WINDOW, 0, seq - band), 8)
            kb = kl_ref[pl.ds(start, band), :]
            vb = vl_ref[pl.ds(start, band), :]
            qpos = j * tq + lax.broadcasted_iota(jnp.int32, (tq, band), 0)
            kpos = start + lax.broadcasted_iota(jnp.int32, (tq, band), 1)
            valid = jnp.abs(qpos - kpos) <= WINDOW
        outs = []
        for hh in range(n_heads):
            kv_half = (hh // grp) % 2
            qh = q_ref[:, hh * HEAD_DIM:(hh + 1) * HEAD_DIM]
            q = jnp.concatenate([qh, zeros] if kv_half == 0 else [zeros, qh], axis=1)
            sink = sink_ref[hh]
            s_c = _nt_dot(q, kc)
            m = jnp.maximum(jnp.max(s_c, axis=-1, keepdims=True), sink)
            if local:
                s_l = jnp.where(valid, _nt_dot(q, kb), NEG_INF)
                m = jnp.maximum(m, jnp.max(s_l, axis=-1, keepdims=True))
            p_c = jnp.exp(s_c - m)
            l = jnp.sum(p_c, axis=-1, keepdims=True) + jnp.exp(sink - m)
            acc = jnp.dot(p_c.astype(BF16), vc, preferred_element_type=F32)
            if local:
                p_l = jnp.exp(s_l - m)
                l = l + jnp.sum(p_l, axis=-1, keepdims=True)
                acc = acc + jnp.dot(p_l.astype(BF16), vb, preferred_element_type=F32)
            outs.append((acc / l)[:, kv_half * HEAD_DIM:(kv_half + 1) * HEAD_DIM])
        o_ref[...] = jnp.concatenate(outs, axis=1).astype(o_ref.dtype)

    @pl.when(i < n_lat_tiles)
    def _():
        attend(True)

    @pl.when(i >= n_lat_tiles)
    def _():
        attend(False)


def _win_attn(rows, q_rows, k_rows, v_rows, sink):
    tq, seq, ctx, bsz = rows.tm, rows.seq, rows.ctx, rows.bsz
    assert k_rows.shape[1] == 2 * HEAD_DIM and ctx == tq and seq >= tq + 2 * WINDOW
    grp = q_rows.shape[1] // k_rows.shape[1]
    nq = seq // tq
    n_lat_tiles = bsz * nq
    sample = lambda i: jnp.where(i < n_lat_tiles, i // nq, i - n_lat_tiles)
    kv_ctx = pl.BlockSpec((ctx, 128), lambda i: (rows.n_lat // ctx + sample(i), 0))
    kv_lat = pl.BlockSpec((seq, 128), lambda i: (sample(i), 0))
    qs = pl.BlockSpec((tq, q_rows.shape[1]), lambda i: (i, 0))
    return pl.pallas_call(
        functools.partial(_win_attn_kernel, tq=tq, seq=seq, grp=grp, n_lat_tiles=n_lat_tiles),
        grid=(n_lat_tiles + bsz,),
        in_specs=[pl.BlockSpec(memory_space=pltpu.SMEM), qs, kv_ctx, kv_ctx, kv_lat, kv_lat],
        out_specs=qs,
        out_shape=jax.ShapeDtypeStruct(q_rows.shape, BF16),
        compiler_params=_params(("parallel",)),
        name="win_attn",
    )(sink, q_rows, k_rows, v_rows, k_rows, v_rows)


def _gelu_tanh(y):
    return 0.5 * y * (1.0 + jnp.tanh(math.sqrt(2.0 / math.pi) * (y + 0.044715 * (y * y * y))))


def _split_bf16(v):
    hi = v.astype(BF16)
    return hi, (v - hi.astype(F32)).astype(BF16)


def _mix_out_kernel(*refs, even, n_lat_tiles):
    if even:
        (xl_ref, xc_ref, mod_ref, o_ref, wout_ref, lng_ref, lnb_ref, rwh_ref, rwl_ref,
         u_ref, yf_ref, yb_ref, dsk_ref, wglu_ref, bglu_ref, x1_ref, h2_ref, lg_ref) = refs
        x = jnp.where(pl.program_id(0) < n_lat_tiles, xl_ref[...], xc_ref[...])
        y = u_ref[...] * dsk_ref[...] + yf_ref[0] + yb_ref[0]
        z = _gelu_tanh(y)
        gate = jax.nn.sigmoid(jnp.dot(z.astype(BF16), wglu_ref[...], preferred_element_type=F32) + bglu_ref[...])
        mix = jnp.concatenate([(z * gate).astype(BF16), o_ref[...]], axis=1)
    else:
        (x_ref, mod_ref, o_ref, wout_ref, lng_ref, lnb_ref, rwh_ref, rwl_ref, x1_ref, h2_ref, lg_ref) = refs
        x = x_ref[...]
        mix = o_ref[...]
    m = mod_ref[0]
    ol = jnp.dot(mix, wout_ref[...], preferred_element_type=F32)
    x1 = _layer_norm(ALPHA * x + m[2:3, :] * ol, lng_ref[...], lnb_ref[...])
    x1_ref[...] = x1
    h2 = x1 * (1.0 + m[4:5, :]) + m[3:4, :]
    h2_ref[...] = h2
    hh, hl = _split_bf16(h2)
    rwh, rwl = rwh_ref[...], rwl_ref[...]
    lg_ref[...] = _nt_dot(rwh, hh) + (_nt_dot(rwh, hl) + _nt_dot(rwl, hh))


def _mix_out(rows, n_rows, x_src, mods, o_rows, w_out, ln_g, ln_b, rw_hi, rw_lo, even_args=None):
    even = even_args is not None
    tm, d = rows.tm, w_out.shape[1]
    row = lambda w: pl.BlockSpec((tm, w), lambda i: (i, 0))
    x_ins, x_specs = (list(x_src), list(rows.two_source(d))) if even else ([x_src], [row(d)])
    ins = x_ins + [mods, o_rows, w_out, ln_g, ln_b, rw_hi, rw_lo]
    specs = x_specs + [pl.BlockSpec((1, 6, d), lambda i: (rows.group(i), 0, 0)), row(o_rows.shape[1]),
                       _full(w_out.shape), _full(ln_g.shape), _full(ln_b.shape), _full(rw_hi.shape),
                       _full(rw_lo.shape)]
    if even:
        u32, y_tm, d_skip, w_glu, b_glu = even_args
        s5w = u32.shape[1]
        tmaj = lambda dr: pl.BlockSpec((1, tm, s5w), lambda i: (dr,) + tuple(rows.time_major(i)))
        ins += [u32, y_tm, y_tm, d_skip, w_glu, b_glu]
        specs += [row(s5w), tmaj(0), tmaj(1), _full(d_skip.shape), _full(w_glu.shape), _full(b_glu.shape)]
    n_e = rw_hi.shape[0]
    return pl.pallas_call(
        functools.partial(_mix_out_kernel, even=even, n_lat_tiles=rows.nl),
        grid=(n_rows // tm,),
        in_specs=specs,
        out_specs=[row(d), row(d), pl.BlockSpec((n_e, tm), lambda i: (0, i))],
        out_shape=[jax.ShapeDtypeStruct((n_rows, d), F32), jax.ShapeDtypeStruct((n_rows, d), F32),
                   jax.ShapeDtypeStruct((n_e, n_rows), F32)],
        compiler_params=_params(("parallel",)),
        name="mix_out_even" if even else "mix_out_odd",
    )(*ins)


def _pair_index(lo, hi):
    out = jnp.zeros(lo.shape, jnp.int32)
    for idx, (a, b) in enumerate(PAIR_ORDER):
        out = jnp.where((lo == a) & (hi == b), idx, out)
    return out


def _route_kernel(lg_ref, bias_ref, cls_ref, wlo_ref, whi_ref):
    epg = EXPERTS_PER_GROUP
    sc = [jax.nn.sigmoid(lg_ref[e]) for e in range(N_EXPERTS)]
    sel = [sc[e] + bias_ref[e] for e in range(N_EXPERTS)]
    best, gidx = None, None
    for g in range(N_GROUPS):
        a = sel[g * epg:(g + 1) * epg]
        pair = None
        for i in range(epg):
            for j in range(i + 1, epg):
                s2 = a[i] + a[j]
                pair = s2 if pair is None else jnp.maximum(pair, s2)
        if best is None:
            best, gidx = pair, jnp.zeros(pair.shape, jnp.int32)
        else:
            take = pair > best
            best = jnp.where(take, pair, best)
            gidx = jnp.where(take, g, gidx)
    sg = [sel[i] for i in range(epg)]
    cg = [sc[i] for i in range(epg)]
    for g in range(1, N_GROUPS):
        hit = gidx == g
        sg = [jnp.where(hit, sel[g * epg + i], sg[i]) for i in range(epg)]
        cg = [jnp.where(hit, sc[g * epg + i], cg[i]) for i in range(epg)]

    def first_max(vals, skip=None):
        bv, bi, bs = None, None, None
        for i in range(epg):
            v = vals[i] if skip is None else jnp.where(skip == i, -jnp.inf, vals[i])
            if bv is None:
                bv, bi, bs = v, jnp.zeros(v.shape, jnp.int32), cg[0]
            else:
                take = v > bv
                bv = jnp.where(take, v, bv)
                bi = jnp.where(take, i, bi)
                bs = jnp.where(take, cg[i], bs)
        return bi, bs

    i0, s0 = first_max(sg)
    i1, s1 = first_max(sg, skip=i0)
    tot = s0 + s1
    w0, w1 = s0 / tot, s1 / tot
    first_low = i0 < i1
    lo, hi = jnp.where(first_low, i0, i1), jnp.where(first_low, i1, i0)
    cls_ref[...] = gidx * len(PAIR_ORDER) + _pair_index(lo, hi)
    wlo_ref[...] = jnp.where(first_low, w0, w1)
    whi_ref[...] = jnp.where(first_low, w1, w0)


def _route(logits_t, bias):
    n_e, n = logits_t.shape
    r = n // 128
    rb = 8
    assert n % (128 * rb) == 0
    lg3 = logits_t.reshape(n_e, r, 128)
    blk = pl.BlockSpec((rb, 128), lambda i: (i, 0))
    outs = pl.pallas_call(
        _route_kernel,
        grid=(r // rb,),
        in_specs=[pl.BlockSpec((n_e, rb, 128), lambda i: (0, i, 0)), pl.BlockSpec(memory_space=pltpu.SMEM)],
        out_specs=[blk, blk, blk],
        out_shape=[jax.ShapeDtypeStruct((r, 128), jnp.int32)] + [jax.ShapeDtypeStruct((r, 128), F32)] * 2,
        compiler_params=_params(("parallel",)),
        name="route",
    )(lg3, bias)
    return [o.reshape(n) for o in outs]


def _rank_kernel(c_ref, tri_ref, rank_ref, cnt_ref, carry_ref, *, bw):
    @pl.when(pl.program_id(0) == 0)
    def _():
        carry_ref[...] = jnp.zeros_like(carry_ref)

    hit = lax.broadcasted_iota(jnp.int32, (N_CLASSES, bw), 0) == c_ref[...]
    onehot = jnp.where(hit, 1.0, 0.0).astype(BF16)
    run = jnp.dot(onehot, tri_ref[...], preferred_element_type=F32) + carry_ref[...]
    rank_ref[...] = (jnp.sum(jnp.where(hit, run, 0.0), axis=0, keepdims=True) - 1.0).astype(jnp.int32)
    carry_ref[...] = run[:, bw - 1:bw]
    cnt_ref[...] = run[:, bw - 1:bw]


def _class_ranks(cls):
    n = cls.shape[0]
    bw = RANK_BLOCK
    assert n % bw == 0
    tri = (jnp.arange(bw)[:, None] <= jnp.arange(bw)[None, :]).astype(BF16)
    rank, cnt = pl.pallas_call(
        functools.partial(_rank_kernel, bw=bw),
        grid=(n // bw,),
        in_specs=[pl.BlockSpec((1, bw), lambda i: (0, i)), _full((bw, bw))],
        out_specs=[pl.BlockSpec((1, bw), lambda i: (0, i)), _full((N_CLASSES, 1))],
        out_shape=[jax.ShapeDtypeStruct((1, n), jnp.int32), jax.ShapeDtypeStruct((N_CLASSES, 1), F32)],
        scratch_shapes=[pltpu.VMEM((N_CLASSES, 1), F32)],
        compiler_params=_params(("arbitrary",)),
        name="class_ranks",
    )(cls.reshape(1, n), tri)
    return rank.reshape(n), cnt.reshape(N_CLASSES).astype(jnp.int32)


def _dispatch_plan(cls, n_tok):
    te = EXPERT_TILE
    n_pairs = len(PAIR_ORDER)
    p_max = -(-(n_tok + N_CLASSES * (te - 1)) // te) * te
    rank, counts = _class_ranks(cls)
    padded = (counts + te - 1) // te * te
    pad_end = jnp.cumsum(padded)
    pad_start = pad_end - padded
    classes = jnp.arange(N_CLASSES, dtype=jnp.int32)
    dest = rank + jnp.sum(jnp.where(cls[:, None] == classes[None, :], pad_start[None, :], 0), axis=1)
    tiles = jnp.arange(p_max // te, dtype=jnp.int32) * te
    tile_class = jnp.minimum(jnp.sum(tiles[:, None] >= pad_end[None, :], axis=1), N_CLASSES - 1)
    e_lo = jnp.array([EXPERTS_PER_GROUP * (c // n_pairs) + PAIR_ORDER[c % n_pairs][0] for c in range(N_CLASSES)],
                     jnp.int32)
    e_hi = jnp.array([EXPERTS_PER_GROUP * (c // n_pairs) + PAIR_ORDER[c % n_pairs][1] for c in range(N_CLASSES)],
                     jnp.int32)
    is_c = tile_class[:, None] == classes[None, :]
    tile_lo = jnp.sum(jnp.where(is_c, e_lo[None, :], 0), axis=1).astype(jnp.int32)
    tile_hi = jnp.sum(jnp.where(is_c, e_hi[None, :], 0), axis=1).astype(jnp.int32)
    n_used = (pad_end[-1] // te).astype(jnp.int32).reshape(1)
    return dest, pad_start + counts, pad_end, tile_lo, tile_hi, n_used, p_max


def _dispatch_kernel(lo_ref, hi_ref, nu_ref, d_ref, h_ref, w_ref, xs_ref, stage_ref, zero_ref, sems, zsem, *, tm):
    i = pl.program_id(0)
    slot = i % 2
    d = h_ref.shape[1]

    @pl.when(i == 0)
    def _():
        zero_ref[...] = jnp.zeros_like(zero_ref)
        te = zero_ref.shape[0]
        n_tiles = xs_ref.shape[0] // te

        def tile_copy(start):
            return pltpu.make_async_copy(zero_ref, xs_ref.at[pl.ds(pl.multiple_of(start, te), te)], zsem)

        def each_fill(act):
            def per_class(c, a):
                @pl.when(lo_ref[c] < hi_ref[c])
                def _():
                    act(tile_copy(hi_ref[c] - te))
                return a

            lax.fori_loop(0, N_CLASSES, per_class, 0)
            lax.fori_loop(nu_ref[0], n_tiles, lambda t, a: (act(tile_copy(t * te)), a)[1], 0)

        each_fill(lambda cp: cp.start())
        each_fill(lambda cp: cp.wait())

    stage_ref[slot, :, :d] = h_ref[...]
    stage_ref[slot, :, d:] = w_ref[...]

    def issue(r2, _):
        for k in range(2):
            r = 2 * r2 + k
            pltpu.make_async_copy(stage_ref.at[slot, pl.ds(r, 1)], xs_ref.at[pl.ds(d_ref[0, 0, r], 1)],
                                  sems.at[slot]).start(priority=k)
        return 0

    lax.fori_loop(0, tm // 2, issue, 0, unroll=4)

    def wait_rows(s):
        pltpu.make_async_copy(xs_ref.at[pl.ds(0, tm)], xs_ref.at[pl.ds(0, tm)], sems.at[s]).wait()

    @pl.when(i > 0)
    def _():
        wait_rows(1 - slot)

    @pl.when(i == pl.num_programs(0) - 1)
    def _():
        wait_rows(slot)


def _dispatch(h2, w_cols, n_rows, dest, pad_lo, pad_hi, n_used, p_max, tm):
    d, wl = h2.shape[1], w_cols.shape[1]
    nt = n_rows // tm
    grid_spec = pltpu.PrefetchScalarGridSpec(
        num_scalar_prefetch=3,
        grid=(nt,),
        in_specs=[pl.BlockSpec((1, 1, tm), lambda i, *_: (i, 0, 0), memory_space=pltpu.SMEM),
                  pl.BlockSpec((tm, d), lambda i, *_: (i, 0)), pl.BlockSpec((tm, wl), lambda i, *_: (i, 0))],
        out_specs=pl.BlockSpec(memory_space=pl.ANY),
        scratch_shapes=[pltpu.VMEM((2, tm, d + wl), h2.dtype), pltpu.VMEM((EXPERT_TILE, d + wl), h2.dtype),
                        pltpu.SemaphoreType.DMA((2,)), pltpu.SemaphoreType.DMA(())],
    )
    return pl.pallas_call(
        functools.partial(_dispatch_kernel, tm=tm),
        grid_spec=grid_spec,
        out_shape=jax.ShapeDtypeStruct((p_max, d + wl), h2.dtype),
        compiler_params=_params(("arbitrary",)),
        name="dispatch",
    )(pad_lo, pad_hi, n_used, dest.reshape(nt, 1, tm), h2, w_cols)


def _expert_kernel(tlo_ref, thi_ref, nu_ref, x_ref, *refs, d, ff):
    w_lo, w_hi = refs[0:3], refs[3:6]
    y_ref = refs[6]
    bf_lo, bf_hi = refs[7:9], refs[9:11]
    i = pl.program_id(0)
    prev = jnp.maximum(i - 1, 0)

    def refresh(t_ref, w_refs, bf_refs):
        @pl.when((i == 0) | (t_ref[i] != t_ref[prev]))
        def _():
            bf_refs[0][:, :ff] = w_refs[0][0, 0].astype(BF16)
            bf_refs[0][:, ff:] = w_refs[1][0, 0].astype(BF16)
            bf_refs[1][...] = w_refs[2][0, 0].astype(BF16)

    refresh(tlo_ref, w_lo, bf_lo)
    refresh(thi_ref, w_hi, bf_hi)

    @pl.when(i < nu_ref[0])
    def _():
        x = x_ref[:, :d].astype(BF16)

        def ffn(bf_refs):
            gu = jnp.dot(x, bf_refs[0][...], preferred_element_type=F32)
            g, u = gu[:, :ff], gu[:, ff:]
            hid = (g * jax.nn.sigmoid(g) * u).astype(BF16)
            return jnp.dot(hid, bf_refs[1][...], preferred_element_type=F32)

        y_ref[...] = x_ref[:, d:d + 1] * ffn(bf_lo) + x_ref[:, d + 1:d + 2] * ffn(bf_hi)

    @pl.when(i >= nu_ref[0])
    def _():
        y_ref[...] = jnp.zeros_like(y_ref)


def _expert_ffn(xs, tile_lo, tile_hi, n_used, layer, w_gate, w_up, w_down):
    p, width = xs.shape
    te = EXPERT_TILE
    d, ff = w_gate.shape[2], w_gate.shape[3]
    lo_w = lambda shape: pl.BlockSpec((1, 1) + shape, lambda i, tl, th, n: (layer, tl[i], 0, 0))
    hi_w = lambda shape: pl.BlockSpec((1, 1) + shape, lambda i, tl, th, n: (layer, th[i], 0, 0))
    grid_spec = pltpu.PrefetchScalarGridSpec(
        num_scalar_prefetch=3,
        grid=(p // te,),
        in_specs=[pl.BlockSpec((te, width), lambda i, tl, th, n: (jnp.minimum(i, n[0] - 1), 0)),
                  lo_w((d, ff)), lo_w((d, ff)), lo_w((ff, d)), hi_w((d, ff)), hi_w((d, ff)), hi_w((ff, d))],
        out_specs=pl.BlockSpec((te, d), lambda i, tl, th, n: (i, 0)),
        scratch_shapes=[pltpu.VMEM((d, 2 * ff), BF16), pltpu.VMEM((ff, d), BF16),
                        pltpu.VMEM((d, 2 * ff), BF16), pltpu.VMEM((ff, d), BF16)],
    )
    return pl.pallas_call(
        functools.partial(_expert_kernel, d=d, ff=ff),
        grid_spec=grid_spec,
        out_shape=jax.ShapeDtypeStruct((p, d), F32),
        compiler_params=_params(("arbitrary",)),
        name="expert_ffn",
    )(tile_lo, tile_hi, n_used, xs, w_gate, w_up, w_down, w_gate, w_up, w_down)


def _moe_out_kernel(d_ref, n_ref, x_ref, mod_ref, ys_ref, lng_ref, lnb_ref, o_ref, y_ref, sems, *, tm):
    i = pl.program_id(0)
    slot = i % 2

    def start_rows(idx_ref, dst):
        def issue(r2, _):
            for k in range(2):
                r = 2 * r2 + k
                pltpu.make_async_copy(ys_ref.at[pl.ds(idx_ref[0, 0, r], 1)], y_ref.at[dst, pl.ds(r, 1)],
                                      sems.at[dst]).start(priority=k)
            return 0

        lax.fori_loop(0, tm // 2, issue, 0, unroll=4)

    @pl.when(i == 0)
    def _():
        start_rows(d_ref, 0)

    @pl.when(i + 1 < pl.num_programs(0))
    def _():
        start_rows(n_ref, 1 - slot)

    pltpu.make_async_copy(ys_ref.at[pl.ds(0, tm)], y_ref.at[slot], sems.at[slot]).wait()
    m = mod_ref[0]
    o_ref[...] = _layer_norm(ALPHA * x_ref[...] + m[5:6, :] * y_ref[slot], lng_ref[...], lnb_ref[...])


def _moe_out(rows, n_rows, x1, mods, ys, dest, ln_g, ln_b):
    tm, d = rows.tm, x1.shape[1]
    nt = n_rows // tm
    row = pl.BlockSpec((tm, d), lambda i: (i, 0))
    idx = pl.BlockSpec((1, 1, tm), lambda i: (i, 0, 0), memory_space=pltpu.SMEM)
    idx_next = pl.BlockSpec((1, 1, tm), lambda i: (jnp.minimum(i + 1, nt - 1), 0, 0), memory_space=pltpu.SMEM)
    d3 = dest.reshape(nt, 1, tm)
    return pl.pallas_call(
        functools.partial(_moe_out_kernel, tm=tm),
        grid=(nt,),
        in_specs=[idx, idx_next, row, pl.BlockSpec((1, 6, d), lambda i: (rows.group(i), 0, 0)),
                  pl.BlockSpec(memory_space=pl.ANY), _full(ln_g.shape), _full(ln_b.shape)],
        out_specs=row,
        out_shape=jax.ShapeDtypeStruct((n_rows, d), F32),
        scratch_shapes=[pltpu.VMEM((2, tm, d), F32), pltpu.SemaphoreType.DMA((2,))],
        compiler_params=_params(("arbitrary",)),
        name="moe_out",
    )(d3, d3, x1, mods, ys, ln_g, ln_b)


def _moe(rows, n_rows, x1, h2, logits_t, mods, router_bias, layer, w_gate, w_up, w_down, ln_g, ln_b):
    cls, w_lo, w_hi = _route(logits_t, router_bias)
    dest, pad_lo, pad_hi, tile_lo, tile_hi, n_used, p_max = _dispatch_plan(cls, n_rows)
    lane = jnp.arange(128, dtype=jnp.int32)[None, :]
    w_cols = jnp.where(lane == 0, w_lo[:, None], jnp.where(lane == 1, w_hi[:, None], 0.0))
    xs = _dispatch(h2, w_cols, n_rows, dest, pad_lo, pad_hi, n_used, p_max, rows.tm)
    ys = _expert_ffn(xs, tile_lo, tile_hi, n_used, layer, w_gate, w_up, w_down)
    return _moe_out(rows, n_rows, x1, mods, ys, dest, ln_g, ln_b)


def _odd_in_kernel(x_ref, mod_ref, w_ref, cos_ref, sin_ref, gq_ref, gk_ref, hm_ref, q_ref, k_ref, v_ref, *, qw, kw):
    m = mod_ref[0]
    h = (x_ref[...] * (1.0 + m[1:2, :]) + m[0:1, :]).astype(BF16)
    r = jnp.dot(h, w_ref[...], preferred_element_type=F32)
    q, q_sw = r[:, :qw], r[:, qw:2 * qw]
    o = 2 * qw
    k, k_sw = r[:, o:o + kw], r[:, o + kw:o + 2 * kw]
    v = r[:, o + 2 * kw:o + 3 * kw]
    cos, sin = cos_ref[...], sin_ref[...]
    hm = hm_ref[...]

    def norm_rope(t, t_sw, gains, width):
        hi, lo = _split_bf16(t * t)
        ms = (jnp.dot(hi, hm[:width, :width], preferred_element_type=F32)
              + jnp.dot(lo, hm[:width, :width], preferred_element_type=F32))
        rs = lax.rsqrt(ms + RMS_EPS)
        c = jnp.concatenate([cos] * (width // 128), axis=1)
        s = jnp.concatenate([sin] * (width // 128), axis=1)
        return (t * gains[0:1, :] * c + t_sw * gains[1:2, :] * s) * rs

    q_ref[...] = (norm_rope(q, q_sw, gq_ref[...], qw) * (HEAD_DIM ** -0.5 * LOG2_E)).astype(BF16)
    k_ref[...] = norm_rope(k, k_sw, gk_ref[...], kw).astype(BF16)
    v_ref[...] = v.astype(BF16)


def _odd_in(rows, x_all, mods, w_cat, cos_t, sin_t, gq, gk, head_mean, qw, kw):
    tm, d = rows.tm, x_all.shape[1]
    n = rows.n_all
    row = lambda w: pl.BlockSpec((tm, w), lambda i: (i, 0))
    tab = pl.BlockSpec((tm, 128), lambda i: (rows.rope_block(i), 0))
    return pl.pallas_call(
        functools.partial(_odd_in_kernel, qw=qw, kw=kw),
        grid=(n // tm,),
        in_specs=[row(d), pl.BlockSpec((1, 6, d), lambda i: (rows.group(i), 0, 0)), _full(w_cat.shape), tab, tab,
                  _full(gq.shape), _full(gk.shape), _full(head_mean.shape)],
        out_specs=[row(qw), row(kw), row(kw)],
        out_shape=[jax.ShapeDtypeStruct((n, qw), BF16), jax.ShapeDtypeStruct((n, kw), BF16),
                   jax.ShapeDtypeStruct((n, kw), BF16)],
        compiler_params=_params(("parallel",)),
        name="odd_in",
    )(x_all, mods, w_cat, cos_t, sin_t, gq, gk, head_mean)


def _dense_attn_kernel(q_ref, kc_ref, kl_ref, vc_ref, vl_ref, o_ref, s0_ref, s1_ref, m0_ref, m1_ref, *,
                       grp, n_tiles):
    half = HEAD_DIM
    ck = DENSE_KEY_CHUNK
    t = pl.program_id(2)
    s_refs, m_refs = (s0_ref, s1_ref), (m0_ref, m1_ref)
    kv_half_of = lambda tile: ((2 * (tile // n_tiles)) // grp) % 2
    kv_new = kv_half_of(jnp.minimum(t, grp * n_tiles - 1))
    kv_old = kv_half_of(jnp.maximum(t - 1, 0))

    @pl.when((pl.program_id(0) == 0) & (pl.program_id(1) == 0) & (t == 0))
    def _():
        for r in s_refs + m_refs:
            r[...] = jnp.zeros_like(r)

    n_ctx = kc_ref.shape[0]
    chunks = [(kc_ref, vc_ref, 0, n_ctx, 0)]
    chunks += [(kl_ref, vl_ref, c, ck, n_ctx + c) for c in range(0, kl_ref.shape[0], ck)]

    def run(new, old):
        qb = q_ref[...]
        sw = jnp.concatenate([qb[:, half:], qb[:, :half]], axis=1)
        lane_half = lax.broadcasted_iota(jnp.int32, qb.shape, 1) // half
        q = jnp.where(lane_half == kv_new, jnp.where(kv_new == new, qb, sw), jnp.zeros_like(qb))
        m_old = m_refs[old][0:1, :]
        m_new, l, o_t = None, None, None
        for k_ref, vt_ref, start, size, col in chunks:
            s_new = _nt_dot(k_ref[start:start + size, :], q)
            s_refs[new][col:col + size, :] = s_new
            mc = jnp.max(s_new, axis=0, keepdims=True)
            m_new = mc if m_new is None else jnp.maximum(m_new, mc)
            p = jnp.exp2(s_refs[old][col:col + size, :] - m_old)
            lc = jnp.sum(p, axis=0, keepdims=True)
            oc = jnp.dot(vt_ref[:, start:start + size], p.astype(BF16), preferred_element_type=F32)
            l, o_t = (lc, oc) if l is None else (l + lc, o_t + oc)
        m_refs[new][...] = jnp.broadcast_to(m_new, m_refs[new].shape)
        o = (o_t / l).T
        o = jnp.where(kv_old == old, o, pltpu.roll(o, half, 1))
        o_ref[:, old * half:(old + 1) * half] = o[:, old * half:(old + 1) * half].astype(o_ref.dtype)

    @pl.when(t % 2 == 0)
    def _():
        run(0, 1)

    @pl.when(t % 2 == 1)
    def _():
        run(1, 0)


def _dense_attn(q_rows, k_rows, v_cols, bsz, seq, ctx):
    n_lat = bsz * seq
    n_q = q_rows.shape[1] // HEAD_DIM
    grp = n_q // (k_rows.shape[1] // HEAD_DIM)
    tq = DENSE_TQ
    nq = seq // tq
    n_tiles = 2 * nq
    n_set = grp * n_tiles
    assert seq % tq == 0 and n_lat % ctx == 0 and grp % 2 == 0 and seq % DENSE_KEY_CHUNK == 0
    assert n_q % (2 * grp) == 0

    def tile_block(b, kb, tile):
        return b * nq + (tile % n_tiles) // 2, kb * grp + tile // n_tiles

    k_ctx = pl.BlockSpec((ctx, 128), lambda b, kb, t: (n_lat // ctx + b, kb))
    k_lat = pl.BlockSpec((seq, 128), lambda b, kb, t: (b, kb))
    v_ctx = pl.BlockSpec((128, ctx), lambda b, kb, t: (kb, n_lat // ctx + b))
    v_lat = pl.BlockSpec((128, seq), lambda b, kb, t: (kb, b))
    return pl.pallas_call(
        functools.partial(_dense_attn_kernel, grp=grp, n_tiles=n_tiles),
        grid=(bsz, n_q // (2 * grp), n_set + 1),
        in_specs=[pl.BlockSpec((tq, 128), lambda b, kb, t: tile_block(b, kb, jnp.minimum(t, n_set - 1))),
                  k_ctx, k_lat, v_ctx, v_lat],
        out_specs=pl.BlockSpec((tq, 128), lambda b, kb, t: tile_block(b, kb, jnp.maximum(t - 1, 0))),
        out_shape=jax.ShapeDtypeStruct((n_lat, q_rows.shape[1]), BF16),
        scratch_shapes=[pltpu.VMEM((ctx + seq, tq), F32), pltpu.VMEM((ctx + seq, tq), F32),
                        pltpu.VMEM((8, tq), F32), pltpu.VMEM((8, tq), F32)],
        compiler_params=_params(("arbitrary", "arbitrary", "arbitrary")),
        name="dense_attn",
    )(q_rows, k_rows, k_rows, v_cols, v_cols)


def _rope_tables(seq, tm):
    n_freq = HEAD_DIM // 4
    inv_freq = ROPE_THETA ** (-jnp.arange(n_freq, dtype=F32) / n_freq)
    rows = seq // GRID_W
    r = jnp.repeat(jnp.arange(rows, dtype=F32), GRID_W)
    col = jnp.tile(jnp.arange(GRID_W, dtype=F32), rows)
    ang = jnp.concatenate([r[:, None] * inv_freq, col[:, None] * inv_freq], -1)
    cos, sin = jnp.cos(ang), jnp.sin(ang)
    cos_t = jnp.concatenate([jnp.tile(cos, (1, 4)), jnp.ones((tm, 128), F32)], axis=0)
    sin_t = jnp.concatenate([jnp.tile(sin, (1, 4)), jnp.zeros((tm, 128), F32)], axis=0)
    return cos_t, sin_t


def _swap_halves(w):
    d, n = w.shape
    t = w.reshape(d, n // HEAD_DIM, 2, HEAD_DIM // 2)
    return jnp.stack([-t[:, :, 1], t[:, :, 0]], axis=2).reshape(d, n)


def kernel(x, c, ctx, c_ctx, ada_w, ada_b, ln_g, ln_b, even_w_in, even_w_out, s5_lam_re, s5_lam_im, s5_log_step,
           s5_b_re, s5_b_im, s5_c_re, s5_c_im, s5_d, s5_w_glu, s5_b_glu, win_sink, odd_w_in, odd_w_out,
           odd_q_norm, odd_k_norm, router_w, router_bias, moe_w_gate, moe_w_up, moe_w_down):
    bsz, seq, d = x.shape
    n_ctx = ctx.shape[1]
    assert ada_w.shape[0] == DEPTH == 2
    rows = _Rows(bsz, seq, n_ctx, ROW_TILE)
    wide = _Rows(bsz, seq, n_ctx, WIDE_ROW_TILE)
    n_lat, n_all = rows.n_lat, rows.n_all
    s5w = s5_d.shape[1]
    win_q = win_sink.shape[1] * HEAD_DIM
    win_kv = (even_w_in.shape[2] - s5w - win_q) // 2
    odd_q = odd_w_out.shape[1]
    odd_kv = (odd_w_in.shape[2] - odd_q) // 2

    cond = jnp.zeros((16, d), F32).at[:bsz].set(c).at[bsz].set(c_ctx)
    mods = _ada(cond, ada_w, ada_b).reshape(DEPTH, 16, 6, d)
    cos_t, sin_t = _rope_tables(seq, wide.tm)
    rw_hi, rw_lo = _split_bf16(router_w.T)
    x_lat, x_ctx = x.reshape(n_lat, d), ctx.reshape(bsz * n_ctx, d)
    lnv = lambda i, j: (ln_g[i, j].reshape(1, d), ln_b[i, j].reshape(1, d))
    moe_w = lambda i: (i, moe_w_gate, moe_w_up, moe_w_down)

    w = even_w_in[0]
    wu, wq, wk, wv = (w[:, :s5w], w[:, s5w:s5w + win_q], w[:, s5w + win_q:s5w + win_q + win_kv],
                      w[:, s5w + win_q + win_kv:])
    w_cat = jnp.concatenate([wu, wq, _swap_halves(wq), wk, _swap_halves(wk), wv], axis=1).astype(BF16)
    u32, u_tm, q0, k0, v0 = _even_in(rows, x_lat, x_ctx, mods[0], w_cat, cos_t, sin_t, s5w, win_q, win_kv)
    wb, a_b, wc = _s5_weights(s5_lam_re[0], s5_lam_im[0], s5_log_step[0], s5_b_re[0], s5_b_im[0],
                              s5_c_re[0], s5_c_im[0], bsz)
    t_all = seq + n_ctx
    y_tm = _s5(u_tm.reshape(t_all * bsz, s5w), wb, a_b, wc, bsz, n_ctx, seq).reshape(2, t_all, bsz * s5w)
    o_rows = _win_attn(rows, q0, k0, v0, win_sink[0].astype(F32))
    g0, b0 = lnv(0, 0)
    even_args = (u32, y_tm, s5_d[0].reshape(1, s5w), s5_w_glu[0].astype(BF16), s5_b_glu[0].reshape(1, s5w))
    x1, h2, lg = _mix_out(rows, n_all, (x_lat, x_ctx), mods[0], o_rows, even_w_out[0].astype(BF16), g0, b0, rw_hi, rw_lo,
                          even_args)
    g1, b1 = lnv(0, 1)
    x2 = _moe(wide, n_all, x1, h2, lg, mods[0], router_bias.astype(F32), *moe_w(0), g1, b1)

    w = odd_w_in[0]
    wq, wk, wv = w[:, :odd_q], w[:, odd_q:odd_q + odd_kv], w[:, odd_q + odd_kv:]
    w_cat = jnp.concatenate([wq, _swap_halves(wq), wk, _swap_halves(wk), wv], axis=1).astype(BF16)

    def gains(gv, width):
        gs = jnp.concatenate([gv[HEAD_DIM // 2:], gv[:HEAD_DIM // 2]])
        return jnp.stack([jnp.tile(gv, width // HEAD_DIM), jnp.tile(gs, width // HEAD_DIM)]).astype(F32)

    head_mean = jnp.kron(jnp.eye(odd_q // HEAD_DIM, dtype=F32),
                         jnp.full((HEAD_DIM, HEAD_DIM), 1.0 / HEAD_DIM, F32)).astype(BF16)
    q1, k1, v1 = _odd_in(wide, x2, mods[1], w_cat, cos_t, sin_t, gains(odd_q_norm[0], odd_q),
                         gains(odd_k_norm[0], odd_kv), head_mean, odd_q, odd_kv)
    o1 = _dense_attn(q1, k1, v1.T, bsz, seq, n_ctx)
    g0, b0 = lnv(1, 0)
    x1, h2, lg = _mix_out(wide, n_lat, x2, mods[1], o1, odd_w_out[0].astype(BF16), g0, b0, rw_hi, rw_lo)
    g1, b1 = lnv(1, 1)
    out = _moe(wide, n_lat, x1, h2, lg, mods[1], router_bias.astype(F32), *moe_w(1), g1, b1)
    return out.reshape(bsz, seq, d)
```

```python
import functools
import math

import jax
import jax.numpy as jnp
from jax import lax
from jax.experimental import pallas as pl
from jax.experimental.pallas import tpu as pltpu

F32 = jnp.float32
BF16 = jnp.bfloat16

HEAD_DIM = 64
GRID_W = 64
ROPE_THETA = 10000.0
S5_GROUP = 16
S5_STATE = 64
WINDOW = 128
N_EXPERTS = 32
N_GROUPS = 8
EXPERTS_PER_GROUP = N_EXPERTS // N_GROUPS
TOP_K = 2
PAIR_ORDER = ((0, 1), (0, 2), (0, 3), (1, 3), (1, 2), (2, 3))
N_CLASSES = N_GROUPS * len(PAIR_ORDER)
DEPTH = 2
ALPHA = (2 * DEPTH) ** 0.25
LN_EPS = 1e-5
LOG2_E = math.log2(math.e)
RMS_EPS = 1e-6
NEG_INF = -1e30

ROW_TILE = 256
WIDE_ROW_TILE = 512
S5_CHUNK = 128
S5_LANE_GROUP = 512
EXPERT_TILE = 256
RANK_BLOCK = 512
DENSE_TQ = 512
DENSE_KEY_CHUNK = 256
VMEM_LIMIT = 48 * 1024 * 1024


def _params(sem):
    return pltpu.CompilerParams(dimension_semantics=sem, vmem_limit_bytes=VMEM_LIMIT)


def _full(shape):
    n = len(shape)
    return pl.BlockSpec(shape, lambda *_: (0,) * n)


def _ada_kernel(c_ref, w_ref, b_ref, o_ref):
    c = c_ref[...]
    s = c * jax.nn.sigmoid(c)
    o_ref[...] = jnp.dot(s, w_ref[0], preferred_element_type=F32, precision=lax.Precision.HIGHEST) + b_ref[0]


def _ada(cond, ada_w, ada_b):
    g, d = cond.shape
    depth, _, n = ada_w.shape
    bn = 1024
    return pl.pallas_call(
        _ada_kernel,
        grid=(depth, n // bn),
        in_specs=[pl.BlockSpec((g, d), lambda i, j: (0, 0)),
                  pl.BlockSpec((1, d, bn), lambda i, j: (i, 0, j)),
                  pl.BlockSpec((1, 1, bn), lambda i, j: (i, 0, j))],
        out_specs=pl.BlockSpec((None, g, bn), lambda i, j: (i, 0, j)),
        out_shape=jax.ShapeDtypeStruct((depth, g, n), F32),
        compiler_params=_params(("arbitrary", "arbitrary")),
        name="ada",
    )(cond, ada_w, ada_b.reshape(depth, 1, n))


class _Rows:
    def __init__(self, bsz, seq, ctx, tm):
        assert seq % tm == 0 and (bsz * ctx) % tm == 0
        self.bsz, self.seq, self.ctx, self.tm = bsz, seq, ctx, tm
        self.tpb = seq // tm
        self.cpb = ctx // tm
        self.nl = bsz * self.tpb
        self.n_lat = bsz * seq
        self.n_all = bsz * (seq + ctx)

    def two_source(self, width):
        return (pl.BlockSpec((self.tm, width), lambda i: (jnp.minimum(i, self.nl - 1), 0)),
                pl.BlockSpec((self.tm, width), lambda i: (jnp.maximum(i - self.nl, 0), 0)))

    def group(self, i):
        return jnp.where(i < self.nl, i // self.tpb, self.bsz)

    def rope_block(self, i):
        return jnp.where(i < self.nl, i % self.tpb, self.tpb)

    def time_major(self, i):
        assert self.ctx % self.tm == 0
        lat = i < self.nl
        j = i - self.nl
        return (jnp.where(lat, self.cpb + i % self.tpb, j % self.cpb),
                jnp.where(lat, i // self.tpb, j // self.cpb))


def _layer_norm(r, g, b):
    mu = jnp.mean(r, axis=-1, keepdims=True)
    rc = r - mu
    var = jnp.mean(rc * rc, axis=-1, keepdims=True)
    return rc * lax.rsqrt(var + LN_EPS) * g + b


def _even_in_kernel(xl_ref, xc_ref, mod_ref, w_ref, cos_ref, sin_ref, u32_ref, utm_ref, q_ref, k_ref, v_ref, *,
                    s5w, qw, kw, n_lat_tiles):
    m = mod_ref[0]
    x = jnp.where(pl.program_id(0) < n_lat_tiles, xl_ref[...], xc_ref[...])
    h = (x * (1.0 + m[1:2, :]) + m[0:1, :]).astype(BF16)
    r = jnp.dot(h, w_ref[...], preferred_element_type=F32)
    u = r[:, :s5w]
    u32_ref[...] = u
    utm_ref[...] = u.astype(BF16)
    o = s5w
    q, q_sw = r[:, o:o + qw], r[:, o + qw:o + 2 * qw]
    o += 2 * qw
    k, k_sw = r[:, o:o + kw], r[:, o + kw:o + 2 * kw]
    o += 2 * kw
    v = r[:, o:o + kw]
    cos, sin = cos_ref[...], sin_ref[...]
    cq = jnp.concatenate([cos] * (qw // 128), axis=1)
    sq = jnp.concatenate([sin] * (qw // 128), axis=1)
    q_ref[...] = ((q * cq + q_sw * sq) * (HEAD_DIM ** -0.5)).astype(BF16)
    k_ref[...] = (k * cos + k_sw * sin).astype(BF16)
    v_ref[...] = v.astype(BF16)


def _even_in(rows, x_lat, x_ctx, mods, w_cat, cos_t, sin_t, s5w, qw, kw):
    tm, d = rows.tm, x_lat.shape[1]
    n = rows.n_all
    t_all = rows.seq + rows.ctx
    kern = functools.partial(_even_in_kernel, s5w=s5w, qw=qw, kw=kw, n_lat_tiles=rows.nl)
    row = lambda w: pl.BlockSpec((tm, w), lambda i: (i, 0))
    return pl.pallas_call(
        kern,
        grid=(n // tm,),
        in_specs=list(rows.two_source(d)) + [
                  pl.BlockSpec((1, 6, d), lambda i: (rows.group(i), 0, 0)),
                  _full(w_cat.shape),
                  pl.BlockSpec((tm, 128), lambda i: (rows.rope_block(i), 0)),
                  pl.BlockSpec((tm, 128), lambda i: (rows.rope_block(i), 0))],
        out_specs=[row(s5w),
                   pl.BlockSpec((tm, s5w), lambda i: rows.time_major(i)),
                   row(qw), row(kw), row(kw)],
        out_shape=[jax.ShapeDtypeStruct((n, s5w), F32),
                   jax.ShapeDtypeStruct((t_all, rows.bsz * s5w), BF16),
                   jax.ShapeDtypeStruct((n, qw), BF16),
                   jax.ShapeDtypeStruct((n, kw), BF16),
                   jax.ShapeDtypeStruct((n, kw), BF16)],
        compiler_params=_params(("parallel",)),
        name="even_in",
    )(x_lat, x_ctx, mods, w_cat, cos_t, sin_t)


def _s5_kernel(u_ref, wb_ref, a_ref, wc_ref, y_ref, bu_ref, h_ref, *, tc, nb, width, n_state):
    lg = S5_LANE_GROUP
    n_lg = n_state // lg
    kch = width // n_lg
    d = pl.program_id(0)

    @pl.when(pl.program_id(1) == 0)
    def _():
        h_ref[...] = jnp.zeros_like(h_ref)

    u = u_ref[...]
    for g in range(n_lg):
        r = jnp.dot(u[:, kch * g:kch * (g + 1)], wb_ref[0, g], preferred_element_type=F32)
        bu_ref[:, lg * g:lg * (g + 1)] = r[:, :lg]
        bu_ref[:, n_state + lg * g:n_state + lg * (g + 1)] = r[:, lg:]

    rev = d == 1
    for g in range(n_lg):
        re = slice(lg * g, lg * (g + 1))
        im = slice(n_state + lg * g, n_state + lg * (g + 1))
        a_re = a_ref[0, :, re]
        a_im = a_ref[0, :, im]

        def body(i, carry, re=re, im=im, a_re=a_re, a_im=a_im):
            hr, hi = carry
            t = jnp.where(rev, tc - 1 - i, i)
            row = pl.multiple_of(t * nb, nb)
            br = bu_ref[pl.ds(row, nb), re]
            bi = bu_ref[pl.ds(row, nb), im]
            nr = a_re * hr - a_im * hi + br
            ni = a_re * hi + a_im * hr + bi
            bu_ref[pl.ds(row, nb), re] = nr
            bu_ref[pl.ds(row, nb), im] = ni
            return nr, ni

        hr, hi = lax.fori_loop(0, tc, body, (h_ref[:, re], h_ref[:, im]), unroll=4)
        h_ref[:, re] = hr
        h_ref[:, im] = hi

    for g in range(n_lg):
        hc = jnp.concatenate([bu_ref[:, lg * g:lg * (g + 1)],
                              bu_ref[:, n_state + lg * g:n_state + lg * (g + 1)]], axis=1).astype(BF16)
        y_ref[0, :, kch * g:kch * (g + 1)] = jnp.dot(hc, wc_ref[0, g], preferred_element_type=F32)


def _s5(u_tm, wb, a_b, wc, nb, ctx, seq):
    tc = S5_CHUNK
    t_all = ctx + seq
    width = u_tm.shape[1]
    n_state = a_b.shape[2] // 2
    assert ctx % tc == 0 and seq % tc == 0 and nb == 8
    n_ctx, n_chunks = ctx // tc, t_all // tc

    def chunk(d, s):
        back = jnp.where(s < n_ctx, n_ctx - 1 - s, n_chunks - 1 - (s - n_ctx))
        return jnp.where(d == 0, s, back)

    kern = functools.partial(_s5_kernel, tc=tc, nb=nb, width=width, n_state=n_state)
    return pl.pallas_call(
        kern,
        grid=(2, n_chunks),
        in_specs=[pl.BlockSpec((tc * nb, width), lambda d, s: (chunk(d, s), 0)),
                  pl.BlockSpec((1,) + wb.shape[1:], lambda d, s: (d, 0, 0, 0)),
                  pl.BlockSpec((1,) + a_b.shape[1:], lambda d, s: (d, 0, 0)),
                  pl.BlockSpec((1,) + wc.shape[1:], lambda d, s: (d, 0, 0, 0))],
        out_specs=pl.BlockSpec((1, tc * nb, width), lambda d, s: (d, chunk(d, s), 0)),
        out_shape=jax.ShapeDtypeStruct((2, t_all * nb, width), F32),
        scratch_shapes=[pltpu.VMEM((tc * nb, 2 * n_state), F32), pltpu.VMEM((nb, 2 * n_state), F32)],
        compiler_params=_params(("arbitrary", "arbitrary")),
        name="s5_scan",
    )(u_tm, wb, a_b, wc)


def _s5_weights(lam_re, lam_im, log_step, b_re, b_im, c_re, c_im, nb):
    n_dir, n_grp, n_p = lam_re.shape
    ch = b_re.shape[-1]
    lr, li = lam_re.astype(F32), lam_im.astype(F32)
    dt = jnp.exp(log_step.astype(F32))[..., None]
    mag = jnp.exp(lr * dt)
    ab_re, ab_im = mag * jnp.cos(li * dt), mag * jnp.sin(li * dt)
    den = lr * lr + li * li
    num_re, num_im = ab_re - 1.0, ab_im
    f_re = (num_re * lr + num_im * li) / den
    f_im = (num_im * lr - num_re * li) / den
    br, bi = b_re.astype(F32), b_im.astype(F32)
    bb_re = f_re[..., None] * br - f_im[..., None] * bi
    bb_im = f_re[..., None] * bi + f_im[..., None] * br
    n_state = n_grp * n_p
    n_lg = n_state // S5_LANE_GROUP
    gpl = n_grp // n_lg
    eye = jnp.eye(gpl, dtype=F32)

    def pack_in(bb):
        t = bb.reshape(n_dir, n_lg, gpl, n_p, ch)
        m = jnp.einsum('dlgpc,gh->dlgchp', t, eye)
        return m.reshape(n_dir, n_lg, gpl * ch, gpl * n_p)

    def pack_out(c):
        t = c.astype(F32).reshape(n_dir, n_lg, gpl, ch, n_p)
        m = jnp.einsum('dlgcp,gh->dlgphc', t, eye)
        return m.reshape(n_dir, n_lg, gpl * n_p, gpl * ch)

    wb = jnp.concatenate([pack_in(bb_re), pack_in(bb_im)], axis=-1).astype(BF16)
    wc = jnp.concatenate([pack_out(c_re), -pack_out(c_im)], axis=-2).astype(BF16)
    a = jnp.concatenate([ab_re.reshape(n_dir, n_state), ab_im.reshape(n_dir, n_state)], axis=-1)
    a_b = jnp.broadcast_to(a[:, None, :], (n_dir, nb, 2 * n_state))
    return wb, a_b, wc


def _nt_dot(a, b):
    return lax.dot_general(a, b, (((1,), (1,)), ((), ())), preferred_element_type=F32)


def _win_attn_kernel(sink_ref, q_ref, kc_ref, vc_ref, kl_ref, vl_ref, o_ref, *, tq, seq, grp, n_lat_tiles):
    i = pl.program_id(0)
    n_heads = q_ref.shape[1] // HEAD_DIM
    kc, vc = kc_ref[...], vc_ref[...]
    zeros = jnp.zeros((tq, HEAD_DIM), q_ref.dtype)

    def attend(local):
        if local:
            band = tq + 2 * WINDOW
            j = i % (seq // tq)
            start = pl.multiple_of(jnp.clip(j * tq - WINDOW, 0, seq - band), 8)
            kb = kl_ref[pl.ds(start, band), :]
            vb = vl_ref[pl.ds(start, band), :]
            qpos = j * tq + lax.broadcasted_iota(jnp.int32, (tq, band), 0)
            kpos = start + lax.broadcasted_iota(jnp.int32, (tq, band), 1)
            valid = jnp.abs(qpos - kpos) <= WINDOW
        outs = []
        for hh in range(n_heads):
            kv_half = (hh // grp) % 2
            qh = q_ref[:, hh * HEAD_DIM:(hh + 1) * HEAD_DIM]
            q = jnp.concatenate([qh, zeros] if kv_half == 0 else [zeros, qh], axis=1)
            sink = sink_ref[hh]
            s_c = _nt_dot(q, kc)
            m = jnp.maximum(jnp.max(s_c, axis=-1, keepdims=True), sink)
            if local:
                s_l = jnp.where(valid, _nt_dot(q, kb), NEG_INF)
                m = jnp.maximum(m, jnp.max(s_l, axis=-1, keepdims=True))
            p_c = jnp.exp(s_c - m)
            l = jnp.sum(p_c, axis=-1, keepdims=True) + jnp.exp(sink - m)
            acc = jnp.dot(p_c.astype(BF16), vc, preferred_element_type=F32)
            if local:
                p_l = jnp.exp(s_l - m)
                l = l + jnp.sum(p_l, axis=-1, keepdims=True)
                acc = acc + jnp.dot(p_l.astype(BF16), vb, preferred_element_type=F32)
            outs.append((acc / l)[:, kv_half * HEAD_DIM:(kv_half + 1) * HEAD_DIM])
        o_ref[...] = jnp.concatenate(outs, axis=1).astype(o_ref.dtype)

    @pl.when(i < n_lat_tiles)
    def _():
        attend(True)

    @pl.when(i >= n_lat_tiles)
    def _():
        attend(False)


def _win_attn(rows, q_rows, k_rows, v_rows, sink):
    tq, seq, ctx, bsz = rows.tm, rows.seq, rows.ctx, rows.bsz
    assert k_rows.shape[1] == 2 * HEAD_DIM and ctx == tq and seq >= tq + 2 * WINDOW
    grp = q_rows.shape[1] // k_rows.shape[1]
    nq = seq // tq
    n_lat_tiles = bsz * nq
    sample = lambda i: jnp.where(i < n_lat_tiles, i // nq, i - n_lat_tiles)
    kv_ctx = pl.BlockSpec((ctx, 128), lambda i: (rows.n_lat // ctx + sample(i), 0))
    kv_lat = pl.BlockSpec((seq, 128), lambda i: (sample(i), 0))
    qs = pl.BlockSpec((tq, q_rows.shape[1]), lambda i: (i, 0))
    return pl.pallas_call(
        functools.partial(_win_attn_kernel, tq=tq, seq=seq, grp=grp, n_lat_tiles=n_lat_tiles),
        grid=(n_lat_tiles + bsz,),
        in_specs=[pl.BlockSpec(memory_space=pltpu.SMEM), qs, kv_ctx, kv_ctx, kv_lat, kv_lat],
        out_specs=qs,
        out_shape=jax.ShapeDtypeStruct(q_rows.shape, BF16),
        compiler_params=_params(("parallel",)),
        name="win_attn",
    )(sink, q_rows, k_rows, v_rows, k_rows, v_rows)


def _gelu_tanh(y):
    return 0.5 * y * (1.0 + jnp.tanh(math.sqrt(2.0 / math.pi) * (y + 0.044715 * (y * y * y))))


def _split_bf16(v):
    hi = v.astype(BF16)
    return hi, (v - hi.astype(F32)).astype(BF16)


def _mix_out_kernel(*refs, even, n_lat_tiles):
    if even:
        (xl_ref, xc_ref, mod_ref, o_ref, wout_ref, lng_ref, lnb_ref, rwh_ref, rwl_ref,
         u_ref, yf_ref, yb_ref, dsk_ref, wglu_ref, bglu_ref, x1_ref, h2_ref, lg_ref) = refs
        x = jnp.where(pl.program_id(0) < n_lat_tiles, xl_ref[...], xc_ref[...])
        y = u_ref[...] * dsk_ref[...] + yf_ref[0] + yb_ref[0]
        z = _gelu_tanh(y)
        gate = jax.nn.sigmoid(jnp.dot(z.astype(BF16), wglu_ref[...], preferred_element_type=F32) + bglu_ref[...])
        mix = jnp.concatenate([(z * gate).astype(BF16), o_ref[...]], axis=1)
    else:
        (x_ref, mod_ref, o_ref, wout_ref, lng_ref, lnb_ref, rwh_ref, rwl_ref, x1_ref, h2_ref, lg_ref) = refs
        x = x_ref[...]
        mix = o_ref[...]
    m = mod_ref[0]
    ol = jnp.dot(mix, wout_ref[...], preferred_element_type=F32)
    x1 = _layer_norm(ALPHA * x + m[2:3, :] * ol, lng_ref[...], lnb_ref[...])
    x1_ref[...] = x1
    h2 = x1 * (1.0 + m[4:5, :]) + m[3:4, :]
    h2_ref[...] = h2
    hh, hl = _split_bf16(h2)
    rwh, rwl = rwh_ref[...], rwl_ref[...]
    lg_ref[...] = _nt_dot(rwh, hh) + (_nt_dot(rwh, hl) + _nt_dot(rwl, hh))


def _mix_out(rows, n_rows, x_src, mods, o_rows, w_out, ln_g, ln_b, rw_hi, rw_lo, even_args=None):
    even = even_args is not None
    tm, d = rows.tm, w_out.shape[1]
    row = lambda w: pl.BlockSpec((tm, w), lambda i: (i, 0))
    x_ins, x_specs = (list(x_src), list(rows.two_source(d))) if even else ([x_src], [row(d)])
    ins = x_ins + [mods, o_rows, w_out, ln_g, ln_b, rw_hi, rw_lo]
    specs = x_specs + [pl.BlockSpec((1, 6, d), lambda i: (rows.group(i), 0, 0)), row(o_rows.shape[1]),
                       _full(w_out.shape), _full(ln_g.shape), _full(ln_b.shape), _full(rw_hi.shape),
                       _full(rw_lo.shape)]
    if even:
        u32, y_tm, d_skip, w_glu, b_glu = even_args
        s5w = u32.shape[1]
        tmaj = lambda dr: pl.BlockSpec((1, tm, s5w), lambda i: (dr,) + tuple(rows.time_major(i)))
        ins += [u32, y_tm, y_tm, d_skip, w_glu, b_glu]
        specs += [row(s5w), tmaj(0), tmaj(1), _full(d_skip.shape), _full(w_glu.shape), _full(b_glu.shape)]
    n_e = rw_hi.shape[0]
    return pl.pallas_call(
        functools.partial(_mix_out_kernel, even=even, n_lat_tiles=rows.nl),
        grid=(n_rows // tm,),
        in_specs=specs,
        out_specs=[row(d), row(d), pl.BlockSpec((n_e, tm), lambda i: (0, i))],
        out_shape=[jax.ShapeDtypeStruct((n_rows, d), F32), jax.ShapeDtypeStruct((n_rows, d), F32),
                   jax.ShapeDtypeStruct((n_e, n_rows), F32)],
        compiler_params=_params(("parallel",)),
        name="mix_out_even" if even else "mix_out_odd",
    )(*ins)


def _pair_index(lo, hi):
    out = jnp.zeros(lo.shape, jnp.int32)
    for idx, (a, b) in enumerate(PAIR_ORDER):
        out = jnp.where((lo == a) & (hi == b), idx, out)
    return out


def _route_kernel(lg_ref, bias_ref, cls_ref, wlo_ref, whi_ref):
    epg = EXPERTS_PER_GROUP
    sc = [jax.nn.sigmoid(lg_ref[e]) for e in range(N_EXPERTS)]
    sel = [sc[e] + bias_ref[e] for e in range(N_EXPERTS)]
    best, gidx = None, None
    for g in range(N_GROUPS):
        a = sel[g * epg:(g + 1) * epg]
        pair = None
        for i in range(epg):
            for j in range(i + 1, epg):
                s2 = a[i] + a[j]
                pair = s2 if pair is None else jnp.maximum(pair, s2)
        if best is None:
            best, gidx = pair, jnp.zeros(pair.shape, jnp.int32)
        else:
            take = pair > best
            best = jnp.where(take, pair, best)
            gidx = jnp.where(take, g, gidx)
    sg = [sel[i] for i in range(epg)]
    cg = [sc[i] for i in range(epg)]
    for g in range(1, N_GROUPS):
        hit = gidx == g
        sg = [jnp.where(hit, sel[g * epg + i], sg[i]) for i in range(epg)]
        cg = [jnp.where(hit, sc[g * epg + i], cg[i]) for i in range(epg)]

    def first_max(vals, skip=None):
        bv, bi, bs = None, None, None
        for i in range(epg):
            v = vals[i] if skip is None else jnp.where(skip == i, -jnp.inf, vals[i])
            if bv is None:
                bv, bi, bs = v, jnp.zeros(v.shape, jnp.int32), cg[0]
            else:
                take = v > bv
                bv = jnp.where(take, v, bv)
                bi = jnp.where(take, i, bi)
                bs = jnp.where(take, cg[i], bs)
        return bi, bs

    i0, s0 = first_max(sg)
    i1, s1 = first_max(sg, skip=i0)
    tot = s0 + s1
    w0, w1 = s0 / tot, s1 / tot
    first_low = i0 < i1
    lo, hi = jnp.where(first_low, i0, i1), jnp.where(first_low, i1, i0)
    cls_ref[...] = gidx * len(PAIR_ORDER) + _pair_index(lo, hi)
    wlo_ref[...] = jnp.where(first_low, w0, w1)
    whi_ref[...] = jnp.where(first_low, w1, w0)


def _route(logits_t, bias):
    n_e, n = logits_t.shape
    r = n // 128
    rb = 8
    assert n % (128 * rb) == 0
    lg3 = logits_t.reshape(n_e, r, 128)
    blk = pl.BlockSpec((rb, 128), lambda i: (i, 0))
    outs = pl.pallas_call(
        _route_kernel,
        grid=(r // rb,),
        in_specs=[pl.BlockSpec((n_e, rb, 128), lambda i: (0, i, 0)), pl.BlockSpec(memory_space=pltpu.SMEM)],
        out_specs=[blk, blk, blk],
        out_shape=[jax.ShapeDtypeStruct((r, 128), jnp.int32)] + [jax.ShapeDtypeStruct((r, 128), F32)] * 2,
        compiler_params=_params(("parallel",)),
        name="route",
    )(lg3, bias)
    return [o.reshape(n) for o in outs]


def _rank_kernel(c_ref, tri_ref, rank_ref, cnt_ref, carry_ref, *, bw):
    @pl.when(pl.program_id(0) == 0)
    def _():
        carry_ref[...] = jnp.zeros_like(carry_ref)

    hit = lax.broadcasted_iota(jnp.int32, (N_CLASSES, bw), 0) == c_ref[...]
    onehot = jnp.where(hit, 1.0, 0.0).astype(BF16)
    run = jnp.dot(onehot, tri_ref[...], preferred_element_type=F32) + carry_ref[...]
    rank_ref[...] = (jnp.sum(jnp.where(hit, run, 0.0), axis=0, keepdims=True) - 1.0).astype(jnp.int32)
    carry_ref[...] = run[:, bw - 1:bw]
    cnt_ref[...] = run[:, bw - 1:bw]


def _class_ranks(cls):
    n = cls.shape[0]
    bw = RANK_BLOCK
    assert n % bw == 0
    tri = (jnp.arange(bw)[:, None] <= jnp.arange(bw)[None, :]).astype(BF16)
    rank, cnt = pl.pallas_call(
        functools.partial(_rank_kernel, bw=bw),
        grid=(n // bw,),
        in_specs=[pl.BlockSpec((1, bw), lambda i: (0, i)), _full((bw, bw))],
        out_specs=[pl.BlockSpec((1, bw), lambda i: (0, i)), _full((N_CLASSES, 1))],
        out_shape=[jax.ShapeDtypeStruct((1, n), jnp.int32), jax.ShapeDtypeStruct((N_CLASSES, 1), F32)],
        scratch_shapes=[pltpu.VMEM((N_CLASSES, 1), F32)],
        compiler_params=_params(("arbitrary",)),
        name="class_ranks",
    )(cls.reshape(1, n), tri)
    return rank.reshape(n), cnt.reshape(N_CLASSES).astype(jnp.int32)


def _dispatch_plan(cls, n_tok):
    te = EXPERT_TILE
    n_pairs = len(PAIR_ORDER)
    p_max = -(-(n_tok + N_CLASSES * (te - 1)) // te) * te
    rank, counts = _class_ranks(cls)
    padded = (counts + te - 1) // te * te
    pad_end = jnp.cumsum(padded)
    pad_start = pad_end - padded
    classes = jnp.arange(N_CLASSES, dtype=jnp.int32)
    dest = rank + jnp.sum(jnp.where(cls[:, None] == classes[None, :], pad_start[None, :], 0), axis=1)
    tiles = jnp.arange(p_max // te, dtype=jnp.int32) * te
    tile_class = jnp.minimum(jnp.sum(tiles[:, None] >= pad_end[None, :], axis=1), N_CLASSES - 1)
    e_lo = jnp.array([EXPERTS_PER_GROUP * (c // n_pairs) + PAIR_ORDER[c % n_pairs][0] for c in range(N_CLASSES)],
                     jnp.int32)
    e_hi = jnp.array([EXPERTS_PER_GROUP * (c // n_pairs) + PAIR_ORDER[c % n_pairs][1] for c in range(N_CLASSES)],
                     jnp.int32)
    is_c = tile_class[:, None] == classes[None, :]
    tile_lo = jnp.sum(jnp.where(is_c, e_lo[None, :], 0), axis=1).astype(jnp.int32)
    tile_hi = jnp.sum(jnp.where(is_c, e_hi[None, :], 0), axis=1).astype(jnp.int32)
    n_used = (pad_end[-1] // te).astype(jnp.int32).reshape(1)
    return dest, pad_start + counts, pad_end, tile_lo, tile_hi, n_used, p_max


def _dispatch_kernel(lo_ref, hi_ref, nu_ref, d_ref, h_ref, w_ref, xs_ref, stage_ref, zero_ref, sems, zsem, *, tm):
    i = pl.program_id(0)
    slot = i % 2
    d = h_ref.shape[1]

    @pl.when(i == 0)
    def _():
        zero_ref[...] = jnp.zeros_like(zero_ref)
        te = zero_ref.shape[0]
        n_tiles = xs_ref.shape[0] // te

        def tile_copy(start):
            return pltpu.make_async_copy(zero_ref, xs_ref.at[pl.ds(pl.multiple_of(start, te), te)], zsem)

        def each_fill(act):
            def per_class(c, a):
                @pl.when(lo_ref[c] < hi_ref[c])
                def _():
                    act(tile_copy(hi_ref[c] - te))
                return a

            lax.fori_loop(0, N_CLASSES, per_class, 0)
            lax.fori_loop(nu_ref[0], n_tiles, lambda t, a: (act(tile_copy(t * te)), a)[1], 0)

        each_fill(lambda cp: cp.start())
        each_fill(lambda cp: cp.wait())

    stage_ref[slot, :, :d] = h_ref[...]
    stage_ref[slot, :, d:] = w_ref[...]

    def issue(r2, _):
        for k in range(2):
            r = 2 * r2 + k
            pltpu.make_async_copy(stage_ref.at[slot, pl.ds(r, 1)], xs_ref.at[pl.ds(d_ref[0, 0, r], 1)],
                                  sems.at[slot]).start(priority=k)
        return 0

    lax.fori_loop(0, tm // 2, issue, 0, unroll=4)

    def wait_rows(s):
        pltpu.make_async_copy(xs_ref.at[pl.ds(0, tm)], xs_ref.at[pl.ds(0, tm)], sems.at[s]).wait()

    @pl.when(i > 0)
    def _():
        wait_rows(1 - slot)

    @pl.when(i == pl.num_programs(0) - 1)
    def _():
        wait_rows(slot)


def _dispatch(h2, w_cols, n_rows, dest, pad_lo, pad_hi, n_used, p_max, tm):
    d, wl = h2.shape[1], w_cols.shape[1]
    nt = n_rows // tm
    grid_spec = pltpu.PrefetchScalarGridSpec(
        num_scalar_prefetch=3,
        grid=(nt,),
        in_specs=[pl.BlockSpec((1, 1, tm), lambda i, *_: (i, 0, 0), memory_space=pltpu.SMEM),
                  pl.BlockSpec((tm, d), lambda i, *_: (i, 0)), pl.BlockSpec((tm, wl), lambda i, *_: (i, 0))],
        out_specs=pl.BlockSpec(memory_space=pl.ANY),
        scratch_shapes=[pltpu.VMEM((2, tm, d + wl), h2.dtype), pltpu.VMEM((EXPERT_TILE, d + wl), h2.dtype),
                        pltpu.SemaphoreType.DMA((2,)), pltpu.SemaphoreType.DMA(())],
    )
    return pl.pallas_call(
        functools.partial(_dispatch_kernel, tm=tm),
        grid_spec=grid_spec,
        out_shape=jax.ShapeDtypeStruct((p_max, d + wl), h2.dtype),
        compiler_params=_params(("arbitrary",)),
        name="dispatch",
    )(pad_lo, pad_hi, n_used, dest.reshape(nt, 1, tm), h2, w_cols)


def _expert_kernel(tlo_ref, thi_ref, nu_ref, x_ref, *refs, d, ff):
    w_lo, w_hi = refs[0:3], refs[3:6]
    y_ref = refs[6]
    bf_lo, bf_hi = refs[7:9], refs[9:11]
    i = pl.program_id(0)
    prev = jnp.maximum(i - 1, 0)

    def refresh(t_ref, w_refs, bf_refs):
        @pl.when((i == 0) | (t_ref[i] != t_ref[prev]))
        def _():
            bf_refs[0][:, :ff] = w_refs[0][0, 0].astype(BF16)
            bf_refs[0][:, ff:] = w_refs[1][0, 0].astype(BF16)
            bf_refs[1][...] = w_refs[2][0, 0].astype(BF16)

    refresh(tlo_ref, w_lo, bf_lo)
    refresh(thi_ref, w_hi, bf_hi)

    @pl.when(i < nu_ref[0])
    def _():
        x = x_ref[:, :d].astype(BF16)

        def ffn(bf_refs):
            gu = jnp.dot(x, bf_refs[0][...], preferred_element_type=F32)
            g, u = gu[:, :ff], gu[:, ff:]
            hid = (g * jax.nn.sigmoid(g) * u).astype(BF16)
            return jnp.dot(hid, bf_refs[1][...], preferred_element_type=F32)

        y_ref[...] = x_ref[:, d:d + 1] * ffn(bf_lo) + x_ref[:, d + 1:d + 2] * ffn(bf_hi)

    @pl.when(i >= nu_ref[0])
    def _():
        y_ref[...] = jnp.zeros_like(y_ref)


def _expert_ffn(xs, tile_lo, tile_hi, n_used, layer, w_gate, w_up, w_down):
    p, width = xs.shape
    te = EXPERT_TILE
    d, ff = w_gate.shape[2], w_gate.shape[3]
    lo_w = lambda shape: pl.BlockSpec((1, 1) + shape, lambda i, tl, th, n: (layer, tl[i], 0, 0))
    hi_w = lambda shape: pl.BlockSpec((1, 1) + shape, lambda i, tl, th, n: (layer, th[i], 0, 0))
    grid_spec = pltpu.PrefetchScalarGridSpec(
        num_scalar_prefetch=3,
        grid=(p // te,),
        in_specs=[pl.BlockSpec((te, width), lambda i, tl, th, n: (jnp.minimum(i, n[0] - 1), 0)),
                  lo_w((d, ff)), lo_w((d, ff)), lo_w((ff, d)), hi_w((d, ff)), hi_w((d, ff)), hi_w((ff, d))],
        out_specs=pl.BlockSpec((te, d), lambda i, tl, th, n: (i, 0)),
        scratch_shapes=[pltpu.VMEM((d, 2 * ff), BF16), pltpu.VMEM((ff, d), BF16),
                        pltpu.VMEM((d, 2 * ff), BF16), pltpu.VMEM((ff, d), BF16)],
    )
    return pl.pallas_call(
        functools.partial(_expert_kernel, d=d, ff=ff),
        grid_spec=grid_spec,
        out_shape=jax.ShapeDtypeStruct((p, d), F32),
        compiler_params=_params(("arbitrary",)),
        name="expert_ffn",
    )(tile_lo, tile_hi, n_used, xs, w_gate, w_up, w_down, w_gate, w_up, w_down)


def _moe_out_kernel(d_ref, n_ref, x_ref, mod_ref, ys_ref, lng_ref, lnb_ref, o_ref, y_ref, sems, *, tm):
    i = pl.program_id(0)
    slot = i % 2

    def start_rows(idx_ref, dst):
        def issue(r2, _):
            for k in range(2):
                r = 2 * r2 + k
                pltpu.make_async_copy(ys_ref.at[pl.ds(idx_ref[0, 0, r], 1)], y_ref.at[dst, pl.ds(r, 1)],
                                      sems.at[dst]).start(priority=k)
            return 0

        lax.fori_loop(0, tm // 2, issue, 0, unroll=4)

    @pl.when(i == 0)
    def _():
        start_rows(d_ref, 0)

    @pl.when(i + 1 < pl.num_programs(0))
    def _():
        start_rows(n_ref, 1 - slot)

    pltpu.make_async_copy(ys_ref.at[pl.ds(0, tm)], y_ref.at[slot], sems.at[slot]).wait()
    m = mod_ref[0]
    o_ref[...] = _layer_norm(ALPHA * x_ref[...] + m[5:6, :] * y_ref[slot], lng_ref[...], lnb_ref[...])


def _moe_out(rows, n_rows, x1, mods, ys, dest, ln_g, ln_b):
    tm, d = rows.tm, x1.shape[1]
    nt = n_rows // tm
    row = pl.BlockSpec((tm, d), lambda i: (i, 0))
    idx = pl.BlockSpec((1, 1, tm), lambda i: (i, 0, 0), memory_space=pltpu.SMEM)
    idx_next = pl.BlockSpec((1, 1, tm), lambda i: (jnp.minimum(i + 1, nt - 1), 0, 0), memory_space=pltpu.SMEM)
    d3 = dest.reshape(nt, 1, tm)
    return pl.pallas_call(
        functools.partial(_moe_out_kernel, tm=tm),
        grid=(nt,),
        in_specs=[idx, idx_next, row, pl.BlockSpec((1, 6, d), lambda i: (rows.group(i), 0, 0)),
                  pl.BlockSpec(memory_space=pl.ANY), _full(ln_g.shape), _full(ln_b.shape)],
        out_specs=row,
        out_shape=jax.ShapeDtypeStruct((n_rows, d), F32),
        scratch_shapes=[pltpu.VMEM((2, tm, d), F32), pltpu.SemaphoreType.DMA((2,))],
        compiler_params=_params(("arbitrary",)),
        name="moe_out",
    )(d3, d3, x1, mods, ys, ln_g, ln_b)


def _moe(rows, n_rows, x1, h2, logits_t, mods, router_bias, layer, w_gate, w_up, w_down, ln_g, ln_b):
    cls, w_lo, w_hi = _route(logits_t, router_bias)
    dest, pad_lo, pad_hi, tile_lo, tile_hi, n_used, p_max = _dispatch_plan(cls, n_rows)
    lane = jnp.arange(128, dtype=jnp.int32)[None, :]
    w_cols = jnp.where(lane == 0, w_lo[:, None], jnp.where(lane == 1, w_hi[:, None], 0.0))
    xs = _dispatch(h2, w_cols, n_rows, dest, pad_lo, pad_hi, n_used, p_max, rows.tm)
    ys = _expert_ffn(xs, tile_lo, tile_hi, n_used, layer, w_gate, w_up, w_down)
    return _moe_out(rows, n_rows, x1, mods, ys, dest, ln_g, ln_b)


def _odd_in_kernel(x_ref, mod_ref, w_ref, cos_ref, sin_ref, gq_ref, gk_ref, hm_ref, q_ref, k_ref, v_ref, *, qw, kw):
    m = mod_ref[0]
    h = (x_ref[...] * (1.0 + m[1:2, :]) + m[0:1, :]).astype(BF16)
    r = jnp.dot(h, w_ref[...], preferred_element_type=F32)
    q, q_sw = r[:, :qw], r[:, qw:2 * qw]
    o = 2 * qw
    k, k_sw = r[:, o:o + kw], r[:, o + kw:o + 2 * kw]
    v = r[:, o + 2 * kw:o + 3 * kw]
    cos, sin = cos_ref[...], sin_ref[...]
    hm = hm_ref[...]

    def norm_rope(t, t_sw, gains, width):
        hi, lo = _split_bf16(t * t)
        ms = (jnp.dot(hi, hm[:width, :width], preferred_element_type=F32)
              + jnp.dot(lo, hm[:width, :width], preferred_element_type=F32))
        rs = lax.rsqrt(ms + RMS_EPS)
        c = jnp.concatenate([cos] * (width // 128), axis=1)
        s = jnp.concatenate([sin] * (width // 128), axis=1)
        return (t * gains[0:1, :] * c + t_sw * gains[1:2, :] * s) * rs

    q_ref[...] = (norm_rope(q, q_sw, gq_ref[...], qw) * (HEAD_DIM ** -0.5 * LOG2_E)).astype(BF16)
    k_ref[...] = norm_rope(k, k_sw, gk_ref[...], kw).astype(BF16)
    v_ref[...] = v.astype(BF16)


def _odd_in(rows, x_all, mods, w_cat, cos_t, sin_t, gq, gk, head_mean, qw, kw):
    tm, d = rows.tm, x_all.shape[1]
    n = rows.n_all
    row = lambda w: pl.BlockSpec((tm, w), lambda i: (i, 0))
    tab = pl.BlockSpec((tm, 128), lambda i: (rows.rope_block(i), 0))
    return pl.pallas_call(
        functools.partial(_odd_in_kernel, qw=qw, kw=kw),
        grid=(n // tm,),
        in_specs=[row(d), pl.BlockSpec((1, 6, d), lambda i: (rows.group(i), 0, 0)), _full(w_cat.shape), tab, tab,
                  _full(gq.shape), _full(gk.shape), _full(head_mean.shape)],
        out_specs=[row(qw), row(kw), row(kw)],
        out_shape=[jax.ShapeDtypeStruct((n, qw), BF16), jax.ShapeDtypeStruct((n, kw), BF16),
                   jax.ShapeDtypeStruct((n, kw), BF16)],
        compiler_params=_params(("parallel",)),
        name="odd_in",
    )(x_all, mods, w_cat, cos_t, sin_t, gq, gk, head_mean)


def _dense_attn_kernel(q_ref, kc_ref, kl_ref, vc_ref, vl_ref, o_ref, s0_ref, s1_ref, m0_ref, m1_ref, *,
                       grp, n_tiles):
    half = HEAD_DIM
    ck = DENSE_KEY_CHUNK
    t = pl.program_id(2)
    s_refs, m_refs = (s0_ref, s1_ref), (m0_ref, m1_ref)
    kv_half_of = lambda tile: ((2 * (tile // n_tiles)) // grp) % 2
    kv_new = kv_half_of(jnp.minimum(t, grp * n_tiles - 1))
    kv_old = kv_half_of(jnp.maximum(t - 1, 0))

    @pl.when((pl.program_id(0) == 0) & (pl.program_id(1) == 0) & (t == 0))
    def _():
        for r in s_refs + m_refs:
            r[...] = jnp.zeros_like(r)

    n_ctx = kc_ref.shape[0]
    chunks = [(kc_ref, vc_ref, 0, n_ctx, 0)]
    chunks += [(kl_ref, vl_ref, c, ck, n_ctx + c) for c in range(0, kl_ref.shape[0], ck)]

    def run(new, old):
        qb = q_ref[...]
        sw = jnp.concatenate([qb[:, half:], qb[:, :half]], axis=1)
        lane_half = lax.broadcasted_iota(jnp.int32, qb.shape, 1) // half
        q = jnp.where(lane_half == kv_new, jnp.where(kv_new == new, qb, sw), jnp.zeros_like(qb))
        m_old = m_refs[old][0:1, :]
        m_new, l, o_t = None, None, None
        for k_ref, vt_ref, start, size, col in chunks:
            s_new = _nt_dot(k_ref[start:start + size, :], q)
            s_refs[new][col:col + size, :] = s_new
            mc = jnp.max(s_new, axis=0, keepdims=True)
            m_new = mc if m_new is None else jnp.maximum(m_new, mc)
            p = jnp.exp2(s_refs[old][col:col + size, :] - m_old)
            lc = jnp.sum(p, axis=0, keepdims=True)
            oc = jnp.dot(vt_ref[:, start:start + size], p.astype(BF16), preferred_element_type=F32)
            l, o_t = (lc, oc) if l is None else (l + lc, o_t + oc)
        m_refs[new][...] = jnp.broadcast_to(m_new, m_refs[new].shape)
        o = (o_t / l).T
        o = jnp.where(kv_old == old, o, pltpu.roll(o, half, 1))
        o_ref[:, old * half:(old + 1) * half] = o[:, old * half:(old + 1) * half].astype(o_ref.dtype)

    @pl.when(t % 2 == 0)
    def _():
        run(0, 1)

    @pl.when(t % 2 == 1)
    def _():
        run(1, 0)


def _dense_attn(q_rows, k_rows, v_cols, bsz, seq, ctx):
    n_lat = bsz * seq
    n_q = q_rows.shape[1] // HEAD_DIM
    grp = n_q // (k_rows.shape[1] // HEAD_DIM)
    tq = DENSE_TQ
    nq = seq // tq
    n_tiles = 2 * nq
    n_set = grp * n_tiles
    assert seq % tq == 0 and n_lat % ctx == 0 and grp % 2 == 0 and seq % DENSE_KEY_CHUNK == 0
    assert n_q % (2 * grp) == 0

    def tile_block(b, kb, tile):
        return b * nq + (tile % n_tiles) // 2, kb * grp + tile // n_tiles

    k_ctx = pl.BlockSpec((ctx, 128), lambda b, kb, t: (n_lat // ctx + b, kb))
    k_lat = pl.BlockSpec((seq, 128), lambda b, kb, t: (b, kb))
    v_ctx = pl.BlockSpec((128, ctx), lambda b, kb, t: (kb, n_lat // ctx + b))
    v_lat = pl.BlockSpec((128, seq), lambda b, kb, t: (kb, b))
    return pl.pallas_call(
        functools.partial(_dense_attn_kernel, grp=grp, n_tiles=n_tiles),
        grid=(bsz, n_q // (2 * grp), n_set + 1),
        in_specs=[pl.BlockSpec((tq, 128), lambda b, kb, t: tile_block(b, kb, jnp.minimum(t, n_set - 1))),
                  k_ctx, k_lat, v_ctx, v_lat],
        out_specs=pl.BlockSpec((tq, 128), lambda b, kb, t: tile_block(b, kb, jnp.maximum(t - 1, 0))),
        out_shape=jax.ShapeDtypeStruct((n_lat, q_rows.shape[1]), BF16),
        scratch_shapes=[pltpu.VMEM((ctx + seq, tq), F32), pltpu.VMEM((ctx + seq, tq), F32),
                        pltpu.VMEM((8, tq), F32), pltpu.VMEM((8, tq), F32)],
        compiler_params=_params(("arbitrary", "arbitrary", "arbitrary")),
        name="dense_attn",
    )(q_rows, k_rows, k_rows, v_cols, v_cols)


def _rope_tables(seq, tm):
    n_freq = HEAD_DIM // 4
    inv_freq = ROPE_THETA ** (-jnp.arange(n_freq, dtype=F32) / n_freq)
    rows = seq // GRID_W
    r = jnp.repeat(jnp.arange(rows, dtype=F32), GRID_W)
    col = jnp.tile(jnp.arange(GRID_W, dtype=F32), rows)
    ang = jnp.concatenate([r[:, None] * inv_freq, col[:, None] * inv_freq], -1)
    cos, sin = jnp.cos(ang), jnp.sin(ang)
    cos_t = jnp.concatenate([jnp.tile(cos, (1, 4)), jnp.ones((tm, 128), F32)], axis=0)
    sin_t = jnp.concatenate([jnp.tile(sin, (1, 4)), jnp.zeros((tm, 128), F32)], axis=0)
    return cos_t, sin_t


def _swap_halves(w):
    d, n = w.shape
    t = w.reshape(d, n // HEAD_DIM, 2, HEAD_DIM // 2)
    return jnp.stack([-t[:, :, 1], t[:, :, 0]], axis=2).reshape(d, n)


def kernel(x, c, ctx, c_ctx, ada_w, ada_b, ln_g, ln_b, even_w_in, even_w_out, s5_lam_re, s5_lam_im, s5_log_step,
           s5_b_re, s5_b_im, s5_c_re, s5_c_im, s5_d, s5_w_glu, s5_b_glu, win_sink, odd_w_in, odd_w_out,
           odd_q_norm, odd_k_norm, router_w, router_bias, moe_w_gate, moe_w_up, moe_w_down):
    bsz, seq, d = x.shape
    n_ctx = ctx.shape[1]
    assert ada_w.shape[0] == DEPTH == 2
    rows = _Rows(bsz, seq, n_ctx, ROW_TILE)
    wide = _Rows(bsz, seq, n_ctx, WIDE_ROW_TILE)
    n_lat, n_all = rows.n_lat, rows.n_all
    s5w = s5_d.shape[1]
    win_q = win_sink.shape[1] * HEAD_DIM
    win_kv = (even_w_in.shape[2] - s5w - win_q) // 2
    odd_q = odd_w_out.shape[1]
    odd_kv = (odd_w_in.shape[2] - odd_q) // 2

    cond = jnp.zeros((16, d), F32).at[:bsz].set(c).at[bsz].set(c_ctx)
    mods = _ada(cond, ada_w, ada_b).reshape(DEPTH, 16, 6, d)
    cos_t, sin_t = _rope_tables(seq, wide.tm)
    rw_hi, rw_lo = _split_bf16(router_w.T)
    x_lat, x_ctx = x.reshape(n_lat, d), ctx.reshape(bsz * n_ctx, d)
    lnv = lambda i, j: (ln_g[i, j].reshape(1, d), ln_b[i, j].reshape(1, d))
    moe_w = lambda i: (i, moe_w_gate, moe_w_up, moe_w_down)

    w = even_w_in[0]
    wu, wq, wk, wv = (w[:, :s5w], w[:, s5w:s5w + win_q], w[:, s5w + win_q:s5w + win_q + win_kv],
                      w[:, s5w + win_q + win_kv:])
    w_cat = jnp.concatenate([wu, wq, _swap_halves(wq), wk, _swap_halves(wk), wv], axis=1).astype(BF16)
    u32, u_tm, q0, k0, v0 = _even_in(rows, x_lat, x_ctx, mods[0], w_cat, cos_t, sin_t, s5w, win_q, win_kv)
    wb, a_b, wc = _s5_weights(s5_lam_re[0], s5_lam_im[0], s5_log_step[0], s5_b_re[0], s5_b_im[0],
                              s5_c_re[0], s5_c_im[0], bsz)
    t_all = seq + n_ctx
    y_tm = _s5(u_tm.reshape(t_all * bsz, s5w), wb, a_b, wc, bsz, n_ctx, seq).reshape(2, t_all, bsz * s5w)
    o_rows = _win_attn(rows, q0, k0, v0, win_sink[0].astype(F32))
    g0, b0 = lnv(0, 0)
    even_args = (u32, y_tm, s5_d[0].reshape(1, s5w), s5_w_glu[0].astype(BF16), s5_b_glu[0].reshape(1, s5w))
    x1, h2, lg = _mix_out(rows, n_all, (x_lat, x_ctx), mods[0], o_rows, even_w_out[0].astype(BF16), g0, b0, rw_hi, rw_lo,
                          even_args)
    g1, b1 = lnv(0, 1)
    x2 = _moe(wide, n_all, x1, h2, lg, mods[0], router_bias.astype(F32), *moe_w(0), g1, b1)

    w = odd_w_in[0]
    wq, wk, wv = w[:, :odd_q], w[:, odd_q:odd_q + odd_kv], w[:, odd_q + odd_kv:]
    w_cat = jnp.concatenate([wq, _swap_halves(wq), wk, _swap_halves(wk), wv], axis=1).astype(BF16)

    def gains(gv, width):
        gs = jnp.concatenate([gv[HEAD_DIM // 2:], gv[:HEAD_DIM // 2]])
        return jnp.stack([jnp.tile(gv, width // HEAD_DIM), jnp.tile(gs, width // HEAD_DIM)]).astype(F32)

    head_mean = jnp.kron(jnp.eye(odd_q // HEAD_DIM, dtype=F32),
                         jnp.full((HEAD_DIM, HEAD_DIM), 1.0 / HEAD_DIM, F32)).astype(BF16)
    q1, k1, v1 = _odd_in(wide, x2, mods[1], w_cat, cos_t, sin_t, gains(odd_q_norm[0], odd_q),
                         gains(odd_k_norm[0], odd_kv), head_mean, odd_q, odd_kv)
    o1 = _dense_attn(q1, k1, v1.T, bsz, seq, n_ctx)
    g0, b0 = lnv(1, 0)
    x1, h2, lg = _mix_out(wide, n_lat, x2, mods[1], o1, odd_w_out[0].astype(BF16), g0, b0, rw_hi, rw_lo)
    g1, b1 = lnv(1, 1)
    out = _moe(wide, n_lat, x1, h2, lg, mods[1], router_bias.astype(F32), *moe_w(1), g1, b1)
    return out.reshape(bsz, seq, d)
```
